```python
import math
import jax, jax.numpy as jnp
from jax import lax
import numpy as np

D_MODEL = 1024
BATCH = 2
SEQ = 8192
DEPTH = 4

GRID_W = 64
CTX_LEN = 256
HEAD_DIM = 64
BRANCH_WIDTH = D_MODEL // 2
FOURIER_GROUP_DIM = 64
FOURIER_GROUPS = BRANCH_WIDTH // FOURIER_GROUP_DIM
CONV_WIDTH = BRANCH_WIDTH
CONV_KERNEL = 31
GQA_Q_HEADS = BRANCH_WIDTH // HEAD_DIM
GQA_GROUP = 4
GQA_KV_HEADS = GQA_Q_HEADS // GQA_GROUP
DIFF_QK_DIM = HEAD_DIM
DIFF_V_DIM = 2 * HEAD_DIM
DIFF_HEADS = BRANCH_WIDTH // DIFF_V_DIM
N_BRANCH = 4
FFN_HIDDEN = -(-8 * D_MODEL // (3 * 256)) * 256
ROPE_BASE = 10000.0
AXIS_ROT_DIM = HEAD_DIM // 2
Q_BLOCK = 128
EPS = 1e-6
LN_EPS = 1e-5

_IN_SIZES = (BRANCH_WIDTH,
             2 * CONV_WIDTH,
             GQA_Q_HEADS * HEAD_DIM,
             GQA_KV_HEADS * HEAD_DIM,
             GQA_KV_HEADS * HEAD_DIM,
             DIFF_HEADS * 2 * DIFF_QK_DIM,
             DIFF_HEADS * 2 * DIFF_QK_DIM,
             DIFF_HEADS * DIFF_V_DIM,
             N_BRANCH * D_MODEL)
IN_COLS = sum(_IN_SIZES)
SPLIT_IDX = tuple(int(v) for v in np.cumsum(_IN_SIZES)[:-1])

kernel_name = "hybrid_gated_dit_block"


def rms_norm(x, g):
    x32 = x.astype(jnp.float32)
    y = x32 * lax.rsqrt(jnp.mean(x32 * x32, axis=-1, keepdims=True) + EPS)
    return (y * g.astype(jnp.float32)).astype(x.dtype)


def layer_norm(x, g, b):
    x32 = x.astype(jnp.float32)
    mu = jnp.mean(x32, axis=-1, keepdims=True)
    var = jnp.mean(jnp.square(x32 - mu), axis=-1, keepdims=True)
    y = (x32 - mu) * lax.rsqrt(var + LN_EPS)
    return (y * g.astype(jnp.float32) + b.astype(jnp.float32)).astype(x.dtype)


def modulate(h, shift, scale):
    return h * (1.0 + scale) + shift


def axial_rope_tables(n_rows):
    row = jnp.repeat(jnp.arange(n_rows, dtype=jnp.float32), GRID_W)
    col = jnp.tile(jnp.arange(GRID_W, dtype=jnp.float32), n_rows)
    inv = ROPE_BASE ** (-jnp.arange(0, AXIS_ROT_DIM, 2, dtype=jnp.float32) / AXIS_ROT_DIM)
    ang = jnp.concatenate([row[:, None] * inv, col[:, None] * inv], axis=-1)
    return jnp.cos(ang), jnp.sin(ang)


def rope(x, cos, sin):
    half = x.shape[-1] // 2
    x32 = x.astype(jnp.float32)
    x1, x2 = x32[..., :half], x32[..., half:]
    return jnp.concatenate([x1 * cos - x2 * sin, x2 * cos + x1 * sin], axis=-1).astype(x.dtype)


def fourier_mix(z):
    b, l, _ = z.shape
    zg = z.reshape(b, l, FOURIER_GROUPS, FOURIER_GROUP_DIM).astype(jnp.float32)
    f = jnp.fft.fft2(zg, axes=(1, 3), norm="ortho").real
    return f.reshape(b, l, BRANCH_WIDTH).astype(z.dtype)


def conformer_conv(z, w, bias, ln_g, ln_b):
    a, g = jnp.split(z, 2, axis=-1)
    u = a * jax.nn.sigmoid(g)
    pad = CONV_KERNEL // 2
    y = lax.conv_general_dilated(u, w[:, None, :].astype(u.dtype), window_strides=(1,),
                                 padding=[(pad, pad)], dimension_numbers=("NWC", "WIO", "NWC"),
                                 feature_group_count=CONV_WIDTH) + bias
    return jax.nn.silu(layer_norm(y, ln_g, ln_b))


def gqa_q(zq, g, cos, sin):
    b, l, _ = zq.shape
    q = rms_norm(zq.reshape(b, l, GQA_Q_HEADS, HEAD_DIM), g).transpose(0, 2, 1, 3)
    if cos is not None:
        q = rope(q, cos, sin)
    return q.reshape(b, GQA_KV_HEADS, GQA_GROUP, l, HEAD_DIM)


def gqa_kv(zk, zv, g, cos, sin):
    b, l, _ = zk.shape
    k = rms_norm(zk.reshape(b, l, GQA_KV_HEADS, HEAD_DIM), g).transpose(0, 2, 1, 3)
    v = zv.reshape(b, l, GQA_KV_HEADS, HEAD_DIM).transpose(0, 2, 1, 3)
    if cos is not None:
        k = rope(k, cos, sin)
    return k, v


def diff_q(zq, cos, sin):
    b, l, _ = zq.shape
    q = zq.reshape(b, l, DIFF_HEADS, 2, DIFF_QK_DIM).transpose(0, 2, 3, 1, 4)
    if cos is not None:
        q = rope(q, cos, sin)
    return q


def diff_kv(zk, zv, cos, sin):
    b, l, _ = zk.shape
    k = zk.reshape(b, l, DIFF_HEADS, 2, DIFF_QK_DIM).transpose(0, 2, 3, 1, 4)
    v = zv.reshape(b, l, DIFF_HEADS, DIFF_V_DIM).transpose(0, 2, 1, 3)
    if cos is not None:
        k = rope(k, cos, sin)
    return k, v


def gqa_attend(q, k, v):
    s = jnp.einsum("bhgqd,bhkd->bhgqk", q, k).astype(jnp.float32) * (HEAD_DIM ** -0.5)
    p = jax.nn.softmax(s, axis=-1).astype(v.dtype)
    return jnp.einsum("bhgqk,bhkd->bhgqd", p, v)


def diff_attend(q, k, v, lam):
    s = jnp.einsum("bhmqd,bhmkd->bhmqk", q, k).astype(jnp.float32) * (DIFF_QK_DIM ** -0.5)
    p = jax.nn.softmax(s, axis=-1)
    w = (p[:, :, 0] - lam * p[:, :, 1]).astype(v.dtype)
    return jnp.einsum("bhqk,bhkd->bhqd", w, v)


def sweep_query_blocks(fn, q):
    *lead, s, d = q.shape
    nb = s // Q_BLOCK
    qb = jnp.moveaxis(q.reshape(*lead, nb, Q_BLOCK, d), -3, 0)
    out = jnp.moveaxis(lax.map(fn, qb), 0, -3)
    return out.reshape(*out.shape[:-3], s, out.shape[-1])


def gqa_merge(o):
    b, _, _, l, d = o.shape
    return o.reshape(b, GQA_Q_HEADS, l, d).transpose(0, 2, 1, 3).reshape(b, l, GQA_Q_HEADS * d)


def diff_merge(o, g, lam_init):
    b, h, l, dv = o.shape
    o = rms_norm(o, g) * (1.0 - lam_init)
    return o.transpose(0, 2, 1, 3).reshape(b, l, h * dv)


def context_keys_values(pc, lp):
    k, v = gqa_kv(pc[3], pc[4], lp["k_norm"], None, None)
    kd, vd = diff_kv(pc[6], pc[7], None, None)
    return (k, v, kd, vd)


def mixer_stream(parts, lp, cos, sin, ctx_kv, lam, lam_init):
    zf, zconv, zgq, zgk, zgv, zdq, zdk, zdv, zgate = parts
    y_four = fourier_mix(zf) @ lp["w_four"]
    y_conv = conformer_conv(zconv, lp["conv_w"], lp["conv_b"], lp["conv_ln_g"], lp["conv_ln_b"]) @ lp["w_conv"]
    q = gqa_q(zgq, lp["q_norm"], cos, sin)
    k, v = gqa_kv(zgk, zgv, lp["k_norm"], cos, sin)
    qd = diff_q(zdq, cos, sin)
    kd, vd = diff_kv(zdk, zdv, cos, sin)
    if ctx_kv is None:
        o_gqa = gqa_attend(q, k, v)
        o_diff = diff_attend(qd, kd, vd, lam)
    else:
        kc, vc, kdc, vdc = ctx_kv
        k_all = jnp.concatenate([k, kc], axis=2)
        v_all = jnp.concatenate([v, vc], axis=2)
        kd_all = jnp.concatenate([kd, kdc], axis=3)
        vd_all = jnp.concatenate([vd, vdc], axis=2)
        o_gqa = sweep_query_blocks(lambda qb: gqa_attend(qb, k_all, v_all), q)
        o_diff = sweep_query_blocks(lambda qb: diff_attend(qb, kd_all, vd_all, lam), qd)
    y_gqa = gqa_merge(o_gqa) @ lp["w_gqa"]
    y_diff = diff_merge(o_diff, lp["diff_norm"], lam_init) @ lp["w_diff"]
    b, l, _ = zgate.shape
    gates = jax.nn.sigmoid(zgate.reshape(b, l, N_BRANCH, D_MODEL))
    ys = jnp.stack([y_four, y_conv, y_gqa, y_diff], axis=2)
    return jnp.sum(gates * ys, axis=2) @ lp["w_out"]


def swiglu(h, w1, w3, w2):
    return (jax.nn.silu(h @ w1) * (h @ w3)) @ w2


def setup_inputs(seed: int = 0) -> dict:
    key = jax.random.key(seed)
    ks = iter(jax.random.split(key, 40))

    def nrm(shape, scale):
        return scale * jax.random.normal(next(ks), shape, jnp.float32)

    def gain(shape):
        return 1.0 + nrm(shape, 0.02)

    L = DEPTH
    return {
        "x": nrm((BATCH, SEQ, D_MODEL), 1.0),
        "c": nrm((BATCH, D_MODEL), 1.0),
        "ctx": nrm((BATCH, CTX_LEN, D_MODEL), 1.0),
        "c_ctx": nrm((D_MODEL,), 1.0),
        "w_ada": nrm((L, D_MODEL, 6 * D_MODEL), 0.5 * D_MODEL ** -0.5),
        "b_ada": nrm((L, 6 * D_MODEL), 0.02),
        "norm_mix": gain((L, D_MODEL)),
        "w_in": nrm((L, D_MODEL, IN_COLS), D_MODEL ** -0.5),
        "w_four": nrm((L, BRANCH_WIDTH, D_MODEL), BRANCH_WIDTH ** -0.5),
        "conv_w": nrm((L, CONV_KERNEL, CONV_WIDTH), CONV_KERNEL ** -0.5),
        "conv_b": nrm((L, CONV_WIDTH), 0.02),
        "conv_ln_g": gain((L, CONV_WIDTH)),
        "conv_ln_b": nrm((L, CONV_WIDTH), 0.02),
        "w_conv": nrm((L, CONV_WIDTH, D_MODEL), CONV_WIDTH ** -0.5),
        "q_norm": gain((L, HEAD_DIM)),
        "k_norm": gain((L, HEAD_DIM)),
        "w_gqa": nrm((L, GQA_Q_HEADS * HEAD_DIM, D_MODEL), (GQA_Q_HEADS * HEAD_DIM) ** -0.5),
        "lam_q1": nrm((L, DIFF_QK_DIM), 0.1),
        "lam_k1": nrm((L, DIFF_QK_DIM), 0.1),
        "lam_q2": nrm((L, DIFF_QK_DIM), 0.1),
        "lam_k2": nrm((L, DIFF_QK_DIM), 0.1),
        "diff_norm": gain((L, DIFF_V_DIM)),
        "w_diff": nrm((L, DIFF_HEADS * DIFF_V_DIM, D_MODEL), (DIFF_HEADS * DIFF_V_DIM) ** -0.5),
        "w_out": nrm((L, D_MODEL, D_MODEL), D_MODEL ** -0.5),
        "norm_ffn": gain((L, D_MODEL)),
        "w_ffn1": nrm((L, D_MODEL, FFN_HIDDEN), D_MODEL ** -0.5),
        "w_ffn3": nrm((L, D_MODEL, FFN_HIDDEN), D_MODEL ** -0.5),
        "w_ffn2": nrm((L, FFN_HIDDEN, D_MODEL), FFN_HIDDEN ** -0.5),
        "final_norm": gain((D_MODEL,)),
    }


def reference(x, c, ctx, c_ctx, w_ada, b_ada, norm_mix, w_in, w_four, conv_w, conv_b, conv_ln_g,
              conv_ln_b, w_conv, q_norm, k_norm, w_gqa, lam_q1, lam_k1, lam_q2, lam_k2, diff_norm,
              w_diff, w_out, norm_ffn, w_ffn1, w_ffn3, w_ffn2, final_norm):
    n_rows = x.shape[1] // GRID_W
    cos, sin = axial_rope_tables(n_rows)
    silu_c = jax.nn.silu(c)
    silu_cc = jax.nn.silu(c_ctx)
    for l in range(DEPTH):
        last = l == DEPTH - 1
        lam_init = 0.8 - 0.6 * math.exp(-0.3 * l)
        lp = dict(w_four=w_four[l], conv_w=conv_w[l], conv_b=conv_b[l], conv_ln_g=conv_ln_g[l],
                  conv_ln_b=conv_ln_b[l], w_conv=w_conv[l], q_norm=q_norm[l], k_norm=k_norm[l],
                  w_gqa=w_gqa[l], diff_norm=diff_norm[l], w_diff=w_diff[l], w_out=w_out[l])
        mod_x = (silu_c @ w_ada[l] + b_ada[l])[:, None, :]
        sh_m, sc_m, g_m, sh_f, sc_f, g_f = jnp.split(mod_x, 6, axis=-1)
        mod_c = silu_cc @ w_ada[l] + b_ada[l]
        csh_m, csc_m, cg_m, csh_f, csc_f, cg_f = jnp.split(mod_c, 6, axis=-1)
        lam = (jnp.exp(jnp.sum(lam_q1[l].astype(jnp.float32) * lam_k1[l].astype(jnp.float32)))
               - jnp.exp(jnp.sum(lam_q2[l].astype(jnp.float32) * lam_k2[l].astype(jnp.float32)))
               + lam_init)
        hx = modulate(rms_norm(x, norm_mix[l]), sh_m, sc_m)
        hc = modulate(rms_norm(ctx, norm_mix[l]), csh_m, csc_m)
        px = jnp.split(hx @ w_in[l], SPLIT_IDX, axis=-1)
        pc = jnp.split(hc @ w_in[l], SPLIT_IDX, axis=-1)
        ctx_kv = context_keys_values(pc, lp)
        x = x + g_m * mixer_stream(px, lp, cos, sin, ctx_kv, lam, lam_init)
        if not last:
            ctx = ctx + cg_m * mixer_stream(pc, lp, None, None, None, lam, lam_init)
            ctx = ctx + cg_f * swiglu(modulate(rms_norm(ctx, norm_ffn[l]), csh_f, csc_f),
                                      w_ffn1[l], w_ffn3[l], w_ffn2[l])
        x = x + g_f * swiglu(modulate(rms_norm(x, norm_ffn[l]), sh_f, sc_f),
                             w_ffn1[l], w_ffn3[l], w_ffn2[l])
    return rms_norm(x, final_norm)
```

```python
import functools
import math

import numpy as np
import jax
import jax.numpy as jnp
from jax import lax
from jax.experimental import pallas as pl
from jax.experimental.pallas import tpu as pltpu

F32 = jnp.float32
BF16 = jnp.bfloat16

HEAD_DIM = 64
BRANCH = 512
GQA_GROUP = 4
GQA_KV_HEADS = 2
DIFF_HEADS = 4
N_BRANCH = 4
GRID_W = 64
CONV_K = 31
ROPE_BASE = 10000.0
EPS = 1e-6
LN_EPS = 1e-5

LANE = 128
BF16_SUBLANES = 16
VMEM_LIMIT = 56 * 1024 * 1024

ZC = 8192
BLK_FOUR, BLK_CA, BLK_CG, BLK_GQ, BLK_DQ, BLK_DK, BLK_DV, BLK_KV = range(8)
GATE_COL0 = 4096
CONV_HALO = 16


def _params(*sem):
    return pltpu.CompilerParams(dimension_semantics=sem, vmem_limit_bytes=VMEM_LIMIT)


def _silu(x):
    return x * jax.nn.sigmoid(x)


def _rms_mod(x, g, shift, scale):
    y = x * lax.rsqrt(jnp.mean(x * x, axis=-1, keepdims=True) + EPS) * g
    return y * (1.0 + scale) + shift


def _ada_kernel(c_ref, w_ref, b_ref, o_ref):
    a = _silu(c_ref[...])
    o_ref[...] = jnp.dot(a, w_ref[...], precision=lax.Precision.HIGHEST,
                         preferred_element_type=F32) + b_ref[...]


def _ada_mod(cond, w_ada, b_ada):
    L, D, N = w_ada.shape
    tn = 1536
    return pl.pallas_call(
        _ada_kernel,
        out_shape=jax.ShapeDtypeStruct((L, 8, N), F32),
        grid=(L, N // tn),
        in_specs=[pl.BlockSpec((8, D), lambda l, j: (0, 0)),
                  pl.BlockSpec((None, D, tn), lambda l, j: (l, 0, j)),
                  pl.BlockSpec((None, 1, tn), lambda l, j: (l, 0, j))],
        out_specs=pl.BlockSpec((None, 8, tn), lambda l, j: (l, 0, j)),
        compiler_params=_params("parallel", "parallel"),
        name="ada_mod",
    )(cond, w_ada, b_ada.reshape(L, 1, N))


def _inproj_kernel(x_ref, g_ref, sh_ref, sc_ref, w_ref, o_ref):
    h = _rms_mod(x_ref[...], g_ref[...], sh_ref[...], sc_ref[...])
    o_ref[...] = jnp.dot(h.astype(BF16), w_ref[...], preferred_element_type=F32).astype(o_ref.dtype)


def _inproj(x, gain, shift, scale, w):
    B, S, D = x.shape
    N = w.shape[1]
    tm = min(256, S)
    tn = 4096
    return pl.pallas_call(
        _inproj_kernel,
        out_shape=jax.ShapeDtypeStruct((B, S, N), BF16),
        grid=(N // tn, B, S // tm),
        in_specs=[pl.BlockSpec((None, tm, D), lambda j, b, i: (b, i, 0)),
                  pl.BlockSpec((1, D), lambda j, b, i: (0, 0)),
                  pl.BlockSpec((None, 1, D), lambda j, b, i: (b, 0, 0)),
                  pl.BlockSpec((None, 1, D), lambda j, b, i: (b, 0, 0)),
                  pl.BlockSpec((D, tn), lambda j, b, i: (0, j))],
        out_specs=pl.BlockSpec((None, tm, tn), lambda j, b, i: (b, i, j)),
        compiler_params=_params("parallel", "parallel", "parallel"),
        name="in_proj",
    )(x, gain, shift, scale, w)


def _ffn_kernel(x_ref, g_ref, sh_ref, sc_ref, gate_ref, w1_ref, w3_ref, w2_ref, fg_ref, o_ref, *, final):
    x = x_ref[...]
    h = _rms_mod(x, g_ref[...], sh_ref[...], sc_ref[...]).astype(BF16)
    a = jnp.dot(h, w1_ref[...], preferred_element_type=F32)
    b = jnp.dot(h, w3_ref[...], preferred_element_type=F32)
    u = (_silu(a) * b).astype(BF16)
    y = x + gate_ref[...] * jnp.dot(u, w2_ref[...], preferred_element_type=F32)
    if final:
        y = y * lax.rsqrt(jnp.mean(y * y, axis=-1, keepdims=True) + EPS) * fg_ref[...]
    o_ref[...] = y


def _ffn(x, gain, shift, scale, gate, w1, w3, w2, final_gain, final):
    B, S, D = x.shape
    H = w1.shape[1]
    tm = min(256, S)
    const = lambda shape: pl.BlockSpec(shape, lambda b, i: (0,) * len(shape), pipeline_mode=pl.Buffered(1))
    per_b = pl.BlockSpec((None, 1, D), lambda b, i: (b, 0, 0))
    return pl.pallas_call(
        functools.partial(_ffn_kernel, final=final),
        out_shape=jax.ShapeDtypeStruct((B, S, D), F32),
        grid=(B, S // tm),
        in_specs=[pl.BlockSpec((None, tm, D), lambda b, i: (b, i, 0)),
                  const((1, D)), per_b, per_b, per_b,
                  const((D, H)), const((D, H)), const((H, D)), const((1, D))],
        out_specs=pl.BlockSpec((None, tm, D), lambda b, i: (b, i, 0)),
        compiler_params=_params("parallel", "parallel"),
        name="ffn",
    )(x, gain, shift, scale, gate, w1, w3, w2, final_gain)


def _merge_kernel(x_ref, gm_ref, f_ref, c_ref, a_ref, d_ref, zg_ref,
                  wf_ref, wc_ref, wa_ref, wd_ref, wo_ref, o_ref):
    D = x_ref.shape[-1]
    acc = None
    for b, (br, w) in enumerate(((f_ref, wf_ref), (c_ref, wc_ref), (a_ref, wa_ref), (d_ref, wd_ref))):
        y = jnp.dot(br[...], w[...], preferred_element_type=F32)
        gate = jax.nn.sigmoid(zg_ref[:, b * D:(b + 1) * D].astype(F32))
        acc = gate * y if acc is None else acc + gate * y
    out = jnp.dot(acc.astype(BF16), wo_ref[...], preferred_element_type=F32)
    o_ref[...] = x_ref[...] + gm_ref[...] * out


def _merge(x, gm, f, cv, og, od, z, wf, wc, wa, wd, wo):
    B, S, D = x.shape
    tm = min(256, S)
    row = lambda width: pl.BlockSpec((None, tm, width), lambda b, i: (b, i, 0))
    const = lambda shape: pl.BlockSpec(shape, lambda b, i: (0, 0))
    return pl.pallas_call(
        _merge_kernel,
        out_shape=jax.ShapeDtypeStruct((B, S, D), F32),
        grid=(B, S // tm),
        in_specs=[row(D), pl.BlockSpec((None, 1, D), lambda b, i: (b, 0, 0)),
                  row(BRANCH), row(BRANCH), row(BRANCH), row(BRANCH),
                  pl.BlockSpec((None, tm, N_BRANCH * D), lambda b, i: (b, i, GATE_COL0 // (N_BRANCH * D))),
                  const((BRANCH, D)), const((BRANCH, D)), const((BRANCH, D)), const((BRANCH, D)),
                  const((D, D))],
        out_specs=row(D),
        compiler_params=_params("parallel", "parallel"),
        name="merge",
    )(x, gm, f, cv, og, od, z, wf, wc, wa, wd, wo)


def _conv_kernel(a_ref, g_ref, ap_ref, gp_ref, an_ref, gn_ref, w_ref, b_ref, lg_ref, lb_ref, o_ref, u_s,
                 *, ts, chunk):
    i = pl.program_id(1)
    n = pl.num_programs(1)

    def glu(a, g):
        return a.astype(F32) * jax.nn.sigmoid(g.astype(F32))

    u_s[CONV_HALO:CONV_HALO + ts, :] = glu(a_ref[...], g_ref[...])
    u_s[0:CONV_HALO, :] = jnp.where(i > 0, glu(ap_ref[...], gp_ref[...]), 0.0)
    u_s[CONV_HALO + ts:2 * CONV_HALO + ts, :] = jnp.where(i < n - 1, glu(an_ref[...], gn_ref[...]), 0.0)
    w = w_ref[...]
    pad = CONV_K // 2

    def body(c, carry):
        r0 = pl.multiple_of(c * chunk, chunk)
        win = u_s[pl.ds(r0, chunk + 2 * CONV_HALO), :]
        acc = jnp.zeros((chunk, BRANCH), F32) + b_ref[...]
        for j in range(CONV_K):
            off = CONV_HALO - pad + j
            acc = acc + w[j:j + 1, :] * win[off:off + chunk, :]
        mu = jnp.mean(acc, axis=-1, keepdims=True)
        d = acc - mu
        var = jnp.mean(d * d, axis=-1, keepdims=True)
        y = d * lax.rsqrt(var + LN_EPS) * lg_ref[...] + lb_ref[...]
        o_ref[pl.ds(r0, chunk), :] = _silu(y).astype(o_ref.dtype)
        return carry

    lax.fori_loop(0, ts // chunk, body, 0)


def _conv_branch(z, w, b, ln_g, ln_b):
    B, S, _ = z.shape
    ts = min(512, S)
    chunk = 32
    hb = ts // CONV_HALO
    last = S // CONV_HALO - 1
    cur = lambda blk: pl.BlockSpec((None, ts, BRANCH), lambda b_, i: (b_, i, blk))
    prev = lambda blk: pl.BlockSpec((None, CONV_HALO, BRANCH),
                                    lambda b_, i: (b_, jnp.maximum(i * hb - 1, 0), blk))
    nxt = lambda blk: pl.BlockSpec((None, CONV_HALO, BRANCH),
                                   lambda b_, i: (b_, jnp.minimum((i + 1) * hb, last), blk))
    const = lambda shape: pl.BlockSpec(shape, lambda b_, i: (0, 0))
    return pl.pallas_call(
        functools.partial(_conv_kernel, ts=ts, chunk=chunk),
        out_shape=jax.ShapeDtypeStruct((B, S, BRANCH), BF16),
        grid=(B, S // ts),
        in_specs=[cur(BLK_CA), cur(BLK_CG), prev(BLK_CA), prev(BLK_CG), nxt(BLK_CA), nxt(BLK_CG),
                  const((CONV_K, BRANCH)), const((1, BRANCH)), const((1, BRANCH)), const((1, BRANCH))],
        out_specs=pl.BlockSpec((None, ts, BRANCH), lambda b_, i: (b_, i, 0)),
        scratch_shapes=[pltpu.VMEM((ts + 2 * CONV_HALO, BRANCH), F32)],
        compiler_params=_params("parallel", "parallel"),
        name="conv_branch",
    )(z, z, z, z, z, z, w, b, ln_g, ln_b)


def _head_rms(x, ones_bd):
    x2 = x * x
    hi = x2.astype(BF16)
    lo = (x2 - hi.astype(F32)).astype(BF16)
    ssum = (jnp.dot(hi, ones_bd, preferred_element_type=F32)
            + jnp.dot(lo, ones_bd, preferred_element_type=F32))
    return x * lax.rsqrt(ssum * (1.0 / HEAD_DIM) + EPS)


def _rope(x, cosf, sins):
    width = x.shape[1]
    reps = width // LANE
    c = jnp.concatenate([cosf] * reps, axis=1) if reps > 1 else cosf
    s = jnp.concatenate([sins] * reps, axis=1) if reps > 1 else sins
    lane = lax.broadcasted_iota(jnp.int32, x.shape, 1)
    first_half = (lane & (HEAD_DIM - 1)) < HEAD_DIM // 2
    partner = jnp.where(first_half, pltpu.roll(x, width - HEAD_DIM // 2, 1), pltpu.roll(x, HEAD_DIM // 2, 1))
    return x * c + partner * s


def _prep_kernel(gq_ref, dq_ref, dk_ref, kv_ref, cos_ref, sin_ref, qn_ref, kn_ref, bd_ref,
                 qg_ref, kg_ref, vg_ref, qd_ref, kd_ref):
    cosf, sins = cos_ref[...], sin_ref[...]
    bd = bd_ref[...]
    scale = HEAD_DIM ** -0.5
    lane = lax.broadcasted_iota(jnp.int32, (gq_ref.shape[0], LANE), 1)
    low = lane < HEAD_DIM
    zero = jnp.zeros((gq_ref.shape[0], LANE), F32)

    q = _rope(_head_rms(gq_ref[...].astype(F32), bd) * qn_ref[...], cosf, sins) * scale
    heads_per_kv = GQA_GROUP
    for p in range(BRANCH // LANE):
        src = q[:, p * LANE:(p + 1) * LANE]
        swapped = pltpu.roll(src, HEAD_DIM, 1)
        for e in range(2):
            h = 2 * p + e
            kvh = h // heads_per_kv
            val = src if e == kvh else swapped
            val = jnp.where(low, val, zero) if kvh == 0 else jnp.where(low, zero, val)
            qg_ref[:, h * LANE:(h + 1) * LANE] = val.astype(qg_ref.dtype)

    kvz = kv_ref[...].astype(F32)
    k = _rope(_head_rms(kvz[:, :LANE], bd[:LANE, :LANE]) * kn_ref[:, :LANE], cosf, sins)
    kg_ref[...] = k.astype(kg_ref.dtype)
    v = kvz[:, LANE:2 * LANE]
    vg_ref[:, :LANE] = jnp.where(low, v, zero).astype(vg_ref.dtype)
    vg_ref[:, LANE:] = jnp.where(low, pltpu.roll(v, HEAD_DIM, 1), zero).astype(vg_ref.dtype)

    qd = _rope(dq_ref[...].astype(F32), cosf, sins) * scale
    for h in range(DIFF_HEADS):
        src = qd[:, h * LANE:(h + 1) * LANE]
        qd_ref[:, (2 * h) * LANE:(2 * h + 1) * LANE] = jnp.where(low, src, zero).astype(qd_ref.dtype)
        qd_ref[:, (2 * h + 1) * LANE:(2 * h + 2) * LANE] = jnp.where(low, zero, src).astype(qd_ref.dtype)
    kd_ref[...] = _rope(dk_ref[...].astype(F32), cosf, sins).astype(kd_ref.dtype)


def _prep(z, cosf, sins, qn, kn, ones_bd):
    B, S, _ = z.shape
    ts = min(256, S)
    slab = lambda blk: pl.BlockSpec((None, ts, BRANCH), lambda b, i: (b, i, blk))
    tab = pl.BlockSpec((ts, LANE), lambda b, i: (i, 0))
    const = lambda shape: pl.BlockSpec(shape, lambda b, i: (0, 0))
    out = lambda width: pl.BlockSpec((None, ts, width), lambda b, i: (b, i, 0))
    widths = (2 * BRANCH, LANE, 2 * LANE, 2 * BRANCH, BRANCH)
    return pl.pallas_call(
        _prep_kernel,
        out_shape=[jax.ShapeDtypeStruct((B, S, w), BF16) for w in widths],
        grid=(B, S // ts),
        in_specs=[slab(BLK_GQ), slab(BLK_DQ), slab(BLK_DK), slab(BLK_KV), tab, tab,
                  const((1, BRANCH)), const((1, BRANCH)), const((BRANCH, BRANCH))],
        out_specs=[out(w) for w in widths],
        compiler_params=_params("parallel", "parallel"),
        name="qk_prep",
    )(z, z, z, z, cosf, sins, qn, kn, ones_bd)


def _attn_kernel(*refs, sets, tq, mode, lam_init):
    if mode == "diff":
        q_ref, k_ref, v_ref, l1q, l1k, l2q, l2k, dn_ref, o_ref, q_s, m_s, l_s, acc_s = refs
    else:
        q_ref, k_ref, v_ref, o_ref, q_s, m_s, l_s, acc_s = refs
    j = pl.program_id(3)
    tk = k_ref.shape[0]

    @pl.when(j == 0)
    def _init():
        for r in range(sets):
            q_s[r * tq:(r + 1) * tq, :] = q_ref[:, r * LANE:(r + 1) * LANE]
        m_s[...] = jnp.full(m_s.shape, -jnp.inf, F32)
        l_s[...] = jnp.zeros(l_s.shape, F32)
        acc_s[...] = jnp.zeros(acc_s.shape, F32)

    s = lax.dot_general(q_s[...], k_ref[...], (((1,), (1,)), ((), ())), preferred_element_type=F32)
    m_prev = m_s[...]
    m_new = jnp.maximum(m_prev, jnp.max(s, axis=1, keepdims=True))
    alpha = jnp.exp(m_prev - m_new)
    p = jnp.exp(s - jnp.concatenate([m_new] * (tk // LANE), axis=1))
    l_s[...] = alpha * l_s[...] + jnp.sum(p, axis=1, keepdims=True)
    acc_s[...] = alpha * acc_s[...] + jnp.dot(p.astype(BF16), v_ref[...], preferred_element_type=F32)
    m_s[...] = m_new

    @pl.when(j == pl.num_programs(3) - 1)
    def _finish():
        o = [acc_s[r * tq:(r + 1) * tq, :] / l_s[r * tq:(r + 1) * tq, :] for r in range(sets)]
        if mode == "diff":
            lam = (jnp.exp(jnp.sum(l1q[...] * l1k[...], axis=1, keepdims=True))
                   - jnp.exp(jnp.sum(l2q[...] * l2k[...], axis=1, keepdims=True)) + lam_init)
            d = o[0] - lam * o[1]
            d = d * lax.rsqrt(jnp.mean(d * d, axis=1, keepdims=True) + EPS) * dn_ref[...]
            o_ref[...] = (d * (1.0 - lam_init)).astype(o_ref.dtype)
        else:
            lane = lax.broadcasted_iota(jnp.int32, (tq, LANE), 1)
            for pair in range(sets // 2):
                packed = jnp.where(lane < HEAD_DIM, o[2 * pair], pltpu.roll(o[2 * pair + 1], HEAD_DIM, 1))
                o_ref[:, pair * LANE:(pair + 1) * LANE] = packed.astype(o_ref.dtype)


def _attention(q, k, v, mode, lam_vecs=None, diff_norm=None, lam_init=0.0):
    B, S, _ = q.shape
    NK = k.shape[1]
    tq = min(256, S)
    tk = 1408 if NK % 1408 == 0 else 256
    if mode == "diff":
        groups, sets, out_w = DIFF_HEADS, 2, LANE
        kmap = lambda b, g, i, j: (b, j, g)
    else:
        groups, sets, out_w = GQA_KV_HEADS, GQA_GROUP, GQA_GROUP * HEAD_DIM
        kmap = lambda b, g, i, j: (b, j, 0)
    in_specs = [pl.BlockSpec((None, tq, sets * LANE), lambda b, g, i, j: (b, i, g)),
                pl.BlockSpec((None, tk, LANE), kmap),
                pl.BlockSpec((None, tk, LANE), lambda b, g, i, j: (b, j, g))]
    args = [q, k, v]
    if mode == "diff":
        in_specs += [pl.BlockSpec((1, HEAD_DIM), lambda b, g, i, j: (0, 0))] * 4
        in_specs += [pl.BlockSpec((1, LANE), lambda b, g, i, j: (0, 0))]
        args += list(lam_vecs) + [diff_norm]
    return pl.pallas_call(
        functools.partial(_attn_kernel, sets=sets, tq=tq, mode=mode, lam_init=lam_init),
        out_shape=jax.ShapeDtypeStruct((B, S, BRANCH), BF16),
        grid=(B, groups, S // tq, NK // tk),
        in_specs=in_specs,
        out_specs=pl.BlockSpec((None, tq, out_w), lambda b, g, i, j: (b, i, g)),
        scratch_shapes=[pltpu.VMEM((sets * tq, LANE), BF16),
                        pltpu.VMEM((sets * tq, LANE), F32),
                        pltpu.VMEM((sets * tq, LANE), F32),
                        pltpu.VMEM((sets * tq, LANE), F32)],
        compiler_params=_params("parallel", "parallel", "parallel", "arbitrary"),
        name="attn_" + mode,
    )(*args)


def _dft_cos_sin(n):
    idx = np.arange(n)
    ang = 2.0 * np.pi * ((idx[:, None] * idx[None, :]) % n) / n
    return np.cos(ang), np.sin(ang)


def _channel_dft_matrix():
    c, s = _dft_cos_sin(GRID_W)
    groups = BRANCH // GRID_W
    eye = np.eye(groups)
    return np.concatenate([np.kron(eye, c), -np.kron(eye, s)], axis=1) / math.sqrt(GRID_W)


def _fourier_a_kernel(z_ref, w0_ref, d1_ref, tc_ref, ts_ref, o_ref):
    n1 = z_ref.shape[0]
    u = jnp.dot(z_ref[...], w0_ref[...], preferred_element_type=F32).astype(BF16)
    p = jnp.dot(d1_ref[...], u, preferred_element_type=F32)
    vr = p[:n1, :BRANCH] + p[n1:, BRANCH:]
    vi = p[:n1, BRANCH:] - p[n1:, :BRANCH]
    reps = BRANCH // LANE
    c = jnp.concatenate([tc_ref[...]] * reps, axis=1)
    s = jnp.concatenate([ts_ref[...]] * reps, axis=1)
    o_ref[:, :BRANCH] = (vr * c + vi * s).astype(o_ref.dtype)
    o_ref[:, BRANCH:] = (vi * c - vr * s).astype(o_ref.dtype)


def _bmm_kernel(a_ref, x_ref, o_ref):
    o_ref[...] = jnp.dot(a_ref[...], x_ref[...], preferred_element_type=F32).astype(o_ref.dtype)


def _bmm(a, x, tn):
    M, K = a.shape
    B, _, N = x.shape
    tn = min(tn, N)
    return pl.pallas_call(
        _bmm_kernel,
        out_shape=jax.ShapeDtypeStruct((B, M, N), BF16),
        grid=(B, N // tn),
        in_specs=[pl.BlockSpec((M, K), lambda b, j: (0, 0)),
                  pl.BlockSpec((None, K, tn), lambda b, j: (b, 0, j))],
        out_specs=pl.BlockSpec((None, M, tn), lambda b, j: (b, 0, j)),
        compiler_params=_params("parallel", "parallel"),
        name="bmm",
    )(a, x)


def _slab_matmul_kernel(z_ref, w_ref, o_ref):
    o_ref[...] = jnp.dot(z_ref[...], w_ref[...], preferred_element_type=F32).astype(o_ref.dtype)


def _slab_matmul(z, blk, w):
    B, S, _ = z.shape
    K, N = w.shape
    ts = min(256, S)
    return pl.pallas_call(
        _slab_matmul_kernel,
        out_shape=jax.ShapeDtypeStruct((B, S, N), BF16),
        grid=(B, S // ts),
        in_specs=[pl.BlockSpec((None, ts, K), lambda b, i: (b, i, blk)),
                  pl.BlockSpec((K, N), lambda b, i: (0, 0))],
        out_specs=pl.BlockSpec((None, ts, N), lambda b, i: (b, i, 0)),
        compiler_params=_params("parallel", "parallel"),
        name="slab_matmul",
    )(z, w)


def _fourier_long(z, w0):
    B, S, _ = z.shape
    n2 = GRID_W
    n1 = S // n2
    c1, s1 = _dft_cos_sin(n1)
    d1 = jnp.asarray(np.concatenate([c1, s1], axis=0) / math.sqrt(n1), F32).astype(BF16)
    ang = 2.0 * np.pi * (np.arange(n2)[:, None] * np.arange(n1)[None, :]) / S
    twc = jnp.asarray(np.repeat(np.cos(ang)[:, :, None], LANE, axis=2), F32)
    tws = jnp.asarray(np.repeat(np.sin(ang)[:, :, None], LANE, axis=2), F32)
    zc_blocks = ZC // BRANCH
    v = pl.pallas_call(
        _fourier_a_kernel,
        out_shape=jax.ShapeDtypeStruct((B, n1, n2 * 2 * BRANCH), BF16),
        grid=(B, n2),
        in_specs=[pl.BlockSpec((None, n1, BRANCH), lambda b, j: (b, 0, j * zc_blocks + BLK_FOUR)),
                  pl.BlockSpec((BRANCH, 2 * BRANCH), lambda b, j: (0, 0)),
                  pl.BlockSpec((2 * n1, n1), lambda b, j: (0, 0)),
                  pl.BlockSpec((None, n1, LANE), lambda b, j: (j, 0, 0)),
                  pl.BlockSpec((None, n1, LANE), lambda b, j: (j, 0, 0))],
        out_specs=pl.BlockSpec((None, n1, 2 * BRANCH), lambda b, j: (b, 0, j)),
        compiler_params=_params("parallel", "parallel"),
        name="fourier_a",
    )(z.reshape(B, n1, n2 * ZC), w0, d1, twc, tws)
    vst = jnp.transpose(v.reshape(B, n1, n2, 2, BRANCH), (0, 3, 2, 1, 4)).reshape(B, 2 * n2, n1 * BRANCH)
    c3, s3 = _dft_cos_sin(n2)
    d3 = jnp.asarray(np.concatenate([c3, s3], axis=1) / math.sqrt(n2), F32).astype(BF16)
    f = _bmm(d3, vst, 8192)
    return f.reshape(B, S, BRANCH)


def _fourier_short(z, w0):
    B, S, _ = z.shape
    u = _slab_matmul(z, BLK_FOUR, w0)
    ust = jnp.concatenate([u[:, :, :BRANCH], u[:, :, BRANCH:]], axis=1)
    c, s = _dft_cos_sin(S)
    dl = jnp.asarray(np.concatenate([c, s], axis=1) / math.sqrt(S), F32).astype(BF16)
    return _bmm(dl, ust, BRANCH)


def _rope_tables(n_rows):
    half = HEAD_DIM // 4
    inv = ROPE_BASE ** (-np.arange(0, HEAD_DIM // 2, 2, dtype=np.float64) / (HEAD_DIM // 2))
    row = np.repeat(np.arange(n_rows, dtype=np.float64), GRID_W)
    col = np.tile(np.arange(GRID_W, dtype=np.float64), n_rows)
    ang = np.concatenate([row[:, None] * inv, col[:, None] * inv], axis=-1)
    assert ang.shape[1] == 2 * half
    cos, sin = np.cos(ang), np.sin(ang)
    cosf = np.tile(np.concatenate([cos, cos], axis=1), (1, LANE // HEAD_DIM))
    sins = np.tile(np.concatenate([-sin, sin], axis=1), (1, LANE // HEAD_DIM))
    return jnp.asarray(cosf, F32), jnp.asarray(sins, F32)


def _permute_w_in(w):
    D = w.shape[0]
    four, conv, gq, gk, gv, dq, dk, dv, gates = jnp.split(
        w, np.cumsum([BRANCH, 2 * BRANCH, BRANCH, LANE, LANE, BRANCH, BRANCH, BRANCH])[:], axis=1)
    pad = jnp.zeros((D, BRANCH - 2 * LANE), w.dtype)
    out = jnp.concatenate([four, conv, gq, dq, dk, dv, gk, gv, pad, gates], axis=1)
    assert out.shape[1] == ZC
    return out.astype(BF16)


def _mixer_branches(z, lw, rope_tabs, ones_bd, w0, long_seq):
    f = _fourier_long(z, w0) if long_seq else _fourier_short(z, w0)
    cv = _conv_branch(z, lw["conv_w"], lw["conv_b"], lw["conv_ln_g"], lw["conv_ln_b"])
    qg, kg, vg, qd, kd = _prep(z, rope_tabs[0], rope_tabs[1], lw["q_norm"], lw["k_norm"], ones_bd)
    vd = z[:, :, BLK_DV * BRANCH:(BLK_DV + 1) * BRANCH]
    return f, cv, qg, kg, vg, qd, kd, vd


def kernel(x, c, ctx, c_ctx, w_ada, b_ada, norm_mix, w_in, w_four, conv_w, conv_b, conv_ln_g, conv_ln_b, w_conv, q_norm, k_norm, w_gqa, lam_q1, lam_k1, lam_q2, lam_k2, diff_norm, w_diff, w_out, norm_ffn, w_ffn1, w_ffn3, w_ffn2, final_norm):
    B, S, D = x.shape
    Sc = ctx.shape[1]
    depth = w_ada.shape[0]
    assert B + 1 <= 8 and S % GRID_W == 0 and D == 2 * BRANCH

    cond = jnp.zeros((8, D), F32).at[:B].set(c).at[B].set(c_ctx)
    mods = _ada_mod(cond, w_ada, b_ada)

    rope_x = _rope_tables(S // GRID_W)
    rope_c = (jnp.ones((Sc, LANE), F32), jnp.zeros((Sc, LANE), F32))
    ones_bd = jnp.asarray(np.kron(np.eye(BRANCH // HEAD_DIM), np.ones((HEAD_DIM, HEAD_DIM))), F32).astype(BF16)
    w0 = jnp.asarray(_channel_dft_matrix(), F32).astype(BF16)
    tile2 = lambda v: jnp.tile(v, LANE // HEAD_DIM * (BRANCH // LANE)).reshape(1, BRANCH)

    for l in range(depth):
        last = l == depth - 1
        lam_init = 0.8 - 0.6 * math.exp(-0.3 * l)
        mx = [m.reshape(B, 1, D) for m in jnp.split(mods[l, :B], 6, axis=-1)]
        mc = [jnp.broadcast_to(m.reshape(1, 1, D), (B, 1, D)) for m in jnp.split(mods[l, B], 6, axis=-1)]
        lw = dict(conv_w=conv_w[l], conv_b=conv_b[l].reshape(1, BRANCH),
                  conv_ln_g=conv_ln_g[l].reshape(1, BRANCH), conv_ln_b=conv_ln_b[l].reshape(1, BRANCH),
                  q_norm=tile2(q_norm[l]), k_norm=tile2(k_norm[l]))
        w_in_l = _permute_w_in(w_in[l])
        wf, wc, wa, wd, wo = (w.astype(BF16) for w in (w_four[l], w_conv[l], w_gqa[l], w_diff[l], w_out[l]))
        w1, w3, w2 = (w.astype(BF16) for w in (w_ffn1[l], w_ffn3[l], w_ffn2[l]))
        gain_m, gain_f = norm_mix[l].reshape(1, D), norm_ffn[l].reshape(1, D)
        lam_vecs = [v[l].reshape(1, HEAD_DIM) for v in (lam_q1, lam_k1, lam_q2, lam_k2)]
        dn = diff_norm[l].reshape(1, LANE)
        fin = final_norm.reshape(1, D)

        zx = _inproj(x, gain_m, mx[0], mx[1], w_in_l)
        zc = _inproj(ctx, gain_m, mc[0], mc[1], w_in_l)
        fx, cvx, qgx, kgx, vgx, qdx, kdx, vdx = _mixer_branches(zx, lw, rope_x, ones_bd, w0, True)
        fc, cvc, qgc, kgc, vgc, qdc, kdc, vdc = _mixer_branches(zc, lw, rope_c, ones_bd, w0, False)

        cat = lambda a, b: jnp.concatenate([a, b], axis=1)
        ogx = _attention(qgx, cat(kgx, kgc), cat(vgx, vgc), "gqa")
        odx = _attention(qdx, cat(kdx, kdc), cat(vdx, vdc), "diff", lam_vecs, dn, lam_init)
        x = _merge(x, mx[2], fx, cvx, ogx, odx, zx, wf, wc, wa, wd, wo)
        if not last:
            ogc = _attention(qgc, kgc, vgc, "gqa")
            odc = _attention(qdc, kdc, vdc, "diff", lam_vecs, dn, lam_init)
            ctx = _merge(ctx, mc[2], fc, cvc, ogc, odc, zc, wf, wc, wa, wd, wo)
            ctx = _ffn(ctx, gain_f, mc[3], mc[4], mc[5], w1, w3, w2, fin, False)
        x = _ffn(x, gain_f, mx[3], mx[4], mx[5], w1, w3, w2, fin, last)
    return x
```

```python
import functools
import math

import numpy as np
import jax
import jax.numpy as jnp
from jax import lax
from jax.experimental import pallas as pl
from jax.experimental.pallas import tpu as pltpu

F32 = jnp.float32
BF16 = jnp.bfloat16

HEAD_DIM = 64
BRANCH = 512
GQA_GROUP = 4
GQA_KV_HEADS = 2
DIFF_HEADS = 4
N_BRANCH = 4
GRID_W = 64
CONV_K = 31
ROPE_BASE = 10000.0
EPS = 1e-6
LN_EPS = 1e-5

LANE = 128
BF16_SUBLANES = 16
VMEM_LIMIT = 56 * 1024 * 1024

ZC = 8192
BLK_FOUR, BLK_CA, BLK_CG, BLK_GQ, BLK_DQ, BLK_DK, BLK_DV, BLK_KV = range(8)
GATE_COL0 = 4096
CONV_HALO = 16


def _params(*sem):
    return pltpu.CompilerParams(dimension_semantics=sem, vmem_limit_bytes=VMEM_LIMIT)


def _silu(x):
    return x * jax.nn.sigmoid(x)


def _rms_mod(x, g, shift, scale):
    y = x * lax.rsqrt(jnp.mean(x * x, axis=-1, keepdims=True) + EPS) * g
    return y * (1.0 + scale) + shift


def _ada_kernel(c_ref, w_ref, b_ref, o_ref):
    a = _silu(c_ref[...])
    o_ref[...] = jnp.dot(a, w_ref[...], precision=lax.Precision.HIGHEST,
                         preferred_element_type=F32) + b_ref[...]


def _ada_mod(cond, w_ada, b_ada):
    L, D, N = w_ada.shape
    tn = 1536
    return pl.pallas_call(
        _ada_kernel,
        out_shape=jax.ShapeDtypeStruct((L, 8, N), F32),
        grid=(L, N // tn),
        in_specs=[pl.BlockSpec((8, D), lambda l, j: (0, 0)),
                  pl.BlockSpec((None, D, tn), lambda l, j: (l, 0, j)),
                  pl.BlockSpec((None, 1, tn), lambda l, j: (l, 0, j))],
        out_specs=pl.BlockSpec((None, 8, tn), lambda l, j: (l, 0, j)),
        compiler_params=_params("parallel", "parallel"),
        name="ada_mod",
    )(cond, w_ada, b_ada.reshape(L, 1, N))


def _inproj_kernel(x_ref, g_ref, sh_ref, sc_ref, w_ref, o_ref):
    h = _rms_mod(x_ref[...], g_ref[...], sh_ref[...], sc_ref[...])
    o_ref[...] = jnp.dot(h.astype(BF16), w_ref[...], preferred_element_type=F32).astype(o_ref.dtype)


def _inproj(x, gain, shift, scale, w):
    B, S, D = x.shape
    N = w.shape[1]
    tm = min(256, S)
    tn = 4096
    return pl.pallas_call(
        _inproj_kernel,
        out_shape=jax.ShapeDtypeStruct((B, S, N), BF16),
        grid=(N // tn, B, S // tm),
        in_specs=[pl.BlockSpec((None, tm, D), lambda j, b, i: (b, i, 0)),
                  pl.BlockSpec((1, D), lambda j, b, i: (0, 0)),
                  pl.BlockSpec((None, 1, D), lambda j, b, i: (b, 0, 0)),
                  pl.BlockSpec((None, 1, D), lambda j, b, i: (b, 0, 0)),
                  pl.BlockSpec((D, tn), lambda j, b, i: (0, j))],
        out_specs=pl.BlockSpec((None, tm, tn), lambda j, b, i: (b, i, j)),
        compiler_params=_params("parallel", "parallel", "parallel"),
        name="in_proj",
    )(x, gain, shift, scale, w)


def _ffn_kernel(x_ref, g_ref, sh_ref, sc_ref, gate_ref, w1_ref, w3_ref, w2_ref, fg_ref, o_ref, *, final):
    x = x_ref[...]
    h = _rms_mod(x, g_ref[...], sh_ref[...], sc_ref[...]).astype(BF16)
    a = jnp.dot(h, w1_ref[...], preferred_element_type=F32)
    b = jnp.dot(h, w3_ref[...], preferred_element_type=F32)
    u = (_silu(a) * b).astype(BF16)
    y = x + gate_ref[...] * jnp.dot(u, w2_ref[...], preferred_element_type=F32)
    if final:
        y = y * lax.rsqrt(jnp.mean(y * y, axis=-1, keepdims=True) + EPS) * fg_ref[...]
    o_ref[...] = y


def _ffn(x, gain, shift, scale, gate, w1, w3, w2, final_gain, final):
    B, S, D = x.shape
    H = w1.shape[1]
    tm = min(256, S)
    const = lambda shape: pl.BlockSpec(shape, lambda b, i: (0,) * len(shape), pipeline_mode=pl.Buffered(1))
    per_b = pl.BlockSpec((None, 1, D), lambda b, i: (b, 0, 0))
    return pl.pallas_call(
        functools.partial(_ffn_kernel, final=final),
        out_shape=jax.ShapeDtypeStruct((B, S, D), F32),
        grid=(B, S // tm),
        in_specs=[pl.BlockSpec((None, tm, D), lambda b, i: (b, i, 0)),
                  const((1, D)), per_b, per_b, per_b,
                  const((D, H)), const((D, H)), const((H, D)), const((1, D))],
        out_specs=pl.BlockSpec((None, tm, D), lambda b, i: (b, i, 0)),
        compiler_params=_params("parallel", "parallel"),
        name="ffn",
    )(x, gain, shift, scale, gate, w1, w3, w2, final_gain)


def _merge_kernel(x_ref, gm_ref, f_ref, c_ref, a_ref, d_ref, zg_ref,
                  wf_ref, wc_ref, wa_ref, wd_ref, wo_ref, o_ref):
    D = x_ref.shape[-1]
    acc = None
    for b, (br, w) in enumerate(((f_ref, wf_ref), (c_ref, wc_ref), (a_ref, wa_ref), (d_ref, wd_ref))):
        y = jnp.dot(br[...], w[...], preferred_element_type=F32)
        gate = jax.nn.sigmoid(zg_ref[:, b * D:(b + 1) * D].astype(F32))
        acc = gate * y if acc is None else acc + gate * y
    out = jnp.dot(acc.astype(BF16), wo_ref[...], preferred_element_type=F32)
    o_ref[...] = x_ref[...] + gm_ref[...] * out


def _merge(x, gm, f, cv, og, od, z, wf, wc, wa, wd, wo):
    B, S, D = x.shape
    tm = min(256, S)
    row = lambda width: pl.BlockSpec((None, tm, width), lambda b, i: (b, i, 0))
    const = lambda shape: pl.BlockSpec(shape, lambda b, i: (0, 0))
    return pl.pallas_call(
        _merge_kernel,
        out_shape=jax.ShapeDtypeStruct((B, S, D), F32),
        grid=(B, S // tm),
        in_specs=[row(D), pl.BlockSpec((None, 1, D), lambda b, i: (b, 0, 0)),
                  row(BRANCH), row(BRANCH), row(BRANCH), row(BRANCH),
                  pl.BlockSpec((None, tm, N_BRANCH * D), lambda b, i: (b, i, GATE_COL0 // (N_BRANCH * D))),
                  const((BRANCH, D)), const((BRANCH, D)), const((BRANCH, D)), const((BRANCH, D)),
                  const((D, D))],
        out_specs=row(D),
        compiler_params=_params("parallel", "parallel"),
        name="merge",
    )(x, gm, f, cv, og, od, z, wf, wc, wa, wd, wo)


def _conv_kernel(a_ref, g_ref, ap_ref, gp_ref, an_ref, gn_ref, w_ref, b_ref, lg_ref, lb_ref, o_ref, u_s,
                 *, ts, chunk):
    i = pl.program_id(1)
    n = pl.num_programs(1)

    def glu(a, g):
        return a.astype(F32) * jax.nn.sigmoid(g.astype(F32))

    u_s[CONV_HALO:CONV_HALO + ts, :] = glu(a_ref[...], g_ref[...])
    u_s[0:CONV_HALO, :] = jnp.where(i > 0, glu(ap_ref[...], gp_ref[...]), 0.0)
    u_s[CONV_HALO + ts:2 * CONV_HALO + ts, :] = jnp.where(i < n - 1, glu(an_ref[...], gn_ref[...]), 0.0)
    w = w_ref[...]
    pad = CONV_K // 2

    def body(c, carry):
        r0 = pl.multiple_of(c * chunk, chunk)
        win = u_s[pl.ds(r0, chunk + 2 * CONV_HALO), :]
        acc = jnp.zeros((chunk, BRANCH), F32) + b_ref[...]
        for j in range(CONV_K):
            off = CONV_HALO - pad + j
            acc = acc + w[j:j + 1, :] * win[off:off + chunk, :]
        mu = jnp.mean(acc, axis=-1, keepdims=True)
        d = acc - mu
        var = jnp.mean(d * d, axis=-1, keepdims=True)
        y = d * lax.rsqrt(var + LN_EPS) * lg_ref[...] + lb_ref[...]
        o_ref[pl.ds(r0, chunk), :] = _silu(y).astype(o_ref.dtype)
        return carry

    lax.fori_loop(0, ts // chunk, body, 0)


def _conv_branch(z, w, b, ln_g, ln_b):
    B, S, _ = z.shape
    ts = min(512, S)
    chunk = 32
    hb = ts // CONV_HALO
    last = S // CONV_HALO - 1
    cur = lambda blk: pl.BlockSpec((None, ts, BRANCH), lambda b_, i: (b_, i, blk))
    prev = lambda blk: pl.BlockSpec((None, CONV_HALO, BRANCH),
                                    lambda b_, i: (b_, jnp.maximum(i * hb - 1, 0), blk))
    nxt = lambda blk: pl.BlockSpec((None, CONV_HALO, BRANCH),
                                   lambda b_, i: (b_, jnp.minimum((i + 1) * hb, last), blk))
    const = lambda shape: pl.BlockSpec(shape, lambda b_, i: (0, 0))
    return pl.pallas_call(
        functools.partial(_conv_kernel, ts=ts, chunk=chunk),
        out_shape=jax.ShapeDtypeStruct((B, S, BRANCH), BF16),
        grid=(B, S // ts),
        in_specs=[cur(BLK_CA), cur(BLK_CG), prev(BLK_CA), prev(BLK_CG), nxt(BLK_CA), nxt(BLK_CG),
                  const((CONV_K, BRANCH)), const((1, BRANCH)), const((1, BRANCH)), const((1, BRANCH))],
        out_specs=pl.BlockSpec((None, ts, BRANCH), lambda b_, i: (b_, i, 0)),
        scratch_shapes=[pltpu.VMEM((ts + 2 * CONV_HALO, BRANCH), F32)],
        compiler_params=_params("parallel", "parallel"),
        name="conv_branch",
    )(z, z, z, z, z, z, w, b, ln_g, ln_b)


def _head_rms(x, ones_bd):
    x2 = x * x
    hi = x2.astype(BF16)
    lo = (x2 - hi.astype(F32)).astype(BF16)
    ssum = (jnp.dot(hi, ones_bd, preferred_element_type=F32)
            + jnp.dot(lo, ones_bd, preferred_element_type=F32))
    return x * lax.rsqrt(ssum * (1.0 / HEAD_DIM) + EPS)


def _rope(x, cosf, sins):
    width = x.shape[1]
    reps = width // LANE
    c = jnp.concatenate([cosf] * reps, axis=1) if reps > 1 else cosf
    s = jnp.concatenate([sins] * reps, axis=1) if reps > 1 else sins
    lane = lax.broadcasted_iota(jnp.int32, x.shape, 1)
    first_half = (lane & (HEAD_DIM - 1)) < HEAD_DIM // 2
    partner = jnp.where(first_half, pltpu.roll(x, width - HEAD_DIM // 2, 1), pltpu.roll(x, HEAD_DIM // 2, 1))
    return x * c + partner * s


def _prep_kernel(gq_ref, dq_ref, dk_ref, dv_ref, kv_ref, cos_ref, sin_ref, qn_ref, kn_ref, bd_ref,
                 qg_ref, kg_ref, vg_ref, qd_ref, kd_ref, vd_ref):
    cosf, sins = cos_ref[...], sin_ref[...]
    bd = bd_ref[...]
    scale = HEAD_DIM ** -0.5
    lane = lax.broadcasted_iota(jnp.int32, (gq_ref.shape[0], LANE), 1)
    low = lane < HEAD_DIM
    zero = jnp.zeros((gq_ref.shape[0], LANE), F32)

    q = _rope(_head_rms(gq_ref[...].astype(F32), bd) * qn_ref[...], cosf, sins) * scale
    for p in range(BRANCH // LANE):
        src = q[:, p * LANE:(p + 1) * LANE]
        swapped = pltpu.roll(src, HEAD_DIM, 1)
        for e in range(2):
            h = 2 * p + e
            kvh = h // GQA_GROUP
            val = src if e == kvh else swapped
            val = jnp.where(low, val, zero) if kvh == 0 else jnp.where(low, zero, val)
            qg_ref[h * LANE:(h + 1) * LANE, :] = val.T.astype(qg_ref.dtype)

    kvz = kv_ref[...].astype(F32)
    k = _rope(_head_rms(kvz[:, :LANE], bd[:LANE, :LANE]) * kn_ref[:, :LANE], cosf, sins)
    kg_ref[...] = k.astype(kg_ref.dtype)
    v = kvz[:, LANE:2 * LANE]
    vg_ref[:LANE, :] = jnp.where(low, v, zero).T.astype(vg_ref.dtype)
    vg_ref[LANE:, :] = jnp.where(low, pltpu.roll(v, HEAD_DIM, 1), zero).T.astype(vg_ref.dtype)

    qd = _rope(dq_ref[...].astype(F32), cosf, sins) * scale
    for h in range(DIFF_HEADS):
        src = qd[:, h * LANE:(h + 1) * LANE]
        qd_ref[(2 * h) * LANE:(2 * h + 1) * LANE, :] = jnp.where(low, src, zero).T.astype(qd_ref.dtype)
        qd_ref[(2 * h + 1) * LANE:(2 * h + 2) * LANE, :] = jnp.where(low, zero, src).T.astype(qd_ref.dtype)
        vd_ref[h * LANE:(h + 1) * LANE, :] = dv_ref[:, h * LANE:(h + 1) * LANE].astype(F32).T.astype(vd_ref.dtype)
    kd_ref[...] = _rope(dk_ref[...].astype(F32), cosf, sins).astype(kd_ref.dtype)


def _prep(z, cosf, sins, qn, kn, ones_bd):
    B, S, _ = z.shape
    ts = min(256, S)
    slab = lambda blk: pl.BlockSpec((None, ts, BRANCH), lambda b, i: (b, i, blk))
    tab = pl.BlockSpec((ts, LANE), lambda b, i: (i, 0))
    const = lambda shape: pl.BlockSpec(shape, lambda b, i: (0, 0))
    rows = lambda width: pl.BlockSpec((None, ts, width), lambda b, i: (b, i, 0))
    cols = lambda chans: pl.BlockSpec((None, chans, ts), lambda b, i: (b, 0, i))
    shapes = [(B, 2 * BRANCH, S), (B, S, LANE), (B, 2 * LANE, S), (B, 2 * BRANCH, S), (B, S, BRANCH), (B, BRANCH, S)]
    specs = [cols(2 * BRANCH), rows(LANE), cols(2 * LANE), cols(2 * BRANCH), rows(BRANCH), cols(BRANCH)]
    return pl.pallas_call(
        _prep_kernel,
        out_shape=[jax.ShapeDtypeStruct(s, BF16) for s in shapes],
        grid=(B, S // ts),
        in_specs=[slab(BLK_GQ), slab(BLK_DQ), slab(BLK_DK), slab(BLK_DV), slab(BLK_KV), tab, tab,
                  const((1, BRANCH)), const((1, BRANCH)), const((BRANCH, BRANCH))],
        out_specs=specs,
        compiler_params=_params("parallel", "parallel"),
        name="qk_prep",
    )(z, z, z, z, z, cosf, sins, qn, kn, ones_bd)


def _attn_kernel(*refs, sets, tq, mode, lam_init):
    if mode == "diff":
        q_ref, k_ref, v_ref, l1q, l1k, l2q, l2k, dn_ref, o_ref, q_s, m_s, l_s, acc_s = refs
    else:
        q_ref, k_ref, v_ref, o_ref, q_s, m_s, l_s, acc_s = refs
    j = pl.program_id(3)

    @pl.when(j == 0)
    def _init():
        for r in range(sets):
            q_s[:, r * tq:(r + 1) * tq] = q_ref[r * LANE:(r + 1) * LANE, :]
        m_s[...] = jnp.full(m_s.shape, -jnp.inf, F32)
        l_s[...] = jnp.zeros(l_s.shape, F32)
        acc_s[...] = jnp.zeros(acc_s.shape, F32)

    st = jnp.dot(k_ref[...], q_s[...], preferred_element_type=F32)
    m_prev = m_s[...]
    m_new = jnp.maximum(m_prev, jnp.max(st, axis=0, keepdims=True))
    alpha = jnp.exp(m_prev - m_new)
    pt = jnp.exp(st - m_new)
    l_s[...] = alpha * l_s[...] + jnp.sum(pt, axis=0, keepdims=True)
    acc_s[...] = alpha * acc_s[...] + jnp.dot(v_ref[...], pt.astype(BF16), preferred_element_type=F32)
    m_s[...] = m_new

    @pl.when(j == pl.num_programs(3) - 1)
    def _finish():
        ot = acc_s[...] / l_s[...]
        o = [ot[:, r * tq:(r + 1) * tq].T for r in range(sets)]
        if mode == "diff":
            lam = (jnp.exp(jnp.sum(l1q[...] * l1k[...], axis=1, keepdims=True))
                   - jnp.exp(jnp.sum(l2q[...] * l2k[...], axis=1, keepdims=True)) + lam_init)
            d = o[0] - lam * o[1]
            d = d * lax.rsqrt(jnp.mean(d * d, axis=1, keepdims=True) + EPS) * dn_ref[...]
            o_ref[...] = (d * (1.0 - lam_init)).astype(o_ref.dtype)
        else:
            lane = lax.broadcasted_iota(jnp.int32, (tq, LANE), 1)
            for pair in range(sets // 2):
                packed = jnp.where(lane < HEAD_DIM, o[2 * pair], pltpu.roll(o[2 * pair + 1], HEAD_DIM, 1))
                o_ref[:, pair * LANE:(pair + 1) * LANE] = packed.astype(o_ref.dtype)


def _attention(qt, k, vt, mode, lam_vecs=None, diff_norm=None, lam_init=0.0):
    B, _, S = qt.shape
    NK = k.shape[1]
    tq = min(256, S)
    tk = 1408 if NK % 1408 == 0 else 256
    if mode == "diff":
        groups, sets, out_w = DIFF_HEADS, 2, LANE
        kmap = lambda b, g, i, j: (b, j, g)
    else:
        groups, sets, out_w = GQA_KV_HEADS, GQA_GROUP, GQA_GROUP * HEAD_DIM
        kmap = lambda b, g, i, j: (b, j, 0)
    in_specs = [pl.BlockSpec((None, sets * LANE, tq), lambda b, g, i, j: (b, g, i)),
                pl.BlockSpec((None, tk, LANE), kmap),
                pl.BlockSpec((None, LANE, tk), lambda b, g, i, j: (b, g, j))]
    args = [qt, k, vt]
    if mode == "diff":
        in_specs += [pl.BlockSpec((1, HEAD_DIM), lambda b, g, i, j: (0, 0))] * 4
        in_specs += [pl.BlockSpec((1, LANE), lambda b, g, i, j: (0, 0))]
        args += list(lam_vecs) + [diff_norm]
    return pl.pallas_call(
        functools.partial(_attn_kernel, sets=sets, tq=tq, mode=mode, lam_init=lam_init),
        out_shape=jax.ShapeDtypeStruct((B, S, BRANCH), BF16),
        grid=(B, groups, S // tq, NK // tk),
        in_specs=in_specs,
        out_specs=pl.BlockSpec((None, tq, out_w), lambda b, g, i, j: (b, i, g)),
        scratch_shapes=[pltpu.VMEM((LANE, sets * tq), BF16),
                        pltpu.VMEM((1, sets * tq), F32),
                        pltpu.VMEM((1, sets * tq), F32),
                        pltpu.VMEM((LANE, sets * tq), F32)],
        compiler_params=_params("parallel", "parallel", "parallel", "arbitrary"),
        name="attn_" + mode,
    )(*args)


def _dft_cos_sin(n):
    idx = np.arange(n)
    ang = 2.0 * np.pi * ((idx[:, None] * idx[None, :]) % n) / n
    return np.cos(ang), np.sin(ang)


def _channel_dft_matrix():
    c, s = _dft_cos_sin(GRID_W)
    groups = BRANCH // GRID_W
    eye = np.eye(groups)
    return np.concatenate([np.kron(eye, c), -np.kron(eye, s)], axis=1) / math.sqrt(GRID_W)


def _fourier_a_kernel(z_ref, w0_ref, d1_ref, tc_ref, ts_ref, o_ref):
    n1 = z_ref.shape[0]
    u = jnp.dot(z_ref[...], w0_ref[...], preferred_element_type=F32).astype(BF16)
    p = jnp.dot(d1_ref[...], u, preferred_element_type=F32)
    vr = p[:n1, :BRANCH] + p[n1:, BRANCH:]
    vi = p[:n1, BRANCH:] - p[n1:, :BRANCH]
    reps = BRANCH // LANE
    c = jnp.concatenate([tc_ref[...]] * reps, axis=1)
    s = jnp.concatenate([ts_ref[...]] * reps, axis=1)
    o_ref[0] = (vr * c + vi * s).astype(o_ref.dtype)
    o_ref[1] = (vi * c - vr * s).astype(o_ref.dtype)


def _bmm_kernel(a_ref, x_ref, o_ref):
    o_ref[...] = jnp.dot(a_ref[...], x_ref[...], preferred_element_type=F32).astype(o_ref.dtype)


def _bmm(a, x, tn):
    M, K = a.shape
    B, _, N = x.shape
    tn = min(tn, N)
    return pl.pallas_call(
        _bmm_kernel,
        out_shape=jax.ShapeDtypeStruct((B, M, N), BF16),
        grid=(B, N // tn),
        in_specs=[pl.BlockSpec((M, K), lambda b, j: (0, 0)),
                  pl.BlockSpec((None, K, tn), lambda b, j: (b, 0, j))],
        out_specs=pl.BlockSpec((None, M, tn), lambda b, j: (b, 0, j)),
        compiler_params=_params("parallel", "parallel"),
        name="bmm",
    )(a, x)


def _slab_matmul_kernel(z_ref, w_ref, o_ref):
    o_ref[...] = jnp.dot(z_ref[...], w_ref[...], preferred_element_type=F32).astype(o_ref.dtype)


def _slab_matmul(z, blk, w):
    B, S, _ = z.shape
    K, N = w.shape
    ts = min(256, S)
    return pl.pallas_call(
        _slab_matmul_kernel,
        out_shape=jax.ShapeDtypeStruct((B, S, N), BF16),
        grid=(B, S // ts),
        in_specs=[pl.BlockSpec((None, ts, K), lambda b, i: (b, i, blk)),
                  pl.BlockSpec((K, N), lambda b, i: (0, 0))],
        out_specs=pl.BlockSpec((None, ts, N), lambda b, i: (b, i, 0)),
        compiler_params=_params("parallel", "parallel"),
        name="slab_matmul",
    )(z, w)


def _fourier_long(z, w0):
    B, S, _ = z.shape
    n2 = GRID_W
    n1 = S // n2
    c1, s1 = _dft_cos_sin(n1)
    d1 = jnp.asarray(np.concatenate([c1, s1], axis=0) / math.sqrt(n1), F32).astype(BF16)
    ang = 2.0 * np.pi * (np.arange(n2)[:, None] * np.arange(n1)[None, :]) / S
    twc = jnp.asarray(np.repeat(np.cos(ang)[:, :, None], LANE, axis=2), F32)
    tws = jnp.asarray(np.repeat(np.sin(ang)[:, :, None], LANE, axis=2), F32)
    zf = z[:, :, BLK_FOUR * BRANCH:(BLK_FOUR + 1) * BRANCH].reshape(B, n1, n2 * BRANCH)
    v = pl.pallas_call(
        _fourier_a_kernel,
        out_shape=jax.ShapeDtypeStruct((B, 2, n2, n1, BRANCH), BF16),
        grid=(B, n2),
        in_specs=[pl.BlockSpec((None, n1, BRANCH), lambda b, j: (b, 0, j)),
                  pl.BlockSpec((BRANCH, 2 * BRANCH), lambda b, j: (0, 0)),
                  pl.BlockSpec((2 * n1, n1), lambda b, j: (0, 0)),
                  pl.BlockSpec((None, n1, LANE), lambda b, j: (j, 0, 0)),
                  pl.BlockSpec((None, n1, LANE), lambda b, j: (j, 0, 0))],
        out_specs=pl.BlockSpec((None, 2, None, n1, BRANCH), lambda b, j: (b, 0, j, 0, 0)),
        compiler_params=_params("parallel", "parallel"),
        name="fourier_a",
    )(zf, w0, d1, twc, tws)
    vst = v.reshape(B, 2 * n2, n1 * BRANCH)
    c3, s3 = _dft_cos_sin(n2)
    d3 = jnp.asarray(np.concatenate([c3, s3], axis=1) / math.sqrt(n2), F32).astype(BF16)
    f = _bmm(d3, vst, 8192)
    return f.reshape(B, S, BRANCH)


def _fourier_short(z, w0):
    B, S, _ = z.shape
    u = _slab_matmul(z, BLK_FOUR, w0)
    ust = jnp.concatenate([u[:, :, :BRANCH], u[:, :, BRANCH:]], axis=1)
    c, s = _dft_cos_sin(S)
    dl = jnp.asarray(np.concatenate([c, s], axis=1) / math.sqrt(S), F32).astype(BF16)
    return _bmm(dl, ust, BRANCH)


def _rope_tables(n_rows):
    half = HEAD_DIM // 4
    inv = ROPE_BASE ** (-np.arange(0, HEAD_DIM // 2, 2, dtype=np.float64) / (HEAD_DIM // 2))
    row = np.repeat(np.arange(n_rows, dtype=np.float64), GRID_W)
    col = np.tile(np.arange(GRID_W, dtype=np.float64), n_rows)
    ang = np.concatenate([row[:, None] * inv, col[:, None] * inv], axis=-1)
    assert ang.shape[1] == 2 * half
    cos, sin = np.cos(ang), np.sin(ang)
    cosf = np.tile(np.concatenate([cos, cos], axis=1), (1, LANE // HEAD_DIM))
    sins = np.tile(np.concatenate([-sin, sin], axis=1), (1, LANE // HEAD_DIM))
    return jnp.asarray(cosf, F32), jnp.asarray(sins, F32)


def _permute_w_in(w):
    D = w.shape[0]
    four, conv, gq, gk, gv, dq, dk, dv, gates = jnp.split(
        w, np.cumsum([BRANCH, 2 * BRANCH, BRANCH, LANE, LANE, BRANCH, BRANCH, BRANCH])[:], axis=1)
    pad = jnp.zeros((D, BRANCH - 2 * LANE), w.dtype)
    out = jnp.concatenate([four, conv, gq, dq, dk, dv, gk, gv, pad, gates], axis=1)
    assert out.shape[1] == ZC
    return out.astype(BF16)


def _mixer_branches(z, lw, rope_tabs, ones_bd, w0, long_seq):
    f = _fourier_long(z, w0) if long_seq else _fourier_short(z, w0)
    cv = _conv_branch(z, lw["conv_w"], lw["conv_b"], lw["conv_ln_g"], lw["conv_ln_b"])
    qg, kg, vg, qd, kd, vd = _prep(z, rope_tabs[0], rope_tabs[1], lw["q_norm"], lw["k_norm"], ones_bd)
    return f, cv, qg, kg, vg, qd, kd, vd


def kernel(x, c, ctx, c_ctx, w_ada, b_ada, norm_mix, w_in, w_four, conv_w, conv_b, conv_ln_g, conv_ln_b, w_conv, q_norm, k_norm, w_gqa, lam_q1, lam_k1, lam_q2, lam_k2, diff_norm, w_diff, w_out, norm_ffn, w_ffn1, w_ffn3, w_ffn2, final_norm):
    B, S, D = x.shape
    Sc = ctx.shape[1]
    depth = w_ada.shape[0]
    assert B + 1 <= 8 and S % GRID_W == 0 and D == 2 * BRANCH

    cond = jnp.zeros((8, D), F32).at[:B].set(c).at[B].set(c_ctx)
    mods = _ada_mod(cond, w_ada, b_ada)

    rope_x = _rope_tables(S // GRID_W)
    rope_c = (jnp.ones((Sc, LANE), F32), jnp.zeros((Sc, LANE), F32))
    ones_bd = jnp.asarray(np.kron(np.eye(BRANCH // HEAD_DIM), np.ones((HEAD_DIM, HEAD_DIM))), F32).astype(BF16)
    w0 = jnp.asarray(_channel_dft_matrix(), F32).astype(BF16)
    tile2 = lambda v: jnp.tile(v, LANE // HEAD_DIM * (BRANCH // LANE)).reshape(1, BRANCH)

    for l in range(depth):
        last = l == depth - 1
        lam_init = 0.8 - 0.6 * math.exp(-0.3 * l)
        mx = [m.reshape(B, 1, D) for m in jnp.split(mods[l, :B], 6, axis=-1)]
        mc = [jnp.broadcast_to(m.reshape(1, 1, D), (B, 1, D)) for m in jnp.split(mods[l, B], 6, axis=-1)]
        lw = dict(conv_w=conv_w[l], conv_b=conv_b[l].reshape(1, BRANCH),
                  conv_ln_g=conv_ln_g[l].reshape(1, BRANCH), conv_ln_b=conv_ln_b[l].reshape(1, BRANCH),
                  q_norm=tile2(q_norm[l]), k_norm=tile2(k_norm[l]))
        w_in_l = _permute_w_in(w_in[l])
        wf, wc, wa, wd, wo = (w.astype(BF16) for w in (w_four[l], w_conv[l], w_gqa[l], w_diff[l], w_out[l]))
        w1, w3, w2 = (w.astype(BF16) for w in (w_ffn1[l], w_ffn3[l], w_ffn2[l]))
        gain_m, gain_f = norm_mix[l].reshape(1, D), norm_ffn[l].reshape(1, D)
        lam_vecs = [v[l].reshape(1, HEAD_DIM) for v in (lam_q1, lam_k1, lam_q2, lam_k2)]
        dn = diff_norm[l].reshape(1, LANE)
        fin = final_norm.reshape(1, D)

        zx = _inproj(x, gain_m, mx[0], mx[1], w_in_l)
        zc = _inproj(ctx, gain_m, mc[0], mc[1], w_in_l)
        fx, cvx, qgx, kgx, vgx, qdx, kdx, vdx = _mixer_branches(zx, lw, rope_x, ones_bd, w0, True)
        fc, cvc, qgc, kgc, vgc, qdc, kdc, vdc = _mixer_branches(zc, lw, rope_c, ones_bd, w0, False)

        cat = lambda a, b, axis: jnp.concatenate([a, b], axis=axis)
        ogx = _attention(qgx, cat(kgx, kgc, 1), cat(vgx, vgc, 2), "gqa")
        odx = _attention(qdx, cat(kdx, kdc, 1), cat(vdx, vdc, 2), "diff", lam_vecs, dn, lam_init)
        x = _merge(x, mx[2], fx, cvx, ogx, odx, zx, wf, wc, wa, wd, wo)
        if not last:
            ogc = _attention(qgc, kgc, vgc, "gqa")
            odc = _attention(qdc, kdc, vdc, "diff", lam_vecs, dn, lam_init)
            ctx = _merge(ctx, mc[2], fc, cvc, ogc, odc, zc, wf, wc, wa, wd, wo)
            ctx = _ffn(ctx, gain_f, mc[3], mc[4], mc[5], w1, w3, w2, fin, False)
        x = _ffn(x, gain_f, mx[3], mx[4], mx[5], w1, w3, w2, fin, last)
    return x
```

```python
import functools
import math

import numpy as np
import jax
import jax.numpy as jnp
from jax import lax
from jax.experimental import pallas as pl
from jax.experimental.pallas import tpu as pltpu

F32 = jnp.float32
BF16 = jnp.bfloat16

HEAD_DIM = 64
BRANCH = 512
GQA_GROUP = 4
GQA_KV_HEADS = 2
DIFF_HEADS = 4
N_BRANCH = 4
GRID_W = 64
CONV_K = 31
ROPE_BASE = 10000.0
EPS = 1e-6
LN_EPS = 1e-5

LANE = 128
BF16_SUBLANES = 16
VMEM_LIMIT = 56 * 1024 * 1024

ZC = 8192
BLK_FOUR, BLK_CA, BLK_CG, BLK_GQ, BLK_DQ, BLK_DK, BLK_DV, BLK_KV = range(8)
GATE_COL0 = 4096
CONV_HALO = 16


def _params(*sem):
    return pltpu.CompilerParams(dimension_semantics=sem, vmem_limit_bytes=VMEM_LIMIT)


def _silu(x):
    return x * jax.nn.sigmoid(x)


def _rms_mod(x, g, shift, scale):
    y = x * lax.rsqrt(jnp.mean(x * x, axis=-1, keepdims=True) + EPS) * g
    return y * (1.0 + scale) + shift


def _ada_kernel(c_ref, w_ref, b_ref, o_ref):
    a = _silu(c_ref[...])
    o_ref[...] = jnp.dot(a, w_ref[...], precision=lax.Precision.HIGHEST,
                         preferred_element_type=F32) + b_ref[...]


def _ada_mod(cond, w_ada, b_ada):
    L, D, N = w_ada.shape
    tn = 1536
    return pl.pallas_call(
        _ada_kernel,
        out_shape=jax.ShapeDtypeStruct((L, 8, N), F32),
        grid=(L, N // tn),
        in_specs=[pl.BlockSpec((8, D), lambda l, j: (0, 0)),
                  pl.BlockSpec((None, D, tn), lambda l, j: (l, 0, j)),
                  pl.BlockSpec((None, 1, tn), lambda l, j: (l, 0, j))],
        out_specs=pl.BlockSpec((None, 8, tn), lambda l, j: (l, 0, j)),
        compiler_params=_params("parallel", "parallel"),
        name="ada_mod",
    )(cond, w_ada, b_ada.reshape(L, 1, N))


def _inproj_kernel(x_ref, g_ref, sh_ref, sc_ref, w_ref, o_ref):
    h = _rms_mod(x_ref[...], g_ref[...], sh_ref[...], sc_ref[...])
    o_ref[...] = jnp.dot(h.astype(BF16), w_ref[...], preferred_element_type=F32).astype(o_ref.dtype)


def _inproj(x, gain, shift, scale, w):
    B, S, D = x.shape
    N = w.shape[1]
    tm = min(256, S)
    tn = 4096
    return pl.pallas_call(
        _inproj_kernel,
        out_shape=jax.ShapeDtypeStruct((B, S, N), BF16),
        grid=(N // tn, B, S // tm),
        in_specs=[pl.BlockSpec((None, tm, D), lambda j, b, i: (b, i, 0)),
                  pl.BlockSpec((1, D), lambda j, b, i: (0, 0)),
                  pl.BlockSpec((None, 1, D), lambda j, b, i: (b, 0, 0)),
                  pl.BlockSpec((None, 1, D), lambda j, b, i: (b, 0, 0)),
                  pl.BlockSpec((D, tn), lambda j, b, i: (0, j))],
        out_specs=pl.BlockSpec((None, tm, tn), lambda j, b, i: (b, i, j)),
        compiler_params=_params("parallel", "parallel", "parallel"),
        name="in_proj",
    )(x, gain, shift, scale, w)


def _ffn_kernel(x_ref, g_ref, sh_ref, sc_ref, gate_ref, w1_ref, w3_ref, w2_ref, fg_ref, o_ref, *, final):
    x = x_ref[...]
    h = _rms_mod(x, g_ref[...], sh_ref[...], sc_ref[...]).astype(BF16)
    a = jnp.dot(h, w1_ref[...], preferred_element_type=F32)
    b = jnp.dot(h, w3_ref[...], preferred_element_type=F32)
    u = (_silu(a) * b).astype(BF16)
    y = x + gate_ref[...] * jnp.dot(u, w2_ref[...], preferred_element_type=F32)
    if final:
        y = y * lax.rsqrt(jnp.mean(y * y, axis=-1, keepdims=True) + EPS) * fg_ref[...]
    o_ref[...] = y


def _ffn(x, gain, shift, scale, gate, w1, w3, w2, final_gain, final):
    B, S, D = x.shape
    H = w1.shape[1]
    tm = min(256, S)
    const = lambda shape: pl.BlockSpec(shape, lambda b, i: (0,) * len(shape), pipeline_mode=pl.Buffered(1))
    per_b = pl.BlockSpec((None, 1, D), lambda b, i: (b, 0, 0))
    return pl.pallas_call(
        functools.partial(_ffn_kernel, final=final),
        out_shape=jax.ShapeDtypeStruct((B, S, D), F32),
        grid=(B, S // tm),
        in_specs=[pl.BlockSpec((None, tm, D), lambda b, i: (b, i, 0)),
                  const((1, D)), per_b, per_b, per_b,
                  const((D, H)), const((D, H)), const((H, D)), const((1, D))],
        out_specs=pl.BlockSpec((None, tm, D), lambda b, i: (b, i, 0)),
        compiler_params=_params("parallel", "parallel"),
        name="ffn",
    )(x, gain, shift, scale, gate, w1, w3, w2, final_gain)


def _merge_kernel(x_ref, gm_ref, f_ref, c_ref, a_ref, d_ref, zg_ref,
                  wf_ref, wc_ref, wa_ref, wd_ref, wo_ref, o_ref):
    D = x_ref.shape[-1]
    acc = None
    for b, (br, w) in enumerate(((f_ref, wf_ref), (c_ref, wc_ref), (a_ref, wa_ref), (d_ref, wd_ref))):
        y = jnp.dot(br[...], w[...], preferred_element_type=F32)
        gate = jax.nn.sigmoid(zg_ref[:, b * D:(b + 1) * D].astype(F32))
        acc = gate * y if acc is None else acc + gate * y
    out = jnp.dot(acc.astype(BF16), wo_ref[...], preferred_element_type=F32)
    o_ref[...] = x_ref[...] + gm_ref[...] * out


def _merge(x, gm, f, cv, og, od, z, wf, wc, wa, wd, wo):
    B, S, D = x.shape
    tm = min(256, S)
    row = lambda width: pl.BlockSpec((None, tm, width), lambda b, i: (b, i, 0))
    const = lambda shape: pl.BlockSpec(shape, lambda b, i: (0, 0))
    return pl.pallas_call(
        _merge_kernel,
        out_shape=jax.ShapeDtypeStruct((B, S, D), F32),
        grid=(B, S // tm),
        in_specs=[row(D), pl.BlockSpec((None, 1, D), lambda b, i: (b, 0, 0)),
                  row(BRANCH), row(BRANCH), row(BRANCH), row(BRANCH),
                  pl.BlockSpec((None, tm, N_BRANCH * D), lambda b, i: (b, i, GATE_COL0 // (N_BRANCH * D))),
                  const((BRANCH, D)), const((BRANCH, D)), const((BRANCH, D)), const((BRANCH, D)),
                  const((D, D))],
        out_specs=row(D),
        compiler_params=_params("parallel", "parallel"),
        name="merge",
    )(x, gm, f, cv, og, od, z, wf, wc, wa, wd, wo)


def _conv_kernel(a_ref, g_ref, ap_ref, gp_ref, an_ref, gn_ref, w_ref, b_ref, lg_ref, lb_ref, o_ref, u_s,
                 *, ts, chunk):
    i = pl.program_id(1)
    n = pl.num_programs(1)

    def glu(a, g):
        return a.astype(F32) * jax.nn.sigmoid(g.astype(F32))

    u_s[CONV_HALO:CONV_HALO + ts, :] = glu(a_ref[...], g_ref[...])
    u_s[0:CONV_HALO, :] = jnp.where(i > 0, glu(ap_ref[...], gp_ref[...]), 0.0)
    u_s[CONV_HALO + ts:2 * CONV_HALO + ts, :] = jnp.where(i < n - 1, glu(an_ref[...], gn_ref[...]), 0.0)
    w = w_ref[...]
    pad = CONV_K // 2

    def body(c, carry):
        r0 = pl.multiple_of(c * chunk, chunk)
        win = u_s[pl.ds(r0, chunk + 2 * CONV_HALO), :]
        acc = jnp.zeros((chunk, BRANCH), F32) + b_ref[...]
        for j in range(CONV_K):
            off = CONV_HALO - pad + j
            acc = acc + w[j:j + 1, :] * win[off:off + chunk, :]
        mu = jnp.mean(acc, axis=-1, keepdims=True)
        d = acc - mu
        var = jnp.mean(d * d, axis=-1, keepdims=True)
        y = d * lax.rsqrt(var + LN_EPS) * lg_ref[...] + lb_ref[...]
        o_ref[pl.ds(r0, chunk), :] = _silu(y).astype(o_ref.dtype)
        return carry

    lax.fori_loop(0, ts // chunk, body, 0)


def _conv_branch(z, w, b, ln_g, ln_b):
    B, S, _ = z.shape
    ts = min(512, S)
    chunk = 32
    hb = ts // CONV_HALO
    last = S // CONV_HALO - 1
    cur = lambda blk: pl.BlockSpec((None, ts, BRANCH), lambda b_, i: (b_, i, blk))
    prev = lambda blk: pl.BlockSpec((None, CONV_HALO, BRANCH),
                                    lambda b_, i: (b_, jnp.maximum(i * hb - 1, 0), blk))
    nxt = lambda blk: pl.BlockSpec((None, CONV_HALO, BRANCH),
                                   lambda b_, i: (b_, jnp.minimum((i + 1) * hb, last), blk))
    const = lambda shape: pl.BlockSpec(shape, lambda b_, i: (0, 0))
    return pl.pallas_call(
        functools.partial(_conv_kernel, ts=ts, chunk=chunk),
        out_shape=jax.ShapeDtypeStruct((B, S, BRANCH), BF16),
        grid=(B, S // ts),
        in_specs=[cur(BLK_CA), cur(BLK_CG), prev(BLK_CA), prev(BLK_CG), nxt(BLK_CA), nxt(BLK_CG),
                  const((CONV_K, BRANCH)), const((1, BRANCH)), const((1, BRANCH)), const((1, BRANCH))],
        out_specs=pl.BlockSpec((None, ts, BRANCH), lambda b_, i: (b_, i, 0)),
        scratch_shapes=[pltpu.VMEM((ts + 2 * CONV_HALO, BRANCH), F32)],
        compiler_params=_params("parallel", "parallel"),
        name="conv_branch",
    )(z, z, z, z, z, z, w, b, ln_g, ln_b)


def _head_rms(x, ones_bd):
    x2 = x * x
    hi = x2.astype(BF16)
    lo = (x2 - hi.astype(F32)).astype(BF16)
    ssum = (jnp.dot(hi, ones_bd, preferred_element_type=F32)
            + jnp.dot(lo, ones_bd, preferred_element_type=F32))
    return x * lax.rsqrt(ssum * (1.0 / HEAD_DIM) + EPS)


def _rope(x, cosf, sins):
    width = x.shape[1]
    reps = width // LANE
    c = jnp.concatenate([cosf] * reps, axis=1) if reps > 1 else cosf
    s = jnp.concatenate([sins] * reps, axis=1) if reps > 1 else sins
    lane = lax.broadcasted_iota(jnp.int32, x.shape, 1)
    first_half = (lane & (HEAD_DIM - 1)) < HEAD_DIM // 2
    partner = jnp.where(first_half, pltpu.roll(x, width - HEAD_DIM // 2, 1), pltpu.roll(x, HEAD_DIM // 2, 1))
    return x * c + partner * s


def _prep_kernel(gq_ref, dq_ref, dk_ref, dv_ref, kv_ref, cos_ref, sin_ref, qn_ref, kn_ref, bd_ref,
                 qg_ref, kg_ref, vg_ref, qd_ref, kd_ref, vd_ref):
    cosf, sins = cos_ref[...], sin_ref[...]
    bd = bd_ref[...]
    scale = HEAD_DIM ** -0.5 * math.log2(math.e)
    lane = lax.broadcasted_iota(jnp.int32, (gq_ref.shape[0], LANE), 1)
    low = lane < HEAD_DIM
    zero = jnp.zeros((gq_ref.shape[0], LANE), F32)

    q = _rope(_head_rms(gq_ref[...].astype(F32), bd) * qn_ref[...], cosf, sins) * scale
    for p in range(BRANCH // LANE):
        src = q[:, p * LANE:(p + 1) * LANE]
        swapped = pltpu.roll(src, HEAD_DIM, 1)
        for e in range(2):
            h = 2 * p + e
            kvh = h // GQA_GROUP
            val = src if e == kvh else swapped
            val = jnp.where(low, val, zero) if kvh == 0 else jnp.where(low, zero, val)
            qg_ref[h * LANE:(h + 1) * LANE, :] = val.T.astype(qg_ref.dtype)

    kvz = kv_ref[...].astype(F32)
    k = _rope(_head_rms(kvz[:, :LANE], bd[:LANE, :LANE]) * kn_ref[:, :LANE], cosf, sins)
    kg_ref[...] = k.astype(kg_ref.dtype)
    v = kvz[:, LANE:2 * LANE]
    vg_ref[:LANE, :] = jnp.where(low, v, zero).T.astype(vg_ref.dtype)
    vg_ref[LANE:, :] = jnp.where(low, pltpu.roll(v, HEAD_DIM, 1), zero).T.astype(vg_ref.dtype)

    qd = _rope(dq_ref[...].astype(F32), cosf, sins) * scale
    for h in range(DIFF_HEADS):
        src = qd[:, h * LANE:(h + 1) * LANE]
        qd_ref[(2 * h) * LANE:(2 * h + 1) * LANE, :] = jnp.where(low, src, zero).T.astype(qd_ref.dtype)
        qd_ref[(2 * h + 1) * LANE:(2 * h + 2) * LANE, :] = jnp.where(low, zero, src).T.astype(qd_ref.dtype)
        vd_ref[h * LANE:(h + 1) * LANE, :] = dv_ref[:, h * LANE:(h + 1) * LANE].astype(F32).T.astype(vd_ref.dtype)
    kd_ref[...] = _rope(dk_ref[...].astype(F32), cosf, sins).astype(kd_ref.dtype)


def _prep(z, cosf, sins, qn, kn, ones_bd):
    B, S, _ = z.shape
    ts = min(256, S)
    slab = lambda blk: pl.BlockSpec((None, ts, BRANCH), lambda b, i: (b, i, blk))
    tab = pl.BlockSpec((ts, LANE), lambda b, i: (i, 0))
    const = lambda shape: pl.BlockSpec(shape, lambda b, i: (0, 0))
    rows = lambda width: pl.BlockSpec((None, ts, width), lambda b, i: (b, i, 0))
    cols = lambda chans: pl.BlockSpec((None, chans, ts), lambda b, i: (b, 0, i))
    shapes = [(B, 2 * BRANCH, S), (B, S, LANE), (B, 2 * LANE, S), (B, 2 * BRANCH, S), (B, S, BRANCH), (B, BRANCH, S)]
    specs = [cols(2 * BRANCH), rows(LANE), cols(2 * LANE), cols(2 * BRANCH), rows(BRANCH), cols(BRANCH)]
    return pl.pallas_call(
        _prep_kernel,
        out_shape=[jax.ShapeDtypeStruct(s, BF16) for s in shapes],
        grid=(B, S // ts),
        in_specs=[slab(BLK_GQ), slab(BLK_DQ), slab(BLK_DK), slab(BLK_DV), slab(BLK_KV), tab, tab,
                  const((1, BRANCH)), const((1, BRANCH)), const((BRANCH, BRANCH))],
        out_specs=specs,
        compiler_params=_params("parallel", "parallel"),
        name="qk_prep",
    )(z, z, z, z, z, cosf, sins, qn, kn, ones_bd)


def _attn_kernel(*refs, sets, tq, tk, mode, lam_init):
    if mode == "diff":
        q_ref, k_ref, v_ref, l1q, l1k, l2q, l2k, dn_ref, o_ref, q_s, s_buf, acc_s = refs
    else:
        q_ref, k_ref, v_ref, o_ref, q_s, s_buf, acc_s = refs
    n_tiles = k_ref.shape[0] // tk
    for r in range(sets):
        q_s[:, r * tq:(r + 1) * tq] = q_ref[r * LANE:(r + 1) * LANE, :]

    def scores(t):
        st = jnp.dot(k_ref[t * tk:(t + 1) * tk, :], q_s[...], preferred_element_type=F32)
        s_buf[t % 2] = st
        return jnp.max(st, axis=0, keepdims=True)

    m_run = jnp.full((1, sets * tq), -jnp.inf, F32)
    l_run = jnp.zeros((1, sets * tq), F32)
    m_tile = scores(0)
    for t in range(n_tiles):
        m_next = scores(t + 1) if t + 1 < n_tiles else None
        m_new = jnp.maximum(m_run, m_tile)
        alpha = jnp.exp2(m_run - m_new)
        pt = jnp.exp2(s_buf[t % 2] - m_new)
        l_run = alpha * l_run + jnp.sum(pt, axis=0, keepdims=True)
        pv = jnp.dot(v_ref[:, t * tk:(t + 1) * tk], pt.astype(BF16), preferred_element_type=F32)
        acc_s[...] = pv if t == 0 else alpha * acc_s[...] + pv
        m_run, m_tile = m_new, m_next

    ot = acc_s[...] / l_run
    o = [ot[:, r * tq:(r + 1) * tq].T for r in range(sets)]
    if mode == "diff":
        lam = (jnp.exp(jnp.sum(l1q[...] * l1k[...], axis=1, keepdims=True))
               - jnp.exp(jnp.sum(l2q[...] * l2k[...], axis=1, keepdims=True)) + lam_init)
        d = o[0] - lam * o[1]
        d = d * lax.rsqrt(jnp.mean(d * d, axis=1, keepdims=True) + EPS) * dn_ref[...]
        o_ref[...] = (d * (1.0 - lam_init)).astype(o_ref.dtype)
    else:
        lane = lax.broadcasted_iota(jnp.int32, (tq, LANE), 1)
        for pair in range(sets // 2):
            packed = jnp.where(lane < HEAD_DIM, o[2 * pair], pltpu.roll(o[2 * pair + 1], HEAD_DIM, 1))
            o_ref[:, pair * LANE:(pair + 1) * LANE] = packed.astype(o_ref.dtype)


def _attention(qt, k, vt, mode, lam_vecs=None, diff_norm=None, lam_init=0.0):
    B, _, S = qt.shape
    NK = k.shape[1]
    tq = min(256, S)
    tk = 1408 if NK % 1408 == 0 else 256
    if mode == "diff":
        groups, sets, out_w = DIFF_HEADS, 2, LANE
        kmap = lambda b, g, i: (b, 0, g)
    else:
        groups, sets, out_w = GQA_KV_HEADS, GQA_GROUP, GQA_GROUP * HEAD_DIM
        kmap = lambda b, g, i: (b, 0, 0)
    in_specs = [pl.BlockSpec((None, sets * LANE, tq), lambda b, g, i: (b, g, i)),
                pl.BlockSpec((None, NK, LANE), kmap),
                pl.BlockSpec((None, LANE, NK), lambda b, g, i: (b, g, 0))]
    args = [qt, k, vt]
    if mode == "diff":
        in_specs += [pl.BlockSpec((1, HEAD_DIM), lambda b, g, i: (0, 0))] * 4
        in_specs += [pl.BlockSpec((1, LANE), lambda b, g, i: (0, 0))]
        args += list(lam_vecs) + [diff_norm]
    return pl.pallas_call(
        functools.partial(_attn_kernel, sets=sets, tq=tq, tk=tk, mode=mode, lam_init=lam_init),
        out_shape=jax.ShapeDtypeStruct((B, S, BRANCH), BF16),
        grid=(B, groups, S // tq),
        in_specs=in_specs,
        out_specs=pl.BlockSpec((None, tq, out_w), lambda b, g, i: (b, i, g)),
        scratch_shapes=[pltpu.VMEM((LANE, sets * tq), BF16),
                        pltpu.VMEM((2, tk, sets * tq), F32),
                        pltpu.VMEM((LANE, sets * tq), F32)],
        compiler_params=_params("parallel", "parallel", "parallel"),
        name="attn_" + mode,
    )(*args)


def _dft_cos_sin(n):
    idx = np.arange(n)
    ang = 2.0 * np.pi * ((idx[:, None] * idx[None, :]) % n) / n
    return np.cos(ang), np.sin(ang)


def _channel_dft_matrix():
    c, s = _dft_cos_sin(GRID_W)
    groups = BRANCH // GRID_W
    eye = np.eye(groups)
    return np.concatenate([np.kron(eye, c), -np.kron(eye, s)], axis=1) / math.sqrt(GRID_W)


def _fourier_a_kernel(z_ref, w0_ref, d1_ref, tc_ref, ts_ref, o_ref):
    n1 = z_ref.shape[0]
    u = jnp.dot(z_ref[...], w0_ref[...], preferred_element_type=F32).astype(BF16)
    p = jnp.dot(d1_ref[...], u, preferred_element_type=F32)
    vr = p[:n1, :BRANCH] + p[n1:, BRANCH:]
    vi = p[:n1, BRANCH:] - p[n1:, :BRANCH]
    reps = BRANCH // LANE
    c = jnp.concatenate([tc_ref[...]] * reps, axis=1)
    s = jnp.concatenate([ts_ref[...]] * reps, axis=1)
    o_ref[0] = (vr * c + vi * s).astype(o_ref.dtype)
    o_ref[1] = (vi * c - vr * s).astype(o_ref.dtype)


def _bmm_kernel(a_ref, x_ref, o_ref):
    o_ref[...] = jnp.dot(a_ref[...], x_ref[...], preferred_element_type=F32).astype(o_ref.dtype)


def _bmm(a, x, tn):
    M, K = a.shape
    B, _, N = x.shape
    tn = min(tn, N)
    return pl.pallas_call(
        _bmm_kernel,
        out_shape=jax.ShapeDtypeStruct((B, M, N), BF16),
        grid=(B, N // tn),
        in_specs=[pl.BlockSpec((M, K), lambda b, j: (0, 0)),
                  pl.BlockSpec((None, K, tn), lambda b, j: (b, 0, j))],
        out_specs=pl.BlockSpec((None, M, tn), lambda b, j: (b, 0, j)),
        compiler_params=_params("parallel", "parallel"),
        name="bmm",
    )(a, x)


def _slab_matmul_kernel(z_ref, w_ref, o_ref):
    o_ref[...] = jnp.dot(z_ref[...], w_ref[...], preferred_element_type=F32).astype(o_ref.dtype)


def _slab_matmul(z, blk, w):
    B, S, _ = z.shape
    K, N = w.shape
    ts = min(256, S)
    return pl.pallas_call(
        _slab_matmul_kernel,
        out_shape=jax.ShapeDtypeStruct((B, S, N), BF16),
        grid=(B, S // ts),
        in_specs=[pl.BlockSpec((None, ts, K), lambda b, i: (b, i, blk)),
                  pl.BlockSpec((K, N), lambda b, i: (0, 0))],
        out_specs=pl.BlockSpec((None, ts, N), lambda b, i: (b, i, 0)),
        compiler_params=_params("parallel", "parallel"),
        name="slab_matmul",
    )(z, w)


def _fourier_long(z, w0):
    B, S, _ = z.shape
    n2 = GRID_W
    n1 = S // n2
    c1, s1 = _dft_cos_sin(n1)
    d1 = jnp.asarray(np.concatenate([c1, s1], axis=0) / math.sqrt(n1), F32).astype(BF16)
    ang = 2.0 * np.pi * (np.arange(n2)[:, None] * np.arange(n1)[None, :]) / S
    twc = jnp.asarray(np.repeat(np.cos(ang)[:, :, None], LANE, axis=2), F32)
    tws = jnp.asarray(np.repeat(np.sin(ang)[:, :, None], LANE, axis=2), F32)
    zf = z[:, :, BLK_FOUR * BRANCH:(BLK_FOUR + 1) * BRANCH].reshape(B, n1, n2 * BRANCH)
    v = pl.pallas_call(
        _fourier_a_kernel,
        out_shape=jax.ShapeDtypeStruct((B, 2, n2, n1, BRANCH), BF16),
        grid=(B, n2),
        in_specs=[pl.BlockSpec((None, n1, BRANCH), lambda b, j: (b, 0, j)),
                  pl.BlockSpec((BRANCH, 2 * BRANCH), lambda b, j: (0, 0)),
                  pl.BlockSpec((2 * n1, n1), lambda b, j: (0, 0)),
                  pl.BlockSpec((None, n1, LANE), lambda b, j: (j, 0, 0)),
                  pl.BlockSpec((None, n1, LANE), lambda b, j: (j, 0, 0))],
        out_specs=pl.BlockSpec((None, 2, None, n1, BRANCH), lambda b, j: (b, 0, j, 0, 0)),
        compiler_params=_params("parallel", "parallel"),
        name="fourier_a",
    )(zf, w0, d1, twc, tws)
    vst = v.reshape(B, 2 * n2, n1 * BRANCH)
    c3, s3 = _dft_cos_sin(n2)
    d3 = jnp.asarray(np.concatenate([c3, s3], axis=1) / math.sqrt(n2), F32).astype(BF16)
    f = _bmm(d3, vst, 8192)
    return f.reshape(B, S, BRANCH)


def _fourier_short(z, w0):
    B, S, _ = z.shape
    u = _slab_matmul(z, BLK_FOUR, w0)
    ust = jnp.concatenate([u[:, :, :BRANCH], u[:, :, BRANCH:]], axis=1)
    c, s = _dft_cos_sin(S)
    dl = jnp.asarray(np.concatenate([c, s], axis=1) / math.sqrt(S), F32).astype(BF16)
    return _bmm(dl, ust, BRANCH)


def _rope_tables(n_rows):
    half = HEAD_DIM // 4
    inv = ROPE_BASE ** (-np.arange(0, HEAD_DIM // 2, 2, dtype=np.float64) / (HEAD_DIM // 2))
    row = np.repeat(np.arange(n_rows, dtype=np.float64), GRID_W)
    col = np.tile(np.arange(GRID_W, dtype=np.float64), n_rows)
    ang = np.concatenate([row[:, None] * inv, col[:, None] * inv], axis=-1)
    assert ang.shape[1] == 2 * half
    cos, sin = np.cos(ang), np.sin(ang)
    cosf = np.tile(np.concatenate([cos, cos], axis=1), (1, LANE // HEAD_DIM))
    sins = np.tile(np.concatenate([-sin, sin], axis=1), (1, LANE // HEAD_DIM))
    return jnp.asarray(cosf, F32), jnp.asarray(sins, F32)


def _permute_w_in(w):
    D = w.shape[0]
    four, conv, gq, gk, gv, dq, dk, dv, gates = jnp.split(
        w, np.cumsum([BRANCH, 2 * BRANCH, BRANCH, LANE, LANE, BRANCH, BRANCH, BRANCH])[:], axis=1)
    pad = jnp.zeros((D, BRANCH - 2 * LANE), w.dtype)
    out = jnp.concatenate([four, conv, gq, dq, dk, dv, gk, gv, pad, gates], axis=1)
    assert out.shape[1] == ZC
    return out.astype(BF16)


def _mixer_branches(z, lw, rope_tabs, ones_bd, w0, long_seq):
    f = _fourier_long(z, w0) if long_seq else _fourier_short(z, w0)
    cv = _conv_branch(z, lw["conv_w"], lw["conv_b"], lw["conv_ln_g"], lw["conv_ln_b"])
    qg, kg, vg, qd, kd, vd = _prep(z, rope_tabs[0], rope_tabs[1], lw["q_norm"], lw["k_norm"], ones_bd)
    return f, cv, qg, kg, vg, qd, kd, vd


def kernel(x, c, ctx, c_ctx, w_ada, b_ada, norm_mix, w_in, w_four, conv_w, conv_b, conv_ln_g, conv_ln_b, w_conv, q_norm, k_norm, w_gqa, lam_q1, lam_k1, lam_q2, lam_k2, diff_norm, w_diff, w_out, norm_ffn, w_ffn1, w_ffn3, w_ffn2, final_norm):
    B, S, D = x.shape
    Sc = ctx.shape[1]
    depth = w_ada.shape[0]
    assert B + 1 <= 8 and S % GRID_W == 0 and D == 2 * BRANCH

    cond = jnp.zeros((8, D), F32).at[:B].set(c).at[B].set(c_ctx)
    mods = _ada_mod(cond, w_ada, b_ada)

    rope_x = _rope_tables(S // GRID_W)
    rope_c = (jnp.ones((Sc, LANE), F32), jnp.zeros((Sc, LANE), F32))
    ones_bd = jnp.asarray(np.kron(np.eye(BRANCH // HEAD_DIM), np.ones((HEAD_DIM, HEAD_DIM))), F32).astype(BF16)
    w0 = jnp.asarray(_channel_dft_matrix(), F32).astype(BF16)
    tile2 = lambda v: jnp.tile(v, LANE // HEAD_DIM * (BRANCH // LANE)).reshape(1, BRANCH)

    for l in range(depth):
        last = l == depth - 1
        lam_init = 0.8 - 0.6 * math.exp(-0.3 * l)
        mx = [m.reshape(B, 1, D) for m in jnp.split(mods[l, :B], 6, axis=-1)]
        mc = [jnp.broadcast_to(m.reshape(1, 1, D), (B, 1, D)) for m in jnp.split(mods[l, B], 6, axis=-1)]
        lw = dict(conv_w=conv_w[l], conv_b=conv_b[l].reshape(1, BRANCH),
                  conv_ln_g=conv_ln_g[l].reshape(1, BRANCH), conv_ln_b=conv_ln_b[l].reshape(1, BRANCH),
                  q_norm=tile2(q_norm[l]), k_norm=tile2(k_norm[l]))
        w_in_l = _permute_w_in(w_in[l])
        wf, wc, wa, wd, wo = (w.astype(BF16) for w in (w_four[l], w_conv[l], w_gqa[l], w_diff[l], w_out[l]))
        w1, w3, w2 = (w.astype(BF16) for w in (w_ffn1[l], w_ffn3[l], w_ffn2[l]))
        gain_m, gain_f = norm_mix[l].reshape(1, D), norm_ffn[l].reshape(1, D)
        lam_vecs = [v[l].reshape(1, HEAD_DIM) for v in (lam_q1, lam_k1, lam_q2, lam_k2)]
        dn = diff_norm[l].reshape(1, LANE)
        fin = final_norm.reshape(1, D)

        zx = _inproj(x, gain_m, mx[0], mx[1], w_in_l)
        zc = _inproj(ctx, gain_m, mc[0], mc[1], w_in_l)
        fx, cvx, qgx, kgx, vgx, qdx, kdx, vdx = _mixer_branches(zx, lw, rope_x, ones_bd, w0, True)
        fc, cvc, qgc, kgc, vgc, qdc, kdc, vdc = _mixer_branches(zc, lw, rope_c, ones_bd, w0, False)

        cat = lambda a, b, axis: jnp.concatenate([a, b], axis=axis)
        ogx = _attention(qgx, cat(kgx, kgc, 1), cat(vgx, vgc, 2), "gqa")
        odx = _attention(qdx, cat(kdx, kdc, 1), cat(vdx, vdc, 2), "diff", lam_vecs, dn, lam_init)
        x = _merge(x, mx[2], fx, cvx, ogx, odx, zx, wf, wc, wa, wd, wo)
        if not last:
            ogc = _attention(qgc, kgc, vgc, "gqa")
            odc = _attention(qdc, kdc, vdc, "diff", lam_vecs, dn, lam_init)
            ctx = _merge(ctx, mc[2], fc, cvc, ogc, odc, zc, wf, wc, wa, wd, wo)
            ctx = _ffn(ctx, gain_f, mc[3], mc[4], mc[5], w1, w3, w2, fin, False)
        x = _ffn(x, gain_f, mx[3], mx[4], mx[5], w1, w3, w2, fin, last)
    return x
```

```python
import functools
import math

import numpy as np
import jax
import jax.numpy as jnp
from jax import lax
from jax.experimental import pallas as pl
from jax.experimental.pallas import tpu as pltpu

F32 = jnp.float32
BF16 = jnp.bfloat16

HEAD_DIM = 64
BRANCH = 512
GQA_GROUP = 4
GQA_KV_HEADS = 2
DIFF_HEADS = 4
N_BRANCH = 4
GRID_W = 64
CONV_K = 31
ROPE_BASE = 10000.0
EPS = 1e-6
LN_EPS = 1e-5

LANE = 128
SUBLANE = 8
BF16_SUBLANES = 16
VMEM_LIMIT = 56 * 1024 * 1024

ZC = 8192
BLK_FOUR, BLK_CA, BLK_CG, BLK_GQ, BLK_DQ, BLK_DK, BLK_DV, BLK_KV = range(8)
GATE_COL0 = 4096
FOURIER_NB = 16
CONV_HALO = 16


def _params(*sem):
    return pltpu.CompilerParams(dimension_semantics=sem, vmem_limit_bytes=VMEM_LIMIT)


def _silu(x):
    return x * jax.nn.sigmoid(x)


def _rms_mod(x, g, shift, scale):
    y = x * lax.rsqrt(jnp.mean(x * x, axis=-1, keepdims=True) + EPS) * g
    return y * (1.0 + scale) + shift


def _ada_kernel(c_ref, w_ref, b_ref, o_ref):
    a = _silu(c_ref[...])
    o_ref[...] = jnp.dot(a, w_ref[...], precision=lax.Precision.HIGHEST,
                         preferred_element_type=F32) + b_ref[...]


def _ada_mod(cond, w_ada, b_ada):
    L, D, N = w_ada.shape
    tn = 1536
    return pl.pallas_call(
        _ada_kernel,
        out_shape=jax.ShapeDtypeStruct((L, 8, N), F32),
        grid=(L, N // tn),
        in_specs=[pl.BlockSpec((8, D), lambda l, j: (0, 0)),
                  pl.BlockSpec((None, D, tn), lambda l, j: (l, 0, j)),
                  pl.BlockSpec((None, 1, tn), lambda l, j: (l, 0, j))],
        out_specs=pl.BlockSpec((None, 8, tn), lambda l, j: (l, 0, j)),
        compiler_params=_params("parallel", "parallel"),
        name="ada_mod",
    )(cond, w_ada, b_ada.reshape(L, 1, N))


def _inproj_kernel(x_ref, g_ref, sh_ref, sc_ref, w_ref, o_ref):
    h = _rms_mod(x_ref[...], g_ref[...], sh_ref[...], sc_ref[...])
    o_ref[...] = jnp.dot(h.astype(BF16), w_ref[...], preferred_element_type=F32).astype(o_ref.dtype)


def _inproj(x, gain, shift, scale, w):
    B, S, D = x.shape
    N = w.shape[1]
    tm = min(256, S)
    tn = 4096
    return pl.pallas_call(
        _inproj_kernel,
        out_shape=jax.ShapeDtypeStruct((B, S, N), BF16),
        grid=(N // tn, B, S // tm),
        in_specs=[pl.BlockSpec((None, tm, D), lambda j, b, i: (b, i, 0)),
                  pl.BlockSpec((1, D), lambda j, b, i: (0, 0)),
                  pl.BlockSpec((None, 1, D), lambda j, b, i: (b, 0, 0)),
                  pl.BlockSpec((None, 1, D), lambda j, b, i: (b, 0, 0)),
                  pl.BlockSpec((D, tn), lambda j, b, i: (0, j))],
        out_specs=pl.BlockSpec((None, tm, tn), lambda j, b, i: (b, i, j)),
        compiler_params=_params("parallel", "parallel", "parallel"),
        name="in_proj",
    )(x, gain, shift, scale, w)


def _ffn_kernel(x_ref, g_ref, sh_ref, sc_ref, gate_ref, w1_ref, w3_ref, w2_ref, fg_ref, o_ref, *, final):
    x = x_ref[...]
    h = _rms_mod(x, g_ref[...], sh_ref[...], sc_ref[...]).astype(BF16)
    a = jnp.dot(h, w1_ref[...], preferred_element_type=F32)
    b = jnp.dot(h, w3_ref[...], preferred_element_type=F32)
    u = (_silu(a) * b).astype(BF16)
    y = x + gate_ref[...] * jnp.dot(u, w2_ref[...], preferred_element_type=F32)
    if final:
        y = y * lax.rsqrt(jnp.mean(y * y, axis=-1, keepdims=True) + EPS) * fg_ref[...]
    o_ref[...] = y


def _ffn(x, gain, shift, scale, gate, w1, w3, w2, final_gain, final):
    B, S, D = x.shape
    H = w1.shape[1]
    tm = min(256, S)
    const = lambda shape: pl.BlockSpec(shape, lambda b, i: (0,) * len(shape), pipeline_mode=pl.Buffered(1))
    per_b = pl.BlockSpec((None, 1, D), lambda b, i: (b, 0, 0))
    return pl.pallas_call(
        functools.partial(_ffn_kernel, final=final),
        out_shape=jax.ShapeDtypeStruct((B, S, D), F32),
        grid=(B, S // tm),
        in_specs=[pl.BlockSpec((None, tm, D), lambda b, i: (b, i, 0)),
                  const((1, D)), per_b, per_b, per_b,
                  const((D, H)), const((D, H)), const((H, D)), const((1, D))],
        out_specs=pl.BlockSpec((None, tm, D), lambda b, i: (b, i, 0)),
        compiler_params=_params("parallel", "parallel"),
        name="ffn",
    )(x, gain, shift, scale, gate, w1, w3, w2, final_gain)


def _merge_kernel(x_ref, gm_ref, f_ref, c_ref, a_ref, d_ref, zg_ref,
                  wf_ref, wc_ref, wa_ref, wd_ref, wo_ref, o_ref):
    D = x_ref.shape[-1]
    acc = None
    for b, (br, w) in enumerate(((f_ref, wf_ref), (c_ref, wc_ref), (a_ref, wa_ref), (d_ref, wd_ref))):
        y = jnp.dot(br[...], w[...], preferred_element_type=F32)
        gate = jax.nn.sigmoid(zg_ref[:, b * D:(b + 1) * D].astype(F32))
        acc = gate * y if acc is None else acc + gate * y
    out = jnp.dot(acc.astype(BF16), wo_ref[...], preferred_element_type=F32)
    o_ref[...] = x_ref[...] + gm_ref[...] * out


def _merge(x, gm, f, cv, og, od, z, wf, wc, wa, wd, wo):
    B, S, D = x.shape
    tm = min(256, S)
    row = lambda width: pl.BlockSpec((None, tm, width), lambda b, i: (b, i, 0))
    const = lambda shape: pl.BlockSpec(shape, lambda b, i: (0, 0))
    return pl.pallas_call(
        _merge_kernel,
        out_shape=jax.ShapeDtypeStruct((B, S, D), F32),
        grid=(B, S // tm),
        in_specs=[row(D), pl.BlockSpec((None, 1, D), lambda b, i: (b, 0, 0)),
                  row(BRANCH), row(BRANCH), row(BRANCH), row(BRANCH),
                  pl.BlockSpec((None, tm, N_BRANCH * D), lambda b, i: (b, i, GATE_COL0 // (N_BRANCH * D))),
                  const((BRANCH, D)), const((BRANCH, D)), const((BRANCH, D)), const((BRANCH, D)),
                  const((D, D))],
        out_specs=row(D),
        compiler_params=_params("parallel", "parallel"),
        name="merge",
    )(x, gm, f, cv, og, od, z, wf, wc, wa, wd, wo)


def _conv_kernel(a_ref, g_ref, ap_ref, gp_ref, an_ref, gn_ref, w_ref, b_ref, lg_ref, lb_ref, o_ref, u_s,
                 *, ts, chunk):
    i = pl.program_id(1)
    n = pl.num_programs(1)

    def glu(a, g):
        return a.astype(F32) * jax.nn.sigmoid(g.astype(F32))

    u_s[CONV_HALO:CONV_HALO + ts, :] = glu(a_ref[...], g_ref[...])
    u_s[0:CONV_HALO, :] = jnp.where(i > 0, glu(ap_ref[...], gp_ref[...]), 0.0)
    u_s[CONV_HALO + ts:2 * CONV_HALO + ts, :] = jnp.where(i < n - 1, glu(an_ref[...], gn_ref[...]), 0.0)
    pad = CONV_K // 2

    def body(c, carry):
        r0 = pl.multiple_of(c * chunk, chunk)
        win = u_s[pl.ds(r0, chunk + 2 * CONV_HALO), :]
        rows = chunk + 2 * CONV_HALO
        acc = jnp.zeros((chunk // SUBLANE, SUBLANE, BRANCH), F32) + b_ref[...]
        for r in range(SUBLANE):
            taps = [j for j in range(CONV_K) if (CONV_HALO - pad + j) % SUBLANE == r]
            shifted = win if r == 0 else pltpu.roll(win, rows - r, 0)
            for j in taps:
                base = CONV_HALO - pad + j - r
                tap = shifted[base:base + chunk, :].reshape(chunk // SUBLANE, SUBLANE, BRANCH)
                acc = acc + w_ref[j] * tap
        acc = acc.reshape(chunk, BRANCH)
        mu = jnp.mean(acc, axis=-1, keepdims=True)
        d = acc - mu
        var = jnp.mean(d * d, axis=-1, keepdims=True)
        y = d * lax.rsqrt(var + LN_EPS) * lg_ref[...] + lb_ref[...]
        o_ref[pl.ds(r0, chunk), :] = _silu(y).astype(o_ref.dtype)
        return carry

    lax.fori_loop(0, ts // chunk, body, 0)


def _conv_branch(z, w, b, ln_g, ln_b):
    B, S, _ = z.shape
    ts = min(512, S)
    chunk = min(256, ts)
    hb = ts // CONV_HALO
    last = S // CONV_HALO - 1
    cur = lambda blk: pl.BlockSpec((None, ts, BRANCH), lambda b_, i: (b_, i, blk))
    prev = lambda blk: pl.BlockSpec((None, CONV_HALO, BRANCH),
                                    lambda b_, i: (b_, jnp.maximum(i * hb - 1, 0), blk))
    nxt = lambda blk: pl.BlockSpec((None, CONV_HALO, BRANCH),
                                   lambda b_, i: (b_, jnp.minimum((i + 1) * hb, last), blk))
    const = lambda shape: pl.BlockSpec(shape, lambda b_, i: (0, 0))
    return pl.pallas_call(
        functools.partial(_conv_kernel, ts=ts, chunk=chunk),
        out_shape=jax.ShapeDtypeStruct((B, S, BRANCH), BF16),
        grid=(B, S // ts),
        in_specs=[cur(BLK_CA), cur(BLK_CG), prev(BLK_CA), prev(BLK_CG), nxt(BLK_CA), nxt(BLK_CG),
                  pl.BlockSpec((CONV_K, SUBLANE, BRANCH), lambda b_, i: (0, 0, 0)),
                  const((1, BRANCH)), const((1, BRANCH)), const((1, BRANCH))],
        out_specs=pl.BlockSpec((None, ts, BRANCH), lambda b_, i: (b_, i, 0)),
        scratch_shapes=[pltpu.VMEM((ts + 2 * CONV_HALO, BRANCH), F32)],
        compiler_params=_params("parallel", "parallel"),
        name="conv_branch",
    )(z, z, z, z, z, z, jnp.broadcast_to(w[:, None, :], (CONV_K, SUBLANE, BRANCH)), b, ln_g, ln_b)


def _head_rms(x, ones_bd):
    x2 = x * x
    hi = x2.astype(BF16)
    lo = (x2 - hi.astype(F32)).astype(BF16)
    ssum = (jnp.dot(hi, ones_bd, preferred_element_type=F32)
            + jnp.dot(lo, ones_bd, preferred_element_type=F32))
    return x * lax.rsqrt(ssum * (1.0 / HEAD_DIM) + EPS)


def _rope(x, cosf, sins):
    width = x.shape[1]
    reps = width // LANE
    c = jnp.concatenate([cosf] * reps, axis=1) if reps > 1 else cosf
    s = jnp.concatenate([sins] * reps, axis=1) if reps > 1 else sins
    lane = lax.broadcasted_iota(jnp.int32, x.shape, 1)
    first_half = (lane & (HEAD_DIM - 1)) < HEAD_DIM // 2
    partner = jnp.where(first_half, pltpu.roll(x, width - HEAD_DIM // 2, 1), pltpu.roll(x, HEAD_DIM // 2, 1))
    return x * c + partner * s


def _prep_kernel(gq_ref, dq_ref, dk_ref, dv_ref, kv_ref, cos_ref, sin_ref, qn_ref, kn_ref, bd_ref,
                 qg_ref, kg_ref, vg_ref, qd_ref, kd_ref, vd_ref):
    cosf, sins = cos_ref[...], sin_ref[...]
    bd = bd_ref[...]
    scale = HEAD_DIM ** -0.5 * math.log2(math.e)
    lane = lax.broadcasted_iota(jnp.int32, (gq_ref.shape[0], LANE), 1)
    low = lane < HEAD_DIM
    zero = jnp.zeros((gq_ref.shape[0], LANE), F32)

    q = _rope(_head_rms(gq_ref[...].astype(F32), bd) * qn_ref[...], cosf, sins) * scale
    for p in range(BRANCH // LANE):
        src = q[:, p * LANE:(p + 1) * LANE]
        swapped = pltpu.roll(src, HEAD_DIM, 1)
        for e in range(2):
            h = 2 * p + e
            kvh = h // GQA_GROUP
            val = src if e == kvh else swapped
            val = jnp.where(low, val, zero) if kvh == 0 else jnp.where(low, zero, val)
            qg_ref[h * LANE:(h + 1) * LANE, :] = val.T.astype(qg_ref.dtype)

    kvz = kv_ref[...].astype(F32)
    k = _rope(_head_rms(kvz[:, :LANE], bd[:LANE, :LANE]) * kn_ref[:, :LANE], cosf, sins)
    kg_ref[...] = k.astype(kg_ref.dtype)
    v = kvz[:, LANE:2 * LANE]
    pad = jnp.where(lane == HEAD_DIM, 1.0, 0.0)
    vg_ref[:LANE, :] = jnp.where(low, v, pad).T.astype(vg_ref.dtype)
    vg_ref[LANE:, :] = jnp.where(low, pltpu.roll(v, HEAD_DIM, 1), pad).T.astype(vg_ref.dtype)

    qd = _rope(dq_ref[...].astype(F32), cosf, sins) * scale
    for h in range(DIFF_HEADS):
        src = qd[:, h * LANE:(h + 1) * LANE]
        qd_ref[(2 * h) * LANE:(2 * h + 1) * LANE, :] = jnp.where(low, src, zero).T.astype(qd_ref.dtype)
        qd_ref[(2 * h + 1) * LANE:(2 * h + 2) * LANE, :] = jnp.where(low, zero, src).T.astype(qd_ref.dtype)
        vd_ref[h * LANE:(h + 1) * LANE, :] = dv_ref[:, h * LANE:(h + 1) * LANE].astype(F32).T.astype(vd_ref.dtype)
    kd_ref[...] = _rope(dk_ref[...].astype(F32), cosf, sins).astype(kd_ref.dtype)


def _prep(z, cosf, sins, qn, kn, ones_bd):
    B, S, _ = z.shape
    ts = min(256, S)
    slab = lambda blk: pl.BlockSpec((None, ts, BRANCH), lambda b, i: (b, i, blk))
    tab = pl.BlockSpec((ts, LANE), lambda b, i: (i, 0))
    const = lambda shape: pl.BlockSpec(shape, lambda b, i: (0, 0))
    rows = lambda width: pl.BlockSpec((None, ts, width), lambda b, i: (b, i, 0))
    cols = lambda chans: pl.BlockSpec((None, chans, ts), lambda b, i: (b, 0, i))
    shapes = [(B, 2 * BRANCH, S), (B, S, LANE), (B, 2 * LANE, S), (B, 2 * BRANCH, S), (B, S, BRANCH), (B, BRANCH, S)]
    specs = [cols(2 * BRANCH), rows(LANE), cols(2 * LANE), cols(2 * BRANCH), rows(BRANCH), cols(BRANCH)]
    return pl.pallas_call(
        _prep_kernel,
        out_shape=[jax.ShapeDtypeStruct(s, BF16) for s in shapes],
        grid=(B, S // ts),
        in_specs=[slab(BLK_GQ), slab(BLK_DQ), slab(BLK_DK), slab(BLK_DV), slab(BLK_KV), tab, tab,
                  const((1, BRANCH)), const((1, BRANCH)), const((BRANCH, BRANCH))],
        out_specs=specs,
        compiler_params=_params("parallel", "parallel"),
        name="qk_prep",
    )(z, z, z, z, z, cosf, sins, qn, kn, ones_bd)


def _attn_kernel(*refs, sets, tq, tk, mode, lam_init):
    if mode == "diff":
        q_ref, k_ref, v_ref, l1q, l1k, l2q, l2k, dn_ref, o_ref, q_s, s_buf, acc_s = refs
    else:
        q_ref, k_ref, v_ref, o_ref, q_s, s_buf, acc_s = refs
    n_tiles = k_ref.shape[0] // tk
    for r in range(sets):
        q_s[:, r * tq:(r + 1) * tq] = q_ref[r * LANE:(r + 1) * LANE, :]

    def scores(t):
        st = jnp.dot(k_ref[t * tk:(t + 1) * tk, :], q_s[...], preferred_element_type=F32)
        s_buf[t % 2] = st
        return jnp.max(st, axis=0, keepdims=True)

    m_run = jnp.full((1, sets * tq), -jnp.inf, F32)
    l_run = jnp.zeros((1, sets * tq), F32)
    m_tile = scores(0)
    for t in range(n_tiles):
        m_next = scores(t + 1) if t + 1 < n_tiles else None
        m_new = jnp.maximum(m_run, m_tile)
        alpha = jnp.exp2(m_run - m_new)
        pt = jnp.exp2(s_buf[t % 2] - m_new)
        if mode == "diff":
            l_run = alpha * l_run + jnp.sum(pt, axis=0, keepdims=True)
        pv = jnp.dot(v_ref[:, t * tk:(t + 1) * tk], pt.astype(BF16), preferred_element_type=F32)
        acc_s[...] = pv if t == 0 else alpha * acc_s[...] + pv
        m_run, m_tile = m_new, m_next

    ot = acc_s[...] / (l_run if mode == "diff" else acc_s[HEAD_DIM:HEAD_DIM + 1, :])
    o = [ot[:, r * tq:(r + 1) * tq].T for r in range(sets)]
    if mode == "diff":
        lam = (jnp.exp(jnp.sum(l1q[...] * l1k[...], axis=1, keepdims=True))
               - jnp.exp(jnp.sum(l2q[...] * l2k[...], axis=1, keepdims=True)) + lam_init)
        d = o[0] - lam * o[1]
        d = d * lax.rsqrt(jnp.mean(d * d, axis=1, keepdims=True) + EPS) * dn_ref[...]
        o_ref[...] = (d * (1.0 - lam_init)).astype(o_ref.dtype)
    else:
        lane = lax.broadcasted_iota(jnp.int32, (tq, LANE), 1)
        for pair in range(sets // 2):
            packed = jnp.where(lane < HEAD_DIM, o[2 * pair], pltpu.roll(o[2 * pair + 1], HEAD_DIM, 1))
            o_ref[:, pair * LANE:(pair + 1) * LANE] = packed.astype(o_ref.dtype)


def _attention(qt, k, vt, mode, lam_vecs=None, diff_norm=None, lam_init=0.0):
    B, _, S = qt.shape
    NK = k.shape[1]
    tq = min(256, S)
    tk = 1408 if mode == "diff" else 768
    tk = tk if NK % tk == 0 else 256
    if mode == "diff":
        groups, sets, out_w = DIFF_HEADS, 2, LANE
        kmap = lambda b, g, i: (b, 0, g)
    else:
        groups, sets, out_w = GQA_KV_HEADS, GQA_GROUP, GQA_GROUP * HEAD_DIM
        kmap = lambda b, g, i: (b, 0, 0)
    in_specs = [pl.BlockSpec((None, sets * LANE, tq), lambda b, g, i: (b, g, i)),
                pl.BlockSpec((None, NK, LANE), kmap),
                pl.BlockSpec((None, LANE, NK), lambda b, g, i: (b, g, 0))]
    args = [qt, k, vt]
    if mode == "diff":
        in_specs += [pl.BlockSpec((1, HEAD_DIM), lambda b, g, i: (0, 0))] * 4
        in_specs += [pl.BlockSpec((1, LANE), lambda b, g, i: (0, 0))]
        args += list(lam_vecs) + [diff_norm]
    return pl.pallas_call(
        functools.partial(_attn_kernel, sets=sets, tq=tq, tk=tk, mode=mode, lam_init=lam_init),
        out_shape=jax.ShapeDtypeStruct((B, S, BRANCH), BF16),
        grid=(B, groups, S // tq),
        in_specs=in_specs,
        out_specs=pl.BlockSpec((None, tq, out_w), lambda b, g, i: (b, i, g)),
        scratch_shapes=[pltpu.VMEM((LANE, sets * tq), BF16),
                        pltpu.VMEM((2, tk, sets * tq), F32),
                        pltpu.VMEM((LANE, sets * tq), F32)],
        compiler_params=_params("parallel", "parallel", "parallel"),
        name="attn_" + mode,
    )(*args)


def _dft_cos_sin(n):
    idx = np.arange(n)
    ang = 2.0 * np.pi * ((idx[:, None] * idx[None, :]) % n) / n
    return np.cos(ang), np.sin(ang)


def _channel_dft_matrix():
    c, s = _dft_cos_sin(GRID_W)
    groups = BRANCH // GRID_W
    eye = np.eye(groups)
    return np.concatenate([np.kron(eye, c), -np.kron(eye, s)], axis=1) / math.sqrt(GRID_W)


def _fourier_a_kernel(z_ref, w0_ref, d1_ref, tc_ref, ts_ref, o_ref):
    n1, nb, _ = z_ref.shape
    u = jnp.dot(z_ref[...].reshape(n1 * nb, BRANCH), w0_ref[...], preferred_element_type=F32)
    u = pltpu.einshape("kjm->jkm", u.reshape(n1, nb, 2 * BRANCH)).astype(BF16)
    reps = BRANCH // LANE
    out_r, out_i = [], []
    for j in range(nb):
        p = jnp.dot(d1_ref[...], u[j], preferred_element_type=F32)
        vr = p[:n1, :BRANCH] + p[n1:, BRANCH:]
        vi = p[:n1, BRANCH:] - p[n1:, :BRANCH]
        c = jnp.concatenate([tc_ref[j]] * reps, axis=1)
        s = jnp.concatenate([ts_ref[j]] * reps, axis=1)
        out_r.append(vr * c + vi * s)
        out_i.append(vi * c - vr * s)
    o_ref[0] = pltpu.einshape("jkm->kjm", jnp.stack(out_r)).astype(o_ref.dtype)
    o_ref[1] = pltpu.einshape("jkm->kjm", jnp.stack(out_i)).astype(o_ref.dtype)


def _fourier_b_kernel(v_ref, c3_ref, s3_ref, o_ref):
    nb = v_ref.shape[1]
    outs = [jnp.dot(c3_ref[...], v_ref[0, j], preferred_element_type=F32)
            + jnp.dot(s3_ref[...], v_ref[1, j], preferred_element_type=F32) for j in range(nb)]
    o_ref[...] = pltpu.einshape("jkm->kjm", jnp.stack(outs)).astype(o_ref.dtype)


def _bmm_kernel(a_ref, x_ref, o_ref):
    o_ref[...] = jnp.dot(a_ref[...], x_ref[...], preferred_element_type=F32).astype(o_ref.dtype)


def _bmm(a, x, tn):
    M, K = a.shape
    B, _, N = x.shape
    tn = min(tn, N)
    return pl.pallas_call(
        _bmm_kernel,
        out_shape=jax.ShapeDtypeStruct((B, M, N), BF16),
        grid=(B, N // tn),
        in_specs=[pl.BlockSpec((M, K), lambda b, j: (0, 0)),
                  pl.BlockSpec((None, K, tn), lambda b, j: (b, 0, j))],
        out_specs=pl.BlockSpec((None, M, tn), lambda b, j: (b, 0, j)),
        compiler_params=_params("parallel", "parallel"),
        name="bmm",
    )(a, x)


def _slab_matmul_kernel(z_ref, w_ref, o_ref):
    o_ref[...] = jnp.dot(z_ref[...], w_ref[...], preferred_element_type=F32).astype(o_ref.dtype)


def _slab_matmul(z, blk, w):
    B, S, _ = z.shape
    K, N = w.shape
    ts = min(256, S)
    return pl.pallas_call(
        _slab_matmul_kernel,
        out_shape=jax.ShapeDtypeStruct((B, S, N), BF16),
        grid=(B, S // ts),
        in_specs=[pl.BlockSpec((None, ts, K), lambda b, i: (b, i, blk)),
                  pl.BlockSpec((K, N), lambda b, i: (0, 0))],
        out_specs=pl.BlockSpec((None, ts, N), lambda b, i: (b, i, 0)),
        compiler_params=_params("parallel", "parallel"),
        name="slab_matmul",
    )(z, w)


def _fourier_long(z, w0):
    B, S, _ = z.shape
    n2 = GRID_W
    n1 = S // n2
    nb = FOURIER_NB
    assert n1 % nb == 0 and n2 % nb == 0
    c1, s1 = _dft_cos_sin(n1)
    d1 = jnp.asarray(np.concatenate([c1, s1], axis=0) / math.sqrt(n1), F32).astype(BF16)
    ang = 2.0 * np.pi * (np.arange(n2)[:, None] * np.arange(n1)[None, :]) / S
    twc = jnp.asarray(np.repeat(np.cos(ang)[:, :, None], LANE, axis=2), F32)
    tws = jnp.asarray(np.repeat(np.sin(ang)[:, :, None], LANE, axis=2), F32)
    v = pl.pallas_call(
        _fourier_a_kernel,
        out_shape=jax.ShapeDtypeStruct((B, 2, n1, n2, BRANCH), BF16),
        grid=(B, n2 // nb),
        in_specs=[pl.BlockSpec((None, n1, nb, BRANCH), lambda b, j: (b, 0, j, BLK_FOUR)),
                  pl.BlockSpec((BRANCH, 2 * BRANCH), lambda b, j: (0, 0)),
                  pl.BlockSpec((2 * n1, n1), lambda b, j: (0, 0)),
                  pl.BlockSpec((nb, n1, LANE), lambda b, j: (j, 0, 0)),
                  pl.BlockSpec((nb, n1, LANE), lambda b, j: (j, 0, 0))],
        out_specs=pl.BlockSpec((None, 2, n1, nb, BRANCH), lambda b, j: (b, 0, 0, j, 0)),
        compiler_params=_params("parallel", "parallel"),
        name="fourier_a",
    )(z.reshape(B, n1, n2, ZC), w0, d1, twc, tws)
    c3, s3 = _dft_cos_sin(n2)
    c3 = jnp.asarray(c3 / math.sqrt(n2), F32).astype(BF16)
    s3 = jnp.asarray(s3 / math.sqrt(n2), F32).astype(BF16)
    f = pl.pallas_call(
        _fourier_b_kernel,
        out_shape=jax.ShapeDtypeStruct((B, n2, n1, BRANCH), BF16),
        grid=(B, n1 // nb),
        in_specs=[pl.BlockSpec((None, 2, nb, n2, BRANCH), lambda b, i: (b, 0, i, 0, 0)),
                  pl.BlockSpec((n2, n2), lambda b, i: (0, 0)),
                  pl.BlockSpec((n2, n2), lambda b, i: (0, 0))],
        out_specs=pl.BlockSpec((None, n2, nb, BRANCH), lambda b, i: (b, 0, i, 0)),
        compiler_params=_params("parallel", "parallel"),
        name="fourier_b",
    )(v, c3, s3)
    return f.reshape(B, S, BRANCH)


def _fourier_short(z, w0):
    B, S, _ = z.shape
    u = _slab_matmul(z, BLK_FOUR, w0)
    ust = jnp.concatenate([u[:, :, :BRANCH], u[:, :, BRANCH:]], axis=1)
    c, s = _dft_cos_sin(S)
    dl = jnp.asarray(np.concatenate([c, s], axis=1) / math.sqrt(S), F32).astype(BF16)
    return _bmm(dl, ust, BRANCH)


def _rope_tables(n_rows):
    half = HEAD_DIM // 4
    inv = ROPE_BASE ** (-np.arange(0, HEAD_DIM // 2, 2, dtype=np.float64) / (HEAD_DIM // 2))
    row = np.repeat(np.arange(n_rows, dtype=np.float64), GRID_W)
    col = np.tile(np.arange(GRID_W, dtype=np.float64), n_rows)
    ang = np.concatenate([row[:, None] * inv, col[:, None] * inv], axis=-1)
    assert ang.shape[1] == 2 * half
    cos, sin = np.cos(ang), np.sin(ang)
    cosf = np.tile(np.concatenate([cos, cos], axis=1), (1, LANE // HEAD_DIM))
    sins = np.tile(np.concatenate([-sin, sin], axis=1), (1, LANE // HEAD_DIM))
    return jnp.asarray(cosf, F32), jnp.asarray(sins, F32)


def _permute_w_in(w):
    D = w.shape[0]
    four, conv, gq, gk, gv, dq, dk, dv, gates = jnp.split(
        w, np.cumsum([BRANCH, 2 * BRANCH, BRANCH, LANE, LANE, BRANCH, BRANCH, BRANCH])[:], axis=1)
    pad = jnp.zeros((D, BRANCH - 2 * LANE), w.dtype)
    out = jnp.concatenate([four, conv, gq, dq, dk, dv, gk, gv, pad, gates], axis=1)
    assert out.shape[1] == ZC
    return out.astype(BF16)


def _mixer_branches(z, lw, rope_tabs, ones_bd, w0, long_seq):
    f = _fourier_long(z, w0) if long_seq else _fourier_short(z, w0)
    cv = _conv_branch(z, lw["conv_w"], lw["conv_b"], lw["conv_ln_g"], lw["conv_ln_b"])
    qg, kg, vg, qd, kd, vd = _prep(z, rope_tabs[0], rope_tabs[1], lw["q_norm"], lw["k_norm"], ones_bd)
    return f, cv, qg, kg, vg, qd, kd, vd


def kernel(x, c, ctx, c_ctx, w_ada, b_ada, norm_mix, w_in, w_four, conv_w, conv_b, conv_ln_g, conv_ln_b, w_conv, q_norm, k_norm, w_gqa, lam_q1, lam_k1, lam_q2, lam_k2, diff_norm, w_diff, w_out, norm_ffn, w_ffn1, w_ffn3, w_ffn2, final_norm):
    B, S, D = x.shape
    Sc = ctx.shape[1]
    depth = w_ada.shape[0]
    assert B + 1 <= 8 and S % GRID_W == 0 and D == 2 * BRANCH

    cond = jnp.zeros((8, D), F32).at[:B].set(c).at[B].set(c_ctx)
    mods = _ada_mod(cond, w_ada, b_ada)

    rope_x = _rope_tables(S // GRID_W)
    rope_c = (jnp.ones((Sc, LANE), F32), jnp.zeros((Sc, LANE), F32))
    ones_bd = jnp.asarray(np.kron(np.eye(BRANCH // HEAD_DIM), np.ones((HEAD_DIM, HEAD_DIM))), F32).astype(BF16)
    w0 = jnp.asarray(_channel_dft_matrix(), F32).astype(BF16)
    tile2 = lambda v: jnp.tile(v, LANE // HEAD_DIM * (BRANCH // LANE)).reshape(1, BRANCH)

    for l in range(depth):
        last = l == depth - 1
        lam_init = 0.8 - 0.6 * math.exp(-0.3 * l)
        mx = [m.reshape(B, 1, D) for m in jnp.split(mods[l, :B], 6, axis=-1)]
        mc = [jnp.broadcast_to(m.reshape(1, 1, D), (B, 1, D)) for m in jnp.split(mods[l, B], 6, axis=-1)]
        lw = dict(conv_w=conv_w[l], conv_b=conv_b[l].reshape(1, BRANCH),
                  conv_ln_g=conv_ln_g[l].reshape(1, BRANCH), conv_ln_b=conv_ln_b[l].reshape(1, BRANCH),
                  q_norm=tile2(q_norm[l]), k_norm=tile2(k_norm[l]))
        w_in_l = _permute_w_in(w_in[l])
        wf, wc, wa, wd, wo = (w.astype(BF16) for w in (w_four[l], w_conv[l], w_gqa[l], w_diff[l], w_out[l]))
        w1, w3, w2 = (w.astype(BF16) for w in (w_ffn1[l], w_ffn3[l], w_ffn2[l]))
        gain_m, gain_f = norm_mix[l].reshape(1, D), norm_ffn[l].reshape(1, D)
        lam_vecs = [v[l].reshape(1, HEAD_DIM) for v in (lam_q1, lam_k1, lam_q2, lam_k2)]
        dn = diff_norm[l].reshape(1, LANE)
        fin = final_norm.reshape(1, D)

        zx = _inproj(x, gain_m, mx[0], mx[1], w_in_l)
        zc = _inproj(ctx, gain_m, mc[0], mc[1], w_in_l)
        fx, cvx, qgx, kgx, vgx, qdx, kdx, vdx = _mixer_branches(zx, lw, rope_x, ones_bd, w0, True)
        fc, cvc, qgc, kgc, vgc, qdc, kdc, vdc = _mixer_branches(zc, lw, rope_c, ones_bd, w0, False)

        cat = lambda a, b, axis: jnp.concatenate([a, b], axis=axis)
        ogx = _attention(qgx, cat(kgx, kgc, 1), cat(vgx, vgc, 2), "gqa")
        odx = _attention(qdx, cat(kdx, kdc, 1), cat(vdx, vdc, 2), "diff", lam_vecs, dn, lam_init)
        x = _merge(x, mx[2], fx, cvx, ogx, odx, zx, wf, wc, wa, wd, wo)
        if not last:
            ogc = _attention(qgc, kgc, vgc, "gqa")
            odc = _attention(qdc, kdc, vdc, "diff", lam_vecs, dn, lam_init)
            ctx = _merge(ctx, mc[2], fc, cvc, ogc, odc, zc, wf, wc, wa, wd, wo)
            ctx = _ffn(ctx, gain_f, mc[3], mc[4], mc[5], w1, w3, w2, fin, False)
        x = _ffn(x, gain_f, mx[3], mx[4], mx[5], w1, w3, w2, fin, last)
    return x
```

```python
import functools
import math

import numpy as np
import jax
import jax.numpy as jnp
from jax import lax
from jax.experimental import pallas as pl
from jax.experimental.pallas import tpu as pltpu

F32 = jnp.float32
BF16 = jnp.bfloat16

HEAD_DIM = 64
BRANCH = 512
GQA_GROUP = 4
GQA_KV_HEADS = 2
DIFF_HEADS = 4
N_BRANCH = 4
GRID_W = 64
CONV_K = 31
ROPE_BASE = 10000.0
EPS = 1e-6
LN_EPS = 1e-5

LANE = 128
SUBLANE = 8
BF16_SUBLANES = 16
VMEM_LIMIT = 56 * 1024 * 1024

ZC = 8192
BLK_FOUR, BLK_CA, BLK_CG, BLK_GQ, BLK_DQ, BLK_DK, BLK_DV, BLK_KV = range(8)
GATE_COL0 = 4096
FOURIER_NB = 16
CONV_HALO = 16


def _params(*sem):
    return pltpu.CompilerParams(dimension_semantics=sem, vmem_limit_bytes=VMEM_LIMIT)


def _silu(x):
    return x * jax.nn.sigmoid(x)


def _rms_mod(x, g, shift, scale):
    y = x * lax.rsqrt(jnp.mean(x * x, axis=-1, keepdims=True) + EPS) * g
    return y * (1.0 + scale) + shift


def _ada_kernel(c_ref, w_ref, b_ref, o_ref):
    a = _silu(c_ref[...])
    o_ref[...] = jnp.dot(a, w_ref[...], precision=lax.Precision.HIGHEST,
                         preferred_element_type=F32) + b_ref[...]


def _ada_mod(cond, w_ada, b_ada):
    L, D, N = w_ada.shape
    tn = 1536
    return pl.pallas_call(
        _ada_kernel,
        out_shape=jax.ShapeDtypeStruct((L, 8, N), F32),
        grid=(L, N // tn),
        in_specs=[pl.BlockSpec((8, D), lambda l, j: (0, 0)),
                  pl.BlockSpec((None, D, tn), lambda l, j: (l, 0, j)),
                  pl.BlockSpec((None, 1, tn), lambda l, j: (l, 0, j))],
        out_specs=pl.BlockSpec((None, 8, tn), lambda l, j: (l, 0, j)),
        compiler_params=_params("parallel", "parallel"),
        name="ada_mod",
    )(cond, w_ada, b_ada.reshape(L, 1, N))


def _inproj_kernel(x_ref, g_ref, sh_ref, sc_ref, w_ref, o_ref):
    h = _rms_mod(x_ref[...], g_ref[...], sh_ref[...], sc_ref[...])
    o_ref[...] = jnp.dot(h.astype(BF16), w_ref[...], preferred_element_type=F32).astype(o_ref.dtype)


def _inproj(x, gain, shift, scale, w):
    B, S, D = x.shape
    N = w.shape[1]
    tm = min(256, S)
    tn = 4096
    return pl.pallas_call(
        _inproj_kernel,
        out_shape=jax.ShapeDtypeStruct((B, S, N), BF16),
        grid=(N // tn, B, S // tm),
        in_specs=[pl.BlockSpec((None, tm, D), lambda j, b, i: (b, i, 0)),
                  pl.BlockSpec((1, D), lambda j, b, i: (0, 0)),
                  pl.BlockSpec((None, 1, D), lambda j, b, i: (b, 0, 0)),
                  pl.BlockSpec((None, 1, D), lambda j, b, i: (b, 0, 0)),
                  pl.BlockSpec((D, tn), lambda j, b, i: (0, j))],
        out_specs=pl.BlockSpec((None, tm, tn), lambda j, b, i: (b, i, j)),
        compiler_params=_params("parallel", "parallel", "parallel"),
        name="in_proj",
    )(x, gain, shift, scale, w)


def _ffn_kernel(x_ref, g_ref, sh_ref, sc_ref, gate_ref, w1_ref, w3_ref, w2_ref, fg_ref, o_ref, *, final):
    x = x_ref[...]
    h = _rms_mod(x, g_ref[...], sh_ref[...], sc_ref[...]).astype(BF16)
    a = jnp.dot(h, w1_ref[...], preferred_element_type=F32)
    b = jnp.dot(h, w3_ref[...], preferred_element_type=F32)
    u = (_silu(a) * b).astype(BF16)
    y = x + gate_ref[...] * jnp.dot(u, w2_ref[...], preferred_element_type=F32)
    if final:
        y = y * lax.rsqrt(jnp.mean(y * y, axis=-1, keepdims=True) + EPS) * fg_ref[...]
    o_ref[...] = y


def _ffn(x, gain, shift, scale, gate, w1, w3, w2, final_gain, final):
    B, S, D = x.shape
    H = w1.shape[1]
    tm = min(256, S)
    const = lambda shape: pl.BlockSpec(shape, lambda b, i: (0,) * len(shape), pipeline_mode=pl.Buffered(1))
    per_b = pl.BlockSpec((None, 1, D), lambda b, i: (b, 0, 0))
    return pl.pallas_call(
        functools.partial(_ffn_kernel, final=final),
        out_shape=jax.ShapeDtypeStruct((B, S, D), F32),
        grid=(B, S // tm),
        in_specs=[pl.BlockSpec((None, tm, D), lambda b, i: (b, i, 0)),
                  const((1, D)), per_b, per_b, per_b,
                  const((D, H)), const((D, H)), const((H, D)), const((1, D))],
        out_specs=pl.BlockSpec((None, tm, D), lambda b, i: (b, i, 0)),
        compiler_params=_params("parallel", "parallel"),
        name="ffn",
    )(x, gain, shift, scale, gate, w1, w3, w2, final_gain)


def _merge_kernel(x_ref, gm_ref, f_ref, c_ref, a_ref, d_ref, zg_ref,
                  wf_ref, wc_ref, wa_ref, wd_ref, wo_ref, o_ref):
    D = x_ref.shape[-1]
    acc = None
    for b, (br, w) in enumerate(((f_ref, wf_ref), (c_ref, wc_ref), (a_ref, wa_ref), (d_ref, wd_ref))):
        y = jnp.dot(br[...], w[...], preferred_element_type=F32)
        gate = jax.nn.sigmoid(zg_ref[:, b * D:(b + 1) * D].astype(F32))
        acc = gate * y if acc is None else acc + gate * y
    out = jnp.dot(acc.astype(BF16), wo_ref[...], preferred_element_type=F32)
    o_ref[...] = x_ref[...] + gm_ref[...] * out


def _merge(x, gm, f, cv, og, od, z, wf, wc, wa, wd, wo):
    B, S, D = x.shape
    tm = min(256, S)
    row = lambda width: pl.BlockSpec((None, tm, width), lambda b, i: (b, i, 0))
    const = lambda shape: pl.BlockSpec(shape, lambda b, i: (0, 0))
    return pl.pallas_call(
        _merge_kernel,
        out_shape=jax.ShapeDtypeStruct((B, S, D), F32),
        grid=(B, S // tm),
        in_specs=[row(D), pl.BlockSpec((None, 1, D), lambda b, i: (b, 0, 0)),
                  row(BRANCH), row(BRANCH), row(BRANCH), row(BRANCH),
                  pl.BlockSpec((None, tm, N_BRANCH * D), lambda b, i: (b, i, GATE_COL0 // (N_BRANCH * D))),
                  const((BRANCH, D)), const((BRANCH, D)), const((BRANCH, D)), const((BRANCH, D)),
                  const((D, D))],
        out_specs=row(D),
        compiler_params=_params("parallel", "parallel"),
        name="merge",
    )(x, gm, f, cv, og, od, z, wf, wc, wa, wd, wo)


def _conv_kernel(a_ref, g_ref, ap_ref, gp_ref, an_ref, gn_ref, w_ref, b_ref, lg_ref, lb_ref, o_ref, u_s,
                 *, ts, chunk):
    i = pl.program_id(1)
    n = pl.num_programs(1)

    def glu(a, g):
        return a.astype(F32) * jax.nn.sigmoid(g.astype(F32))

    u_s[CONV_HALO:CONV_HALO + ts, :] = glu(a_ref[...], g_ref[...])
    u_s[0:CONV_HALO, :] = jnp.where(i > 0, glu(ap_ref[...], gp_ref[...]), 0.0)
    u_s[CONV_HALO + ts:2 * CONV_HALO + ts, :] = jnp.where(i < n - 1, glu(an_ref[...], gn_ref[...]), 0.0)
    pad = CONV_K // 2

    def body(c, carry):
        r0 = pl.multiple_of(c * chunk, chunk)
        win = u_s[pl.ds(r0, chunk + 2 * CONV_HALO), :]
        rows = chunk + 2 * CONV_HALO
        acc = jnp.zeros((chunk // SUBLANE, SUBLANE, BRANCH), F32) + b_ref[...]
        for r in range(SUBLANE):
            taps = [j for j in range(CONV_K) if (CONV_HALO - pad + j) % SUBLANE == r]
            shifted = win if r == 0 else pltpu.roll(win, rows - r, 0)
            for j in taps:
                base = CONV_HALO - pad + j - r
                tap = shifted[base:base + chunk, :].reshape(chunk // SUBLANE, SUBLANE, BRANCH)
                acc = acc + w_ref[j] * tap
        acc = acc.reshape(chunk, BRANCH)
        mu = jnp.mean(acc, axis=-1, keepdims=True)
        d = acc - mu
        var = jnp.mean(d * d, axis=-1, keepdims=True)
        y = d * lax.rsqrt(var + LN_EPS) * lg_ref[...] + lb_ref[...]
        o_ref[pl.ds(r0, chunk), :] = _silu(y).astype(o_ref.dtype)
        return carry

    lax.fori_loop(0, ts // chunk, body, 0)


def _conv_branch(z, w, b, ln_g, ln_b):
    B, S, _ = z.shape
    ts = min(512, S)
    chunk = min(256, ts)
    hb = ts // CONV_HALO
    last = S // CONV_HALO - 1
    cur = lambda blk: pl.BlockSpec((None, ts, BRANCH), lambda b_, i: (b_, i, blk))
    prev = lambda blk: pl.BlockSpec((None, CONV_HALO, BRANCH),
                                    lambda b_, i: (b_, jnp.maximum(i * hb - 1, 0), blk))
    nxt = lambda blk: pl.BlockSpec((None, CONV_HALO, BRANCH),
                                   lambda b_, i: (b_, jnp.minimum((i + 1) * hb, last), blk))
    const = lambda shape: pl.BlockSpec(shape, lambda b_, i: (0, 0))
    return pl.pallas_call(
        functools.partial(_conv_kernel, ts=ts, chunk=chunk),
        out_shape=jax.ShapeDtypeStruct((B, S, BRANCH), BF16),
        grid=(B, S // ts),
        in_specs=[cur(BLK_CA), cur(BLK_CG), prev(BLK_CA), prev(BLK_CG), nxt(BLK_CA), nxt(BLK_CG),
                  pl.BlockSpec((CONV_K, SUBLANE, BRANCH), lambda b_, i: (0, 0, 0)),
                  const((1, BRANCH)), const((1, BRANCH)), const((1, BRANCH))],
        out_specs=pl.BlockSpec((None, ts, BRANCH), lambda b_, i: (b_, i, 0)),
        scratch_shapes=[pltpu.VMEM((ts + 2 * CONV_HALO, BRANCH), F32)],
        compiler_params=_params("parallel", "parallel"),
        name="conv_branch",
    )(z, z, z, z, z, z, jnp.broadcast_to(w[:, None, :], (CONV_K, SUBLANE, BRANCH)), b, ln_g, ln_b)


def _head_rms(x, ones_bd):
    x2 = x * x
    hi = x2.astype(BF16)
    lo = (x2 - hi.astype(F32)).astype(BF16)
    ssum = (jnp.dot(hi, ones_bd, preferred_element_type=F32)
            + jnp.dot(lo, ones_bd, preferred_element_type=F32))
    return x * lax.rsqrt(ssum * (1.0 / HEAD_DIM) + EPS)


def _rope(x, cosf, sins):
    width = x.shape[1]
    reps = width // LANE
    c = jnp.concatenate([cosf] * reps, axis=1) if reps > 1 else cosf
    s = jnp.concatenate([sins] * reps, axis=1) if reps > 1 else sins
    lane = lax.broadcasted_iota(jnp.int32, x.shape, 1)
    first_half = (lane & (HEAD_DIM - 1)) < HEAD_DIM // 2
    partner = jnp.where(first_half, pltpu.roll(x, width - HEAD_DIM // 2, 1), pltpu.roll(x, HEAD_DIM // 2, 1))
    return x * c + partner * s


def _prep_kernel(gq_ref, dq_ref, dk_ref, dv_ref, kv_ref, cos_ref, sin_ref, qn_ref, kn_ref, bd_ref,
                 qg_ref, kg_ref, vg_ref, qd_ref, kd_ref, vd_ref):
    cosf, sins = cos_ref[...], sin_ref[...]
    bd = bd_ref[...]
    scale = HEAD_DIM ** -0.5 * math.log2(math.e)
    lane = lax.broadcasted_iota(jnp.int32, (gq_ref.shape[0], LANE), 1)
    low = lane < HEAD_DIM
    zero = jnp.zeros((gq_ref.shape[0], LANE), F32)

    q = _rope(_head_rms(gq_ref[...].astype(F32), bd) * qn_ref[...], cosf, sins) * scale
    for p in range(BRANCH // LANE):
        src = q[:, p * LANE:(p + 1) * LANE]
        swapped = pltpu.roll(src, HEAD_DIM, 1)
        for e in range(2):
            h = 2 * p + e
            kvh = h // GQA_GROUP
            val = src if e == kvh else swapped
            val = jnp.where(low, val, zero) if kvh == 0 else jnp.where(low, zero, val)
            qg_ref[h * LANE:(h + 1) * LANE, :] = val.T.astype(qg_ref.dtype)

    kvz = kv_ref[...].astype(F32)
    k = _rope(_head_rms(kvz[:, :LANE], bd[:LANE, :LANE]) * kn_ref[:, :LANE], cosf, sins)
    kg_ref[...] = k.astype(kg_ref.dtype)
    v = kvz[:, LANE:2 * LANE]
    pad = jnp.where(lane == HEAD_DIM, 1.0, 0.0)
    vg_ref[:LANE, :] = jnp.where(low, v, pad).T.astype(vg_ref.dtype)
    vg_ref[LANE:, :] = jnp.where(low, pltpu.roll(v, HEAD_DIM, 1), pad).T.astype(vg_ref.dtype)

    qd = _rope(dq_ref[...].astype(F32), cosf, sins) * scale
    for h in range(DIFF_HEADS):
        src = qd[:, h * LANE:(h + 1) * LANE]
        qd_ref[(2 * h) * LANE:(2 * h + 1) * LANE, :] = jnp.where(low, src, zero).T.astype(qd_ref.dtype)
        qd_ref[(2 * h + 1) * LANE:(2 * h + 2) * LANE, :] = jnp.where(low, zero, src).T.astype(qd_ref.dtype)
        vd_ref[h * LANE:(h + 1) * LANE, :] = dv_ref[:, h * LANE:(h + 1) * LANE].astype(F32).T.astype(vd_ref.dtype)
    kd_ref[...] = _rope(dk_ref[...].astype(F32), cosf, sins).astype(kd_ref.dtype)


def _prep(z, cosf, sins, qn, kn, ones_bd):
    B, S, _ = z.shape
    ts = min(256, S)
    slab = lambda blk: pl.BlockSpec((None, ts, BRANCH), lambda b, i: (b, i, blk))
    tab = pl.BlockSpec((ts, LANE), lambda b, i: (i, 0))
    const = lambda shape: pl.BlockSpec(shape, lambda b, i: (0, 0))
    rows = lambda width: pl.BlockSpec((None, ts, width), lambda b, i: (b, i, 0))
    cols = lambda chans: pl.BlockSpec((None, chans, ts), lambda b, i: (b, 0, i))
    shapes = [(B, 2 * BRANCH, S), (B, S, LANE), (B, 2 * LANE, S), (B, 2 * BRANCH, S), (B, S, BRANCH), (B, BRANCH, S)]
    specs = [cols(2 * BRANCH), rows(LANE), cols(2 * LANE), cols(2 * BRANCH), rows(BRANCH), cols(BRANCH)]
    return pl.pallas_call(
        _prep_kernel,
        out_shape=[jax.ShapeDtypeStruct(s, BF16) for s in shapes],
        grid=(B, S // ts),
        in_specs=[slab(BLK_GQ), slab(BLK_DQ), slab(BLK_DK), slab(BLK_DV), slab(BLK_KV), tab, tab,
                  const((1, BRANCH)), const((1, BRANCH)), const((BRANCH, BRANCH))],
        out_specs=specs,
        compiler_params=_params("parallel", "parallel"),
        name="qk_prep",
    )(z, z, z, z, z, cosf, sins, qn, kn, ones_bd)


def _attn_kernel(*refs, sets, tq, tk, mode, lam_init):
    if mode == "diff":
        q_ref, k_ref, v_ref, l1q, l1k, l2q, l2k, dn_ref, o_ref, q_s, acc_s = refs
    else:
        q_ref, k_ref, v_ref, o_ref, q_s, acc_s = refs
    n_tiles = k_ref.shape[0] // tk
    width = sets * tq
    for r in range(sets):
        q_s[:, r * tq:(r + 1) * tq] = q_ref[r * LANE:(r + 1) * LANE, :]

    def finish(l_run):
        ot = acc_s[...] / (l_run if mode == "diff" else acc_s[HEAD_DIM:HEAD_DIM + 1, :])
        o = [ot[:, r * tq:(r + 1) * tq].T for r in range(sets)]
        if mode == "diff":
            lam = (jnp.exp(jnp.sum(l1q[...] * l1k[...], axis=1, keepdims=True))
                   - jnp.exp(jnp.sum(l2q[...] * l2k[...], axis=1, keepdims=True)) + lam_init)
            d = o[0] - lam * o[1]
            d = d * lax.rsqrt(jnp.mean(d * d, axis=1, keepdims=True) + EPS) * dn_ref[...]
            o_ref[...] = (d * (1.0 - lam_init)).astype(o_ref.dtype)
        else:
            lane = lax.broadcasted_iota(jnp.int32, (tq, LANE), 1)
            for pair in range(sets // 2):
                packed = jnp.where(lane < HEAD_DIM, o[2 * pair], pltpu.roll(o[2 * pair + 1], HEAD_DIM, 1))
                o_ref[:, pair * LANE:(pair + 1) * LANE] = packed.astype(o_ref.dtype)

    m_run = l_run = None
    for t in range(n_tiles):
        st = jnp.dot(k_ref[t * tk:(t + 1) * tk, :], q_s[...], preferred_element_type=F32)
        m_tile = jnp.max(st, axis=0, keepdims=True)
        pt = jnp.exp2(st - (m_tile if t == 0 else m_run))
        pv = jnp.dot(v_ref[:, t * tk:(t + 1) * tk], pt.astype(BF16), preferred_element_type=F32)
        l_tile = jnp.sum(pt, axis=0, keepdims=True) if mode == "diff" else None
        if t == 0:
            m_run, l_run = m_tile, l_tile
            acc_s[...] = pv
        else:
            m_new = jnp.maximum(m_run, m_tile)
            alpha = jnp.exp2(m_run - m_new)
            if mode == "diff":
                l_run = (l_run + l_tile) * alpha
            acc_s[...] = (acc_s[...] + pv) * alpha
            m_run = m_new
    finish(l_run)

    finite = jnp.isfinite(acc_s[...])
    if mode == "diff":
        finite = jnp.logical_and(finite, jnp.isfinite(l_run))
    overflowed = jnp.max(jnp.where(finite, 0.0, 1.0)) > 0.0

    @pl.when(overflowed)
    def _exact():
        def body(t, carry):
            m_old, l_old = carry
            st = jnp.dot(k_ref[pl.ds(pl.multiple_of(t * tk, tk), tk), :], q_s[...], preferred_element_type=F32)
            m_new = jnp.maximum(m_old, jnp.max(st, axis=0, keepdims=True))
            alpha = jnp.exp2(m_old - m_new)
            pt = jnp.exp2(st - m_new)
            v_tile = v_ref[:, pl.ds(pl.multiple_of(t * tk, LANE), tk)]
            acc_s[...] = alpha * acc_s[...] + jnp.dot(v_tile, pt.astype(BF16), preferred_element_type=F32)
            return m_new, alpha * l_old + jnp.sum(pt, axis=0, keepdims=True)

        acc_s[...] = jnp.zeros(acc_s.shape, F32)
        init = (jnp.full((1, width), -jnp.inf, F32), jnp.zeros((1, width), F32))
        _, l_exact = lax.fori_loop(0, n_tiles, body, init)
        finish(l_exact)


def _attention(qt, k, vt, mode, lam_vecs=None, diff_norm=None, lam_init=0.0):
    B, _, S = qt.shape
    NK = k.shape[1]
    tq = min(512 if mode == "diff" else 256, S)
    tk = 768
    tk = tk if NK % tk == 0 else 256
    if mode == "diff":
        groups, sets, out_w = DIFF_HEADS, 2, LANE
        kmap = lambda b, g, i: (b, 0, g)
    else:
        groups, sets, out_w = GQA_KV_HEADS, GQA_GROUP, GQA_GROUP * HEAD_DIM
        kmap = lambda b, g, i: (b, 0, 0)
    in_specs = [pl.BlockSpec((None, sets * LANE, tq), lambda b, g, i: (b, g, i)),
                pl.BlockSpec((None, NK, LANE), kmap),
                pl.BlockSpec((None, LANE, NK), lambda b, g, i: (b, g, 0))]
    args = [qt, k, vt]
    if mode == "diff":
        in_specs += [pl.BlockSpec((1, HEAD_DIM), lambda b, g, i: (0, 0))] * 4
        in_specs += [pl.BlockSpec((1, LANE), lambda b, g, i: (0, 0))]
        args += list(lam_vecs) + [diff_norm]
    return pl.pallas_call(
        functools.partial(_attn_kernel, sets=sets, tq=tq, tk=tk, mode=mode, lam_init=lam_init),
        out_shape=jax.ShapeDtypeStruct((B, S, BRANCH), BF16),
        grid=(B, groups, S // tq),
        in_specs=in_specs,
        out_specs=pl.BlockSpec((None, tq, out_w), lambda b, g, i: (b, i, g)),
        scratch_shapes=[pltpu.VMEM((LANE, sets * tq), BF16),
                        pltpu.VMEM((LANE, sets * tq), F32)],
        compiler_params=_params("parallel", "parallel", "parallel"),
        name="attn_" + mode,
    )(*args)


def _dft_cos_sin(n):
    idx = np.arange(n)
    ang = 2.0 * np.pi * ((idx[:, None] * idx[None, :]) % n) / n
    return np.cos(ang), np.sin(ang)


def _channel_dft_matrix():
    c, s = _dft_cos_sin(GRID_W)
    groups = BRANCH // GRID_W
    eye = np.eye(groups)
    return np.concatenate([np.kron(eye, c), -np.kron(eye, s)], axis=1) / math.sqrt(GRID_W)


def _fourier_a_kernel(z_ref, w0_ref, d1_ref, tc_ref, ts_ref, o_ref):
    n1, nb, _ = z_ref.shape
    u = jnp.dot(z_ref[...].reshape(n1 * nb, BRANCH), w0_ref[...], preferred_element_type=F32)
    u = pltpu.einshape("kjm->jkm", u.reshape(n1, nb, 2 * BRANCH)).astype(BF16)
    reps = BRANCH // LANE
    out_r, out_i = [], []
    for j in range(nb):
        p = jnp.dot(d1_ref[...], u[j], preferred_element_type=F32)
        vr = p[:n1, :BRANCH] + p[n1:, BRANCH:]
        vi = p[:n1, BRANCH:] - p[n1:, :BRANCH]
        c = jnp.concatenate([tc_ref[j]] * reps, axis=1)
        s = jnp.concatenate([ts_ref[j]] * reps, axis=1)
        out_r.append(vr * c + vi * s)
        out_i.append(vi * c - vr * s)
    o_ref[0] = pltpu.einshape("jkm->kjm", jnp.stack(out_r)).astype(o_ref.dtype)
    o_ref[1] = pltpu.einshape("jkm->kjm", jnp.stack(out_i)).astype(o_ref.dtype)


def _fourier_b_kernel(v_ref, c3_ref, s3_ref, o_ref):
    nb = v_ref.shape[1]
    outs = [jnp.dot(c3_ref[...], v_ref[0, j], preferred_element_type=F32)
            + jnp.dot(s3_ref[...], v_ref[1, j], preferred_element_type=F32) for j in range(nb)]
    o_ref[...] = pltpu.einshape("jkm->kjm", jnp.stack(outs)).astype(o_ref.dtype)


def _bmm_kernel(a_ref, x_ref, o_ref):
    o_ref[...] = jnp.dot(a_ref[...], x_ref[...], preferred_element_type=F32).astype(o_ref.dtype)


def _bmm(a, x, tn):
    M, K = a.shape
    B, _, N = x.shape
    tn = min(tn, N)
    return pl.pallas_call(
        _bmm_kernel,
        out_shape=jax.ShapeDtypeStruct((B, M, N), BF16),
        grid=(B, N // tn),
        in_specs=[pl.BlockSpec((M, K), lambda b, j: (0, 0)),
                  pl.BlockSpec((None, K, tn), lambda b, j: (b, 0, j))],
        out_specs=pl.BlockSpec((None, M, tn), lambda b, j: (b, 0, j)),
        compiler_params=_params("parallel", "parallel"),
        name="bmm",
    )(a, x)


def _slab_matmul_kernel(z_ref, w_ref, o_ref):
    o_ref[...] = jnp.dot(z_ref[...], w_ref[...], preferred_element_type=F32).astype(o_ref.dtype)


def _slab_matmul(z, blk, w):
    B, S, _ = z.shape
    K, N = w.shape
    ts = min(256, S)
    return pl.pallas_call(
        _slab_matmul_kernel,
        out_shape=jax.ShapeDtypeStruct((B, S, N), BF16),
        grid=(B, S // ts),
        in_specs=[pl.BlockSpec((None, ts, K), lambda b, i: (b, i, blk)),
                  pl.BlockSpec((K, N), lambda b, i: (0, 0))],
        out_specs=pl.BlockSpec((None, ts, N), lambda b, i: (b, i, 0)),
        compiler_params=_params("parallel", "parallel"),
        name="slab_matmul",
    )(z, w)


def _fourier_long(z, w0):
    B, S, _ = z.shape
    n2 = GRID_W
    n1 = S // n2
    nb = FOURIER_NB
    assert n1 % nb == 0 and n2 % nb == 0
    c1, s1 = _dft_cos_sin(n1)
    d1 = jnp.asarray(np.concatenate([c1, s1], axis=0) / math.sqrt(n1), F32).astype(BF16)
    ang = 2.0 * np.pi * (np.arange(n2)[:, None] * np.arange(n1)[None, :]) / S
    twc = jnp.asarray(np.repeat(np.cos(ang)[:, :, None], LANE, axis=2), F32)
    tws = jnp.asarray(np.repeat(np.sin(ang)[:, :, None], LANE, axis=2), F32)
    v = pl.pallas_call(
        _fourier_a_kernel,
        out_shape=jax.ShapeDtypeStruct((B, 2, n1, n2, BRANCH), BF16),
        grid=(B, n2 // nb),
        in_specs=[pl.BlockSpec((None, n1, nb, BRANCH), lambda b, j: (b, 0, j, BLK_FOUR)),
                  pl.BlockSpec((BRANCH, 2 * BRANCH), lambda b, j: (0, 0)),
                  pl.BlockSpec((2 * n1, n1), lambda b, j: (0, 0)),
                  pl.BlockSpec((nb, n1, LANE), lambda b, j: (j, 0, 0)),
                  pl.BlockSpec((nb, n1, LANE), lambda b, j: (j, 0, 0))],
        out_specs=pl.BlockSpec((None, 2, n1, nb, BRANCH), lambda b, j: (b, 0, 0, j, 0)),
        compiler_params=_params("parallel", "parallel"),
        name="fourier_a",
    )(z.reshape(B, n1, n2, ZC), w0, d1, twc, tws)
    c3, s3 = _dft_cos_sin(n2)
    c3 = jnp.asarray(c3 / math.sqrt(n2), F32).astype(BF16)
    s3 = jnp.asarray(s3 / math.sqrt(n2), F32).astype(BF16)
    f = pl.pallas_call(
        _fourier_b_kernel,
        out_shape=jax.ShapeDtypeStruct((B, n2, n1, BRANCH), BF16),
        grid=(B, n1 // nb),
        in_specs=[pl.BlockSpec((None, 2, nb, n2, BRANCH), lambda b, i: (b, 0, i, 0, 0)),
                  pl.BlockSpec((n2, n2), lambda b, i: (0, 0)),
                  pl.BlockSpec((n2, n2), lambda b, i: (0, 0))],
        out_specs=pl.BlockSpec((None, n2, nb, BRANCH), lambda b, i: (b, 0, i, 0)),
        compiler_params=_params("parallel", "parallel"),
        name="fourier_b",
    )(v, c3, s3)
    return f.reshape(B, S, BRANCH)


def _fourier_short(z, w0):
    B, S, _ = z.shape
    u = _slab_matmul(z, BLK_FOUR, w0)
    ust = jnp.concatenate([u[:, :, :BRANCH], u[:, :, BRANCH:]], axis=1)
    c, s = _dft_cos_sin(S)
    dl = jnp.asarray(np.concatenate([c, s], axis=1) / math.sqrt(S), F32).astype(BF16)
    return _bmm(dl, ust, BRANCH)


def _rope_tables(n_rows):
    half = HEAD_DIM // 4
    inv = ROPE_BASE ** (-np.arange(0, HEAD_DIM // 2, 2, dtype=np.float64) / (HEAD_DIM // 2))
    row = np.repeat(np.arange(n_rows, dtype=np.float64), GRID_W)
    col = np.tile(np.arange(GRID_W, dtype=np.float64), n_rows)
    ang = np.concatenate([row[:, None] * inv, col[:, None] * inv], axis=-1)
    assert ang.shape[1] == 2 * half
    cos, sin = np.cos(ang), np.sin(ang)
    cosf = np.tile(np.concatenate([cos, cos], axis=1), (1, LANE // HEAD_DIM))
    sins = np.tile(np.concatenate([-sin, sin], axis=1), (1, LANE // HEAD_DIM))
    return jnp.asarray(cosf, F32), jnp.asarray(sins, F32)


def _permute_w_in(w):
    D = w.shape[0]
    four, conv, gq, gk, gv, dq, dk, dv, gates = jnp.split(
        w, np.cumsum([BRANCH, 2 * BRANCH, BRANCH, LANE, LANE, BRANCH, BRANCH, BRANCH])[:], axis=1)
    pad = jnp.zeros((D, BRANCH - 2 * LANE), w.dtype)
    out = jnp.concatenate([four, conv, gq, dq, dk, dv, gk, gv, pad, gates], axis=1)
    assert out.shape[1] == ZC
    return out.astype(BF16)


def _mixer_branches(z, lw, rope_tabs, ones_bd, w0, long_seq):
    f = _fourier_long(z, w0) if long_seq else _fourier_short(z, w0)
    cv = _conv_branch(z, lw["conv_w"], lw["conv_b"], lw["conv_ln_g"], lw["conv_ln_b"])
    qg, kg, vg, qd, kd, vd = _prep(z, rope_tabs[0], rope_tabs[1], lw["q_norm"], lw["k_norm"], ones_bd)
    return f, cv, qg, kg, vg, qd, kd, vd


def kernel(x, c, ctx, c_ctx, w_ada, b_ada, norm_mix, w_in, w_four, conv_w, conv_b, conv_ln_g, conv_ln_b, w_conv, q_norm, k_norm, w_gqa, lam_q1, lam_k1, lam_q2, lam_k2, diff_norm, w_diff, w_out, norm_ffn, w_ffn1, w_ffn3, w_ffn2, final_norm):
    B, S, D = x.shape
    Sc = ctx.shape[1]
    depth = w_ada.shape[0]
    assert B + 1 <= 8 and S % GRID_W == 0 and D == 2 * BRANCH

    cond = jnp.zeros((8, D), F32).at[:B].set(c).at[B].set(c_ctx)
    mods = _ada_mod(cond, w_ada, b_ada)

    rope_x = _rope_tables(S // GRID_W)
    rope_c = (jnp.ones((Sc, LANE), F32), jnp.zeros((Sc, LANE), F32))
    ones_bd = jnp.asarray(np.kron(np.eye(BRANCH // HEAD_DIM), np.ones((HEAD_DIM, HEAD_DIM))), F32).astype(BF16)
    w0 = jnp.asarray(_channel_dft_matrix(), F32).astype(BF16)
    tile2 = lambda v: jnp.tile(v, LANE // HEAD_DIM * (BRANCH // LANE)).reshape(1, BRANCH)

    for l in range(depth):
        last = l == depth - 1
        lam_init = 0.8 - 0.6 * math.exp(-0.3 * l)
        mx = [m.reshape(B, 1, D) for m in jnp.split(mods[l, :B], 6, axis=-1)]
        mc = [jnp.broadcast_to(m.reshape(1, 1, D), (B, 1, D)) for m in jnp.split(mods[l, B], 6, axis=-1)]
        lw = dict(conv_w=conv_w[l], conv_b=conv_b[l].reshape(1, BRANCH),
                  conv_ln_g=conv_ln_g[l].reshape(1, BRANCH), conv_ln_b=conv_ln_b[l].reshape(1, BRANCH),
                  q_norm=tile2(q_norm[l]), k_norm=tile2(k_norm[l]))
        w_in_l = _permute_w_in(w_in[l])
        wf, wc, wa, wd, wo = (w.astype(BF16) for w in (w_four[l], w_conv[l], w_gqa[l], w_diff[l], w_out[l]))
        w1, w3, w2 = (w.astype(BF16) for w in (w_ffn1[l], w_ffn3[l], w_ffn2[l]))
        gain_m, gain_f = norm_mix[l].reshape(1, D), norm_ffn[l].reshape(1, D)
        lam_vecs = [v[l].reshape(1, HEAD_DIM) for v in (lam_q1, lam_k1, lam_q2, lam_k2)]
        dn = diff_norm[l].reshape(1, LANE)
        fin = final_norm.reshape(1, D)

        zx = _inproj(x, gain_m, mx[0], mx[1], w_in_l)
        zc = _inproj(ctx, gain_m, mc[0], mc[1], w_in_l)
        fx, cvx, qgx, kgx, vgx, qdx, kdx, vdx = _mixer_branches(zx, lw, rope_x, ones_bd, w0, True)
        fc, cvc, qgc, kgc, vgc, qdc, kdc, vdc = _mixer_branches(zc, lw, rope_c, ones_bd, w0, False)

        cat = lambda a, b, axis: jnp.concatenate([a, b], axis=axis)
        ogx = _attention(qgx, cat(kgx, kgc, 1), cat(vgx, vgc, 2), "gqa")
        odx = _attention(qdx, cat(kdx, kdc, 1), cat(vdx, vdc, 2), "diff", lam_vecs, dn, lam_init)
        x = _merge(x, mx[2], fx, cvx, ogx, odx, zx, wf, wc, wa, wd, wo)
        if not last:
            ogc = _attention(qgc, kgc, vgc, "gqa")
            odc = _attention(qdc, kdc, vdc, "diff", lam_vecs, dn, lam_init)
            ctx = _merge(ctx, mc[2], fc, cvc, ogc, odc, zc, wf, wc, wa, wd, wo)
            ctx = _ffn(ctx, gain_f, mc[3], mc[4], mc[5], w1, w3, w2, fin, False)
        x = _ffn(x, gain_f, mx[3], mx[4], mx[5], w1, w3, w2, fin, last)
    return x
```

```python
import functools
import math

import numpy as np
import jax
import jax.numpy as jnp
from jax import lax
from jax.experimental import pallas as pl
from jax.experimental.pallas import tpu as pltpu

F32 = jnp.float32
BF16 = jnp.bfloat16

HEAD_DIM = 64
BRANCH = 512
GQA_GROUP = 4
GQA_KV_HEADS = 2
GQA_VT_ROWS = 80
DIFF_HEADS = 4
N_BRANCH = 4
GRID_W = 64
CONV_K = 31
ROPE_BASE = 10000.0
EPS = 1e-6
LN_EPS = 1e-5

LANE = 128
SUBLANE = 8
BF16_SUBLANES = 16
VMEM_LIMIT = 56 * 1024 * 1024

ZC = 8192
BLK_FOUR, BLK_CA, BLK_CG, BLK_GQ, BLK_DQ, BLK_DK, BLK_DV, BLK_KV = range(8)
GATE_COL0 = 4096
FOURIER_NB = 16
CONV_HALO = 16


def _params(*sem):
    return pltpu.CompilerParams(dimension_semantics=sem, vmem_limit_bytes=VMEM_LIMIT)


def _silu(x):
    return x * jax.nn.sigmoid(x)


def _rms_mod(x, g, shift, scale):
    y = x * lax.rsqrt(jnp.mean(x * x, axis=-1, keepdims=True) + EPS) * g
    return y * (1.0 + scale) + shift


def _ada_kernel(c_ref, w_ref, b_ref, o_ref):
    a = _silu(c_ref[...])
    o_ref[...] = jnp.dot(a, w_ref[...], precision=lax.Precision.HIGHEST,
                         preferred_element_type=F32) + b_ref[...]


def _ada_mod(cond, w_ada, b_ada):
    L, D, N = w_ada.shape
    tn = 1536
    return pl.pallas_call(
        _ada_kernel,
        out_shape=jax.ShapeDtypeStruct((L, 8, N), F32),
        grid=(L, N // tn),
        in_specs=[pl.BlockSpec((8, D), lambda l, j: (0, 0)),
                  pl.BlockSpec((None, D, tn), lambda l, j: (l, 0, j)),
                  pl.BlockSpec((None, 1, tn), lambda l, j: (l, 0, j))],
        out_specs=pl.BlockSpec((None, 8, tn), lambda l, j: (l, 0, j)),
        compiler_params=_params("parallel", "parallel"),
        name="ada_mod",
    )(cond, w_ada, b_ada.reshape(L, 1, N))


def _inproj_kernel(x_ref, g_ref, sh_ref, sc_ref, w_ref, o_ref):
    h = _rms_mod(x_ref[...], g_ref[...], sh_ref[...], sc_ref[...])
    o_ref[...] = jnp.dot(h.astype(BF16), w_ref[...], preferred_element_type=F32).astype(o_ref.dtype)


def _inproj(x, gain, shift, scale, w):
    B, S, D = x.shape
    N = w.shape[1]
    tm = min(512, S)
    tn = 4096
    return pl.pallas_call(
        _inproj_kernel,
        out_shape=jax.ShapeDtypeStruct((B, S, N), BF16),
        grid=(N // tn, B, S // tm),
        in_specs=[pl.BlockSpec((None, tm, D), lambda j, b, i: (b, i, 0)),
                  pl.BlockSpec((1, D), lambda j, b, i: (0, 0)),
                  pl.BlockSpec((None, 1, D), lambda j, b, i: (b, 0, 0)),
                  pl.BlockSpec((None, 1, D), lambda j, b, i: (b, 0, 0)),
                  pl.BlockSpec((D, tn), lambda j, b, i: (0, j))],
        out_specs=pl.BlockSpec((None, tm, tn), lambda j, b, i: (b, i, j)),
        compiler_params=_params("parallel", "parallel", "parallel"),
        name="in_proj",
    )(x, gain, shift, scale, w)


def _ffn_kernel(x_ref, g_ref, sh_ref, sc_ref, gate_ref, w1_ref, w3_ref, w2_ref, fg_ref, o_ref, *, final):
    x = x_ref[...]
    h = _rms_mod(x, g_ref[...], sh_ref[...], sc_ref[...]).astype(BF16)
    a = jnp.dot(h, w1_ref[...], preferred_element_type=F32)
    b = jnp.dot(h, w3_ref[...], preferred_element_type=F32)
    u = (_silu(a) * b).astype(BF16)
    y = x + gate_ref[...] * jnp.dot(u, w2_ref[...], preferred_element_type=F32)
    if final:
        y = y * lax.rsqrt(jnp.mean(y * y, axis=-1, keepdims=True) + EPS) * fg_ref[...]
    o_ref[...] = y


def _ffn(x, gain, shift, scale, gate, w1, w3, w2, final_gain, final):
    B, S, D = x.shape
    H = w1.shape[1]
    tm = min(256, S)
    const = lambda shape: pl.BlockSpec(shape, lambda b, i: (0,) * len(shape), pipeline_mode=pl.Buffered(1))
    per_b = pl.BlockSpec((None, 1, D), lambda b, i: (b, 0, 0))
    return pl.pallas_call(
        functools.partial(_ffn_kernel, final=final),
        out_shape=jax.ShapeDtypeStruct((B, S, D), F32),
        grid=(B, S // tm),
        in_specs=[pl.BlockSpec((None, tm, D), lambda b, i: (b, i, 0)),
                  const((1, D)), per_b, per_b, per_b,
                  const((D, H)), const((D, H)), const((H, D)), const((1, D))],
        out_specs=pl.BlockSpec((None, tm, D), lambda b, i: (b, i, 0)),
        compiler_params=_params("parallel", "parallel"),
        name="ffn",
    )(x, gain, shift, scale, gate, w1, w3, w2, final_gain)


def _merge_kernel(x_ref, gm_ref, f_ref, c_ref, a_ref, d_ref, zg_ref,
                  wf_ref, wc_ref, wa_ref, wd_ref, wo_ref, o_ref):
    D = x_ref.shape[-1]
    acc = None
    for b, (br, w) in enumerate(((f_ref, wf_ref), (c_ref, wc_ref), (a_ref, wa_ref), (d_ref, wd_ref))):
        y = jnp.dot(br[...], w[...], preferred_element_type=F32)
        gate = jax.nn.sigmoid(zg_ref[:, b * D:(b + 1) * D].astype(F32))
        acc = gate * y if acc is None else acc + gate * y
    out = jnp.dot(acc.astype(BF16), wo_ref[...], preferred_element_type=F32)
    o_ref[...] = x_ref[...] + gm_ref[...] * out


def _merge(x, gm, f, cv, og, od, z, wf, wc, wa, wd, wo):
    B, S, D = x.shape
    tm = min(512, S)
    row = lambda width: pl.BlockSpec((None, tm, width), lambda b, i: (b, i, 0))
    const = lambda shape: pl.BlockSpec(shape, lambda b, i: (0, 0))
    return pl.pallas_call(
        _merge_kernel,
        out_shape=jax.ShapeDtypeStruct((B, S, D), F32),
        grid=(B, S // tm),
        in_specs=[row(D), pl.BlockSpec((None, 1, D), lambda b, i: (b, 0, 0)),
                  row(BRANCH), row(BRANCH), row(BRANCH), row(BRANCH),
                  pl.BlockSpec((None, tm, N_BRANCH * D), lambda b, i: (b, i, GATE_COL0 // (N_BRANCH * D))),
                  const((BRANCH, D)), const((BRANCH, D)), const((BRANCH, D)), const((BRANCH, D)),
                  const((D, D))],
        out_specs=row(D),
        compiler_params=_params("parallel", "parallel"),
        name="merge",
    )(x, gm, f, cv, og, od, z, wf, wc, wa, wd, wo)


def _conv_kernel(a_ref, g_ref, ap_ref, gp_ref, an_ref, gn_ref, w_ref, b_ref, lg_ref, lb_ref, o_ref, u_s,
                 *, ts, chunk):
    i = pl.program_id(1)
    n = pl.num_programs(1)

    def glu(a, g):
        return a.astype(F32) * jax.nn.sigmoid(g.astype(F32))

    u_s[CONV_HALO:CONV_HALO + ts, :] = glu(a_ref[...], g_ref[...])
    u_s[0:CONV_HALO, :] = jnp.where(i > 0, glu(ap_ref[...], gp_ref[...]), 0.0)
    u_s[CONV_HALO + ts:2 * CONV_HALO + ts, :] = jnp.where(i < n - 1, glu(an_ref[...], gn_ref[...]), 0.0)
    pad = CONV_K // 2

    def body(c, carry):
        r0 = pl.multiple_of(c * chunk, chunk)
        win = u_s[pl.ds(r0, chunk + 2 * CONV_HALO), :]
        rows = chunk + 2 * CONV_HALO
        acc = jnp.zeros((chunk // SUBLANE, SUBLANE, BRANCH), F32) + b_ref[...]
        for r in range(SUBLANE):
            taps = [j for j in range(CONV_K) if (CONV_HALO - pad + j) % SUBLANE == r]
            shifted = win if r == 0 else pltpu.roll(win, rows - r, 0)
            for j in taps:
                base = CONV_HALO - pad + j - r
                tap = shifted[base:base + chunk, :].reshape(chunk // SUBLANE, SUBLANE, BRANCH)
                acc = acc + w_ref[j] * tap
        acc = acc.reshape(chunk, BRANCH)
        mu = jnp.mean(acc, axis=-1, keepdims=True)
        d = acc - mu
        var = jnp.mean(d * d, axis=-1, keepdims=True)
        y = d * lax.rsqrt(var + LN_EPS) * lg_ref[...] + lb_ref[...]
        o_ref[pl.ds(r0, chunk), :] = _silu(y).astype(o_ref.dtype)
        return carry

    lax.fori_loop(0, ts // chunk, body, 0)


def _conv_branch(z, w, b, ln_g, ln_b):
    B, S, _ = z.shape
    ts = min(512, S)
    chunk = min(256, ts)
    hb = ts // CONV_HALO
    last = S // CONV_HALO - 1
    cur = lambda blk: pl.BlockSpec((None, ts, BRANCH), lambda b_, i: (b_, i, blk))
    prev = lambda blk: pl.BlockSpec((None, CONV_HALO, BRANCH),
                                    lambda b_, i: (b_, jnp.maximum(i * hb - 1, 0), blk))
    nxt = lambda blk: pl.BlockSpec((None, CONV_HALO, BRANCH),
                                   lambda b_, i: (b_, jnp.minimum((i + 1) * hb, last), blk))
    const = lambda shape: pl.BlockSpec(shape, lambda b_, i: (0, 0))
    return pl.pallas_call(
        functools.partial(_conv_kernel, ts=ts, chunk=chunk),
        out_shape=jax.ShapeDtypeStruct((B, S, BRANCH), BF16),
        grid=(B, S // ts),
        in_specs=[cur(BLK_CA), cur(BLK_CG), prev(BLK_CA), prev(BLK_CG), nxt(BLK_CA), nxt(BLK_CG),
                  pl.BlockSpec((CONV_K, SUBLANE, BRANCH), lambda b_, i: (0, 0, 0)),
                  const((1, BRANCH)), const((1, BRANCH)), const((1, BRANCH))],
        out_specs=pl.BlockSpec((None, ts, BRANCH), lambda b_, i: (b_, i, 0)),
        scratch_shapes=[pltpu.VMEM((ts + 2 * CONV_HALO, BRANCH), F32)],
        compiler_params=_params("parallel", "parallel"),
        name="conv_branch",
    )(z, z, z, z, z, z, jnp.broadcast_to(w[:, None, :], (CONV_K, SUBLANE, BRANCH)), b, ln_g, ln_b)


def _head_rms(x, ones_bd):
    x2 = x * x
    hi = x2.astype(BF16)
    lo = (x2 - hi.astype(F32)).astype(BF16)
    ssum = (jnp.dot(hi, ones_bd, preferred_element_type=F32)
            + jnp.dot(lo, ones_bd, preferred_element_type=F32))
    return x * lax.rsqrt(ssum * (1.0 / HEAD_DIM) + EPS)


def _rope(x, cosf, sins):
    width = x.shape[1]
    reps = width // LANE
    c = jnp.concatenate([cosf] * reps, axis=1) if reps > 1 else cosf
    s = jnp.concatenate([sins] * reps, axis=1) if reps > 1 else sins
    lane = lax.broadcasted_iota(jnp.int32, x.shape, 1)
    first_half = (lane & (HEAD_DIM - 1)) < HEAD_DIM // 2
    partner = jnp.where(first_half, pltpu.roll(x, width - HEAD_DIM // 2, 1), pltpu.roll(x, HEAD_DIM // 2, 1))
    return x * c + partner * s


def _prep_kernel(gq_ref, dq_ref, dk_ref, dv_ref, kv_ref, cos_ref, sin_ref, qn_ref, kn_ref, bd_ref, *rest):
    qg_ref, kg_ref, vg_ref, qd_ref, kd_ref, vd_ref = rest[-6:]
    cosf, sins = cos_ref[...], sin_ref[...]
    bd = bd_ref[...]
    scale = HEAD_DIM ** -0.5 * math.log2(math.e)
    ts = gq_ref.shape[0]

    half = jnp.zeros((HEAD_DIM, ts), qg_ref.dtype)
    q = _rope(_head_rms(gq_ref[...].astype(F32), bd) * qn_ref[...], cosf, sins) * scale
    for p in range(BRANCH // LANE):
        qt = q[:, p * LANE:(p + 1) * LANE].T
        for e in range(2):
            h = 2 * p + e
            kvh = h // GQA_GROUP
            base = h * LANE
            qg_ref[base + kvh * HEAD_DIM:base + (kvh + 1) * HEAD_DIM, :] = (
                qt[e * HEAD_DIM:(e + 1) * HEAD_DIM, :].astype(qg_ref.dtype))
            qg_ref[base + (1 - kvh) * HEAD_DIM:base + (2 - kvh) * HEAD_DIM, :] = half

    kvz = kv_ref[...].astype(F32)
    k = _rope(_head_rms(kvz[:, :LANE], bd[:LANE, :LANE]) * kn_ref[:, :LANE], cosf, sins)
    kg_ref[...] = k.astype(kg_ref.dtype)
    vt = kvz[:, LANE:2 * LANE].T
    row = lax.broadcasted_iota(jnp.int32, (GQA_VT_ROWS - HEAD_DIM, ts), 0)
    tail = jnp.where(row == 0, 1.0, 0.0).astype(vg_ref.dtype)
    for g in range(GQA_KV_HEADS):
        vg_ref[g * GQA_VT_ROWS:g * GQA_VT_ROWS + HEAD_DIM, :] = vt[g * HEAD_DIM:(g + 1) * HEAD_DIM, :].astype(vg_ref.dtype)
        vg_ref[g * GQA_VT_ROWS + HEAD_DIM:(g + 1) * GQA_VT_ROWS, :] = tail

    qd = _rope(dq_ref[...].astype(F32), cosf, sins) * scale
    for h in range(DIFF_HEADS):
        qt = qd[:, h * LANE:(h + 1) * LANE].T.astype(qd_ref.dtype)
        base = 2 * h * LANE
        qd_ref[base:base + HEAD_DIM, :] = qt[:HEAD_DIM, :]
        qd_ref[base + HEAD_DIM:base + LANE, :] = half
        qd_ref[base + LANE:base + LANE + HEAD_DIM, :] = half
        qd_ref[base + LANE + HEAD_DIM:base + 2 * LANE, :] = qt[HEAD_DIM:, :]
        vd_ref[h * LANE:(h + 1) * LANE, :] = dv_ref[:, h * LANE:(h + 1) * LANE].astype(F32).T.astype(vd_ref.dtype)
    kd_ref[...] = _rope(dk_ref[...].astype(F32), cosf, sins).astype(kd_ref.dtype)


def _prep(z, cosf, sins, qn, kn, ones_bd, n_keys, key_start, kv_bufs=None):
    B, S, _ = z.shape
    ts = min(256, S)
    assert key_start % ts == 0
    k0 = key_start // ts
    slab = lambda blk: pl.BlockSpec((None, ts, BRANCH), lambda b, i: (b, i, blk))
    tab = pl.BlockSpec((ts, LANE), lambda b, i: (i, 0))
    const = lambda shape: pl.BlockSpec(shape, lambda b, i: (0, 0))
    q_cols = lambda chans: pl.BlockSpec((None, chans, ts), lambda b, i: (b, 0, i))
    k_rows = lambda width: pl.BlockSpec((None, ts, width), lambda b, i: (b, k0 + i, 0))
    v_cols = lambda chans: pl.BlockSpec((None, chans, ts), lambda b, i: (b, 0, k0 + i))
    vg_rows = GQA_KV_HEADS * GQA_VT_ROWS
    shapes = [(B, 2 * BRANCH, S), (B, n_keys, LANE), (B, vg_rows, n_keys),
              (B, 2 * BRANCH, S), (B, n_keys, BRANCH), (B, BRANCH, n_keys)]
    specs = [q_cols(2 * BRANCH), k_rows(LANE), v_cols(vg_rows), q_cols(2 * BRANCH), k_rows(BRANCH), v_cols(BRANCH)]
    in_specs = [slab(BLK_GQ), slab(BLK_DQ), slab(BLK_DK), slab(BLK_DV), slab(BLK_KV), tab, tab,
                const((1, BRANCH)), const((1, BRANCH)), const((BRANCH, BRANCH))]
    args = [z, z, z, z, z, cosf, sins, qn, kn, ones_bd]
    aliases = {}
    if kv_bufs is not None:
        for out_idx, buf in zip((1, 2, 4, 5), kv_bufs):
            aliases[len(args)] = out_idx
            in_specs.append(pl.BlockSpec(memory_space=pl.ANY))
            args.append(buf)
    return pl.pallas_call(
        _prep_kernel,
        out_shape=[jax.ShapeDtypeStruct(s, BF16) for s in shapes],
        grid=(B, S // ts),
        in_specs=in_specs,
        out_specs=specs,
        input_output_aliases=aliases,
        compiler_params=_params("parallel", "parallel"),
        name="qk_prep",
    )(*args)


def _attn_kernel(*refs, sets, tq, tk, mode, lam_init):
    if mode == "diff":
        q_ref, k_ref, v_ref, l1q, l1k, l2q, l2k, dn_ref, o_ref, q_s, acc_s = refs
    else:
        q_ref, k_ref, v_ref, o_ref, q_s, acc_s = refs
    n_tiles = k_ref.shape[0] // tk
    width = sets * tq
    for r in range(sets):
        q_s[:, r * tq:(r + 1) * tq] = q_ref[r * LANE:(r + 1) * LANE, :]

    def finish(l_run):
        if mode == "diff":
            ot = acc_s[...] / l_run
            o = [ot[:, r * tq:(r + 1) * tq].T for r in range(sets)]
            lam = (jnp.exp(jnp.sum(l1q[...] * l1k[...], axis=1, keepdims=True))
                   - jnp.exp(jnp.sum(l2q[...] * l2k[...], axis=1, keepdims=True)) + lam_init)
            d = o[0] - lam * o[1]
            d = d * lax.rsqrt(jnp.mean(d * d, axis=1, keepdims=True) + EPS) * dn_ref[...]
            o_ref[...] = (d * (1.0 - lam_init)).astype(o_ref.dtype)
        else:
            ot = acc_s[:HEAD_DIM, :] / acc_s[HEAD_DIM:HEAD_DIM + 1, :]
            for pair in range(sets // 2):
                both = jnp.concatenate([ot[:, (2 * pair) * tq:(2 * pair + 1) * tq],
                                        ot[:, (2 * pair + 1) * tq:(2 * pair + 2) * tq]], axis=0)
                o_ref[:, pair * LANE:(pair + 1) * LANE] = both.T.astype(o_ref.dtype)

    m_run = l_run = None
    for t in range(n_tiles):
        st = jnp.dot(k_ref[t * tk:(t + 1) * tk, :], q_s[...], preferred_element_type=F32)
        m_tile = jnp.max(st, axis=0, keepdims=True)
        pt = jnp.exp2(st - (m_tile if t == 0 else m_run))
        pv = jnp.dot(v_ref[:, t * tk:(t + 1) * tk], pt.astype(BF16), preferred_element_type=F32)
        l_tile = jnp.sum(pt, axis=0, keepdims=True) if mode == "diff" else None
        if t == 0:
            m_run, l_run = m_tile, l_tile
            acc_s[...] = pv
        else:
            m_new = jnp.maximum(m_run, m_tile)
            alpha = jnp.exp2(m_run - m_new)
            if mode == "diff":
                l_run = (l_run + l_tile) * alpha
            acc_s[...] = (acc_s[...] + pv) * alpha
            m_run = m_new
    finish(l_run)

    finite = jnp.isfinite(acc_s[...])
    if mode == "diff":
        finite = jnp.logical_and(finite, jnp.isfinite(l_run))
    overflowed = jnp.max(jnp.where(finite, 0.0, 1.0)) > 0.0

    @pl.when(overflowed)
    def _exact():
        def body(t, carry):
            m_old, l_old = carry
            st = jnp.dot(k_ref[pl.ds(pl.multiple_of(t * tk, tk), tk), :], q_s[...], preferred_element_type=F32)
            m_new = jnp.maximum(m_old, jnp.max(st, axis=0, keepdims=True))
            alpha = jnp.exp2(m_old - m_new)
            pt = jnp.exp2(st - m_new)
            v_tile = v_ref[:, pl.ds(pl.multiple_of(t * tk, LANE), tk)]
            acc_s[...] = alpha * acc_s[...] + jnp.dot(v_tile, pt.astype(BF16), preferred_element_type=F32)
            return m_new, alpha * l_old + jnp.sum(pt, axis=0, keepdims=True)

        acc_s[...] = jnp.zeros(acc_s.shape, F32)
        init = (jnp.full((1, width), -jnp.inf, F32), jnp.zeros((1, width), F32))
        _, l_exact = lax.fori_loop(0, n_tiles, body, init)
        finish(l_exact)


def _attention(qt, k, vt, mode, key_start, NK, lam_vecs=None, diff_norm=None, lam_init=0.0):
    B, _, S = qt.shape
    assert key_start % NK == 0
    kb = key_start // NK
    tq = min(512 if mode == "diff" else 256, S)
    tk = 768
    tk = tk if NK % tk == 0 else 256
    if mode == "diff":
        groups, sets, out_w, vrows = DIFF_HEADS, 2, LANE, LANE
        kmap = lambda b, g, i: (b, kb, g)
    else:
        groups, sets, out_w, vrows = GQA_KV_HEADS, GQA_GROUP, GQA_GROUP * HEAD_DIM, GQA_VT_ROWS
        kmap = lambda b, g, i: (b, kb, 0)
    in_specs = [pl.BlockSpec((None, sets * LANE, tq), lambda b, g, i: (b, g, i)),
                pl.BlockSpec((None, NK, LANE), kmap),
                pl.BlockSpec((None, vrows, NK), lambda b, g, i: (b, g, kb))]
    args = [qt, k, vt]
    if mode == "diff":
        in_specs += [pl.BlockSpec((1, HEAD_DIM), lambda b, g, i: (0, 0))] * 4
        in_specs += [pl.BlockSpec((1, LANE), lambda b, g, i: (0, 0))]
        args += list(lam_vecs) + [diff_norm]
    return pl.pallas_call(
        functools.partial(_attn_kernel, sets=sets, tq=tq, tk=tk, mode=mode, lam_init=lam_init),
        out_shape=jax.ShapeDtypeStruct((B, S, BRANCH), BF16),
        grid=(B, groups, S // tq),
        in_specs=in_specs,
        out_specs=pl.BlockSpec((None, tq, out_w), lambda b, g, i: (b, i, g)),
        scratch_shapes=[pltpu.VMEM((LANE, sets * tq), BF16),
                        pltpu.VMEM((vrows, sets * tq), F32)],
        compiler_params=_params("parallel", "parallel", "parallel"),
        name="attn_" + mode,
    )(*args)


def _dft_cos_sin(n):
    idx = np.arange(n)
    ang = 2.0 * np.pi * ((idx[:, None] * idx[None, :]) % n) / n
    return np.cos(ang), np.sin(ang)


def _channel_dft_matrix():
    c, s = _dft_cos_sin(GRID_W)
    groups = BRANCH // GRID_W
    eye = np.eye(groups)
    return np.concatenate([np.kron(eye, c), -np.kron(eye, s)], axis=1) / math.sqrt(GRID_W)


def _fourier_a_kernel(z_ref, w0_ref, d1_ref, tc_ref, ts_ref, o_ref):
    n1, nb, _ = z_ref.shape
    u = jnp.dot(z_ref[...].reshape(n1 * nb, BRANCH), w0_ref[...], preferred_element_type=F32)
    u = pltpu.einshape("kjm->jkm", u.reshape(n1, nb, 2 * BRANCH)).astype(BF16)
    reps = BRANCH // LANE
    out_r, out_i = [], []
    for j in range(nb):
        p = jnp.dot(d1_ref[...], u[j], preferred_element_type=F32)
        vr = p[:n1, :BRANCH] + p[n1:, BRANCH:]
        vi = p[:n1, BRANCH:] - p[n1:, :BRANCH]
        c = jnp.concatenate([tc_ref[j]] * reps, axis=1)
        s = jnp.concatenate([ts_ref[j]] * reps, axis=1)
        out_r.append(vr * c + vi * s)
        out_i.append(vi * c - vr * s)
    o_ref[0] = pltpu.einshape("jkm->kjm", jnp.stack(out_r)).astype(o_ref.dtype)
    o_ref[1] = pltpu.einshape("jkm->kjm", jnp.stack(out_i)).astype(o_ref.dtype)


def _fourier_b_kernel(v_ref, c3_ref, s3_ref, o_ref):
    nb = v_ref.shape[1]
    outs = [jnp.dot(c3_ref[...], v_ref[0, j], preferred_element_type=F32)
            + jnp.dot(s3_ref[...], v_ref[1, j], preferred_element_type=F32) for j in range(nb)]
    o_ref[...] = pltpu.einshape("jkm->kjm", jnp.stack(outs)).astype(o_ref.dtype)


def _bmm_kernel(a_ref, x_ref, o_ref):
    o_ref[...] = jnp.dot(a_ref[...], x_ref[...], preferred_element_type=F32).astype(o_ref.dtype)


def _bmm(a, x, tn):
    M, K = a.shape
    B, _, N = x.shape
    tn = min(tn, N)
    return pl.pallas_call(
        _bmm_kernel,
        out_shape=jax.ShapeDtypeStruct((B, M, N), BF16),
        grid=(B, N // tn),
        in_specs=[pl.BlockSpec((M, K), lambda b, j: (0, 0)),
                  pl.BlockSpec((None, K, tn), lambda b, j: (b, 0, j))],
        out_specs=pl.BlockSpec((None, M, tn), lambda b, j: (b, 0, j)),
        compiler_params=_params("parallel", "parallel"),
        name="bmm",
    )(a, x)


def _slab_matmul_kernel(z_ref, w_ref, o_ref):
    o_ref[...] = jnp.dot(z_ref[...], w_ref[...], preferred_element_type=F32).astype(o_ref.dtype)


def _slab_matmul(z, blk, w):
    B, S, _ = z.shape
    K, N = w.shape
    ts = min(256, S)
    return pl.pallas_call(
        _slab_matmul_kernel,
        out_shape=jax.ShapeDtypeStruct((B, S, N), BF16),
        grid=(B, S // ts),
        in_specs=[pl.BlockSpec((None, ts, K), lambda b, i: (b, i, blk)),
                  pl.BlockSpec((K, N), lambda b, i: (0, 0))],
        out_specs=pl.BlockSpec((None, ts, N), lambda b, i: (b, i, 0)),
        compiler_params=_params("parallel", "parallel"),
        name="slab_matmul",
    )(z, w)


def _fourier_long(z, w0):
    B, S, _ = z.shape
    n2 = GRID_W
    n1 = S // n2
    nb = FOURIER_NB
    assert n1 % nb == 0 and n2 % nb == 0
    c1, s1 = _dft_cos_sin(n1)
    d1 = jnp.asarray(np.concatenate([c1, s1], axis=0) / math.sqrt(n1), F32).astype(BF16)
    ang = 2.0 * np.pi * (np.arange(n2)[:, None] * np.arange(n1)[None, :]) / S
    twc = jnp.asarray(np.repeat(np.cos(ang)[:, :, None], LANE, axis=2), F32)
    tws = jnp.asarray(np.repeat(np.sin(ang)[:, :, None], LANE, axis=2), F32)
    v = pl.pallas_call(
        _fourier_a_kernel,
        out_shape=jax.ShapeDtypeStruct((B, 2, n1, n2, BRANCH), BF16),
        grid=(B, n2 // nb),
        in_specs=[pl.BlockSpec((None, n1, nb, BRANCH), lambda b, j: (b, 0, j, BLK_FOUR)),
                  pl.BlockSpec((BRANCH, 2 * BRANCH), lambda b, j: (0, 0)),
                  pl.BlockSpec((2 * n1, n1), lambda b, j: (0, 0)),
                  pl.BlockSpec((nb, n1, LANE), lambda b, j: (j, 0, 0)),
                  pl.BlockSpec((nb, n1, LANE), lambda b, j: (j, 0, 0))],
        out_specs=pl.BlockSpec((None, 2, n1, nb, BRANCH), lambda b, j: (b, 0, 0, j, 0)),
        compiler_params=_params("parallel", "parallel"),
        name="fourier_a",
    )(z.reshape(B, n1, n2, ZC), w0, d1, twc, tws)
    c3, s3 = _dft_cos_sin(n2)
    c3 = jnp.asarray(c3 / math.sqrt(n2), F32).astype(BF16)
    s3 = jnp.asarray(s3 / math.sqrt(n2), F32).astype(BF16)
    f = pl.pallas_call(
        _fourier_b_kernel,
        out_shape=jax.ShapeDtypeStruct((B, n2, n1, BRANCH), BF16),
        grid=(B, n1 // nb),
        in_specs=[pl.BlockSpec((None, 2, nb, n2, BRANCH), lambda b, i: (b, 0, i, 0, 0)),
                  pl.BlockSpec((n2, n2), lambda b, i: (0, 0)),
                  pl.BlockSpec((n2, n2), lambda b, i: (0, 0))],
        out_specs=pl.BlockSpec((None, n2, nb, BRANCH), lambda b, i: (b, 0, i, 0)),
        compiler_params=_params("parallel", "parallel"),
        name="fourier_b",
    )(v, c3, s3)
    return f.reshape(B, S, BRANCH)


def _fourier_short(z, w0):
    B, S, _ = z.shape
    u = _slab_matmul(z, BLK_FOUR, w0)
    ust = jnp.concatenate([u[:, :, :BRANCH], u[:, :, BRANCH:]], axis=1)
    c, s = _dft_cos_sin(S)
    dl = jnp.asarray(np.concatenate([c, s], axis=1) / math.sqrt(S), F32).astype(BF16)
    return _bmm(dl, ust, BRANCH)


def _rope_tables(n_rows):
    half = HEAD_DIM // 4
    inv = ROPE_BASE ** (-np.arange(0, HEAD_DIM // 2, 2, dtype=np.float64) / (HEAD_DIM // 2))
    row = np.repeat(np.arange(n_rows, dtype=np.float64), GRID_W)
    col = np.tile(np.arange(GRID_W, dtype=np.float64), n_rows)
    ang = np.concatenate([row[:, None] * inv, col[:, None] * inv], axis=-1)
    assert ang.shape[1] == 2 * half
    cos, sin = np.cos(ang), np.sin(ang)
    cosf = np.tile(np.concatenate([cos, cos], axis=1), (1, LANE // HEAD_DIM))
    sins = np.tile(np.concatenate([-sin, sin], axis=1), (1, LANE // HEAD_DIM))
    return jnp.asarray(cosf, F32), jnp.asarray(sins, F32)


def _permute_w_in(w):
    D = w.shape[0]
    four, conv, gq, gk, gv, dq, dk, dv, gates = jnp.split(
        w, np.cumsum([BRANCH, 2 * BRANCH, BRANCH, LANE, LANE, BRANCH, BRANCH, BRANCH])[:], axis=1)
    pad = jnp.zeros((D, BRANCH - 2 * LANE), w.dtype)
    out = jnp.concatenate([four, conv, gq, dq, dk, dv, gk, gv, pad, gates], axis=1)
    assert out.shape[1] == ZC
    return out.astype(BF16)


def _mixer_branches(z, lw, w0, long_seq):
    f = _fourier_long(z, w0) if long_seq else _fourier_short(z, w0)
    cv = _conv_branch(z, lw["conv_w"], lw["conv_b"], lw["conv_ln_g"], lw["conv_ln_b"])
    return f, cv


def kernel(x, c, ctx, c_ctx, w_ada, b_ada, norm_mix, w_in, w_four, conv_w, conv_b, conv_ln_g, conv_ln_b, w_conv, q_norm, k_norm, w_gqa, lam_q1, lam_k1, lam_q2, lam_k2, diff_norm, w_diff, w_out, norm_ffn, w_ffn1, w_ffn3, w_ffn2, final_norm):
    B, S, D = x.shape
    Sc = ctx.shape[1]
    depth = w_ada.shape[0]
    assert B + 1 <= 8 and S % GRID_W == 0 and D == 2 * BRANCH

    cond = jnp.zeros((8, D), F32).at[:B].set(c).at[B].set(c_ctx)
    mods = _ada_mod(cond, w_ada, b_ada)

    rope_x = _rope_tables(S // GRID_W)
    rope_c = (jnp.ones((Sc, LANE), F32), jnp.zeros((Sc, LANE), F32))
    ones_bd = jnp.asarray(np.kron(np.eye(BRANCH // HEAD_DIM), np.ones((HEAD_DIM, HEAD_DIM))), F32).astype(BF16)
    w0 = jnp.asarray(_channel_dft_matrix(), F32).astype(BF16)
    tile2 = lambda v: jnp.tile(v, LANE // HEAD_DIM * (BRANCH // LANE)).reshape(1, BRANCH)

    for l in range(depth):
        last = l == depth - 1
        lam_init = 0.8 - 0.6 * math.exp(-0.3 * l)
        mx = [m.reshape(B, 1, D) for m in jnp.split(mods[l, :B], 6, axis=-1)]
        mc = [jnp.broadcast_to(m.reshape(1, 1, D), (B, 1, D)) for m in jnp.split(mods[l, B], 6, axis=-1)]
        lw = dict(conv_w=conv_w[l], conv_b=conv_b[l].reshape(1, BRANCH),
                  conv_ln_g=conv_ln_g[l].reshape(1, BRANCH), conv_ln_b=conv_ln_b[l].reshape(1, BRANCH),
                  q_norm=tile2(q_norm[l]), k_norm=tile2(k_norm[l]))
        w_in_l = _permute_w_in(w_in[l])
        wf, wc, wa, wd, wo = (w.astype(BF16) for w in (w_four[l], w_conv[l], w_gqa[l], w_diff[l], w_out[l]))
        w1, w3, w2 = (w.astype(BF16) for w in (w_ffn1[l], w_ffn3[l], w_ffn2[l]))
        gain_m, gain_f = norm_mix[l].reshape(1, D), norm_ffn[l].reshape(1, D)
        lam_vecs = [v[l].reshape(1, HEAD_DIM) for v in (lam_q1, lam_k1, lam_q2, lam_k2)]
        dn = diff_norm[l].reshape(1, LANE)
        fin = final_norm.reshape(1, D)

        zx = _inproj(x, gain_m, mx[0], mx[1], w_in_l)
        zc = _inproj(ctx, gain_m, mc[0], mc[1], w_in_l)
        fx, cvx = _mixer_branches(zx, lw, w0, True)
        fc, cvc = _mixer_branches(zc, lw, w0, False)
        qgx, kg, vg, qdx, kd, vd = _prep(zx, rope_x[0], rope_x[1], lw["q_norm"], lw["k_norm"], ones_bd, S + Sc, 0)
        qgc, kg, vg, qdc, kd, vd = _prep(zc, rope_c[0], rope_c[1], lw["q_norm"], lw["k_norm"], ones_bd, S + Sc, S,
                                         kv_bufs=(kg, vg, kd, vd))

        ogx = _attention(qgx, kg, vg, "gqa", 0, S + Sc)
        odx = _attention(qdx, kd, vd, "diff", 0, S + Sc, lam_vecs, dn, lam_init)
        x = _merge(x, mx[2], fx, cvx, ogx, odx, zx, wf, wc, wa, wd, wo)
        if not last:
            ogc = _attention(qgc, kg, vg, "gqa", S, Sc)
            odc = _attention(qdc, kd, vd, "diff", S, Sc, lam_vecs, dn, lam_init)
            ctx = _merge(ctx, mc[2], fc, cvc, ogc, odc, zc, wf, wc, wa, wd, wo)
            ctx = _ffn(ctx, gain_f, mc[3], mc[4], mc[5], w1, w3, w2, fin, False)
        x = _ffn(x, gain_f, mx[3], mx[4], mx[5], w1, w3, w2, fin, last)
    return x
```

```python
import functools
import math

import numpy as np
import jax
import jax.numpy as jnp
from jax import lax
from jax.experimental import pallas as pl
from jax.experimental.pallas import tpu as pltpu

F32 = jnp.float32
BF16 = jnp.bfloat16

HEAD_DIM = 64
BRANCH = 512
GQA_GROUP = 4
GQA_KV_HEADS = 2
DIFF_VT_ROWS = 144
GQA_VT_ROWS = 80
DIFF_HEADS = 4
N_BRANCH = 4
GRID_W = 64
CONV_K = 31
ROPE_BASE = 10000.0
EPS = 1e-6
LN_EPS = 1e-5

LANE = 128
SUBLANE = 8
BF16_SUBLANES = 16
VMEM_LIMIT = 56 * 1024 * 1024

ZC = 8192
BLK_FOUR, BLK_CA, BLK_CG, BLK_GQ, BLK_DQ, BLK_DK, BLK_DV, BLK_KV = range(8)
GATE_COL0 = 4096
FOURIER_NB = 16
CONV_HALO = 16


def _params(*sem):
    return pltpu.CompilerParams(dimension_semantics=sem, vmem_limit_bytes=VMEM_LIMIT)


def _silu(x):
    return x * jax.nn.sigmoid(x)


def _rms_mod(x, g, shift, scale):
    y = x * lax.rsqrt(jnp.mean(x * x, axis=-1, keepdims=True) + EPS) * g
    return y * (1.0 + scale) + shift


def _ada_kernel(c_ref, w_ref, b_ref, o_ref):
    a = _silu(c_ref[...])
    o_ref[...] = jnp.dot(a, w_ref[...], precision=lax.Precision.HIGHEST,
                         preferred_element_type=F32) + b_ref[...]


def _ada_mod(cond, w_ada, b_ada):
    L, D, N = w_ada.shape
    tn = 1536
    return pl.pallas_call(
        _ada_kernel,
        out_shape=jax.ShapeDtypeStruct((L, 8, N), F32),
        grid=(L, N // tn),
        in_specs=[pl.BlockSpec((8, D), lambda l, j: (0, 0)),
                  pl.BlockSpec((None, D, tn), lambda l, j: (l, 0, j)),
                  pl.BlockSpec((None, 1, tn), lambda l, j: (l, 0, j))],
        out_specs=pl.BlockSpec((None, 8, tn), lambda l, j: (l, 0, j)),
        compiler_params=_params("parallel", "parallel"),
        name="ada_mod",
    )(cond, w_ada, b_ada.reshape(L, 1, N))


def _inproj_kernel(x_ref, g_ref, sh_ref, sc_ref, w_ref, o_ref):
    h = _rms_mod(x_ref[...], g_ref[...], sh_ref[...], sc_ref[...])
    o_ref[...] = jnp.dot(h.astype(BF16), w_ref[...], preferred_element_type=F32).astype(o_ref.dtype)


def _inproj(x, gain, shift, scale, w):
    B, S, D = x.shape
    N = w.shape[1]
    tm = min(512, S)
    tn = 4096
    return pl.pallas_call(
        _inproj_kernel,
        out_shape=jax.ShapeDtypeStruct((B, S, N), BF16),
        grid=(N // tn, B, S // tm),
        in_specs=[pl.BlockSpec((None, tm, D), lambda j, b, i: (b, i, 0)),
                  pl.BlockSpec((1, D), lambda j, b, i: (0, 0)),
                  pl.BlockSpec((None, 1, D), lambda j, b, i: (b, 0, 0)),
                  pl.BlockSpec((None, 1, D), lambda j, b, i: (b, 0, 0)),
                  pl.BlockSpec((D, tn), lambda j, b, i: (0, j))],
        out_specs=pl.BlockSpec((None, tm, tn), lambda j, b, i: (b, i, j)),
        compiler_params=_params("parallel", "parallel", "parallel"),
        name="in_proj",
    )(x, gain, shift, scale, w)


def _ffn_kernel(x_ref, g_ref, sh_ref, sc_ref, gate_ref, w1_ref, w3_ref, w2_ref, fg_ref, o_ref, *, final):
    x = x_ref[...]
    h = _rms_mod(x, g_ref[...], sh_ref[...], sc_ref[...]).astype(BF16)
    a = jnp.dot(h, w1_ref[...], preferred_element_type=F32)
    b = jnp.dot(h, w3_ref[...], preferred_element_type=F32)
    u = (_silu(a) * b).astype(BF16)
    y = x + gate_ref[...] * jnp.dot(u, w2_ref[...], preferred_element_type=F32)
    if final:
        y = y * lax.rsqrt(jnp.mean(y * y, axis=-1, keepdims=True) + EPS) * fg_ref[...]
    o_ref[...] = y


def _ffn(x, gain, shift, scale, gate, w1, w3, w2, final_gain, final):
    B, S, D = x.shape
    H = w1.shape[1]
    tm = min(512, S)
    const = lambda shape: pl.BlockSpec(shape, lambda b, i: (0,) * len(shape), pipeline_mode=pl.Buffered(1))
    per_b = pl.BlockSpec((None, 1, D), lambda b, i: (b, 0, 0))
    return pl.pallas_call(
        functools.partial(_ffn_kernel, final=final),
        out_shape=jax.ShapeDtypeStruct((B, S, D), F32),
        grid=(B, S // tm),
        in_specs=[pl.BlockSpec((None, tm, D), lambda b, i: (b, i, 0)),
                  const((1, D)), per_b, per_b, per_b,
                  const((D, H)), const((D, H)), const((H, D)), const((1, D))],
        out_specs=pl.BlockSpec((None, tm, D), lambda b, i: (b, i, 0)),
        compiler_params=_params("parallel", "parallel"),
        name="ffn",
    )(x, gain, shift, scale, gate, w1, w3, w2, final_gain)


def _merge_kernel(x_ref, gm_ref, f_ref, c_ref, a_ref, d_ref, zg_ref,
                  wf_ref, wc_ref, wa_ref, wd_ref, wo_ref, o_ref):
    D = x_ref.shape[-1]
    acc = None
    for b, (br, w) in enumerate(((f_ref, wf_ref), (c_ref, wc_ref), (a_ref, wa_ref), (d_ref, wd_ref))):
        y = jnp.dot(br[...], w[...], preferred_element_type=F32)
        gate = jax.nn.sigmoid(zg_ref[:, b * D:(b + 1) * D].astype(F32))
        acc = gate * y if acc is None else acc + gate * y
    out = jnp.dot(acc.astype(BF16), wo_ref[...], preferred_element_type=F32)
    o_ref[...] = x_ref[...] + gm_ref[...] * out


def _merge(x, gm, f, cv, og, od, z, wf, wc, wa, wd, wo):
    B, S, D = x.shape
    tm = min(512, S)
    row = lambda width: pl.BlockSpec((None, tm, width), lambda b, i: (b, i, 0))
    const = lambda shape: pl.BlockSpec(shape, lambda b, i: (0, 0))
    return pl.pallas_call(
        _merge_kernel,
        out_shape=jax.ShapeDtypeStruct((B, S, D), F32),
        grid=(B, S // tm),
        in_specs=[row(D), pl.BlockSpec((None, 1, D), lambda b, i: (b, 0, 0)),
                  row(BRANCH), row(BRANCH), row(BRANCH), row(BRANCH),
                  pl.BlockSpec((None, tm, N_BRANCH * D), lambda b, i: (b, i, GATE_COL0 // (N_BRANCH * D))),
                  const((BRANCH, D)), const((BRANCH, D)), const((BRANCH, D)), const((BRANCH, D)),
                  const((D, D))],
        out_specs=row(D),
        compiler_params=_params("parallel", "parallel"),
        name="merge",
    )(x, gm, f, cv, og, od, z, wf, wc, wa, wd, wo)


def _conv_kernel(a_ref, g_ref, ap_ref, gp_ref, an_ref, gn_ref, w_ref, b_ref, lg_ref, lb_ref, o_ref, u_s,
                 *, ts, chunk):
    i = pl.program_id(1)
    n = pl.num_programs(1)

    def glu(a, g):
        return a.astype(F32) * jax.nn.sigmoid(g.astype(F32))

    u_s[CONV_HALO:CONV_HALO + ts, :] = glu(a_ref[...], g_ref[...])
    u_s[0:CONV_HALO, :] = jnp.where(i > 0, glu(ap_ref[...], gp_ref[...]), 0.0)
    u_s[CONV_HALO + ts:2 * CONV_HALO + ts, :] = jnp.where(i < n - 1, glu(an_ref[...], gn_ref[...]), 0.0)
    pad = CONV_K // 2

    def body(c, carry):
        r0 = pl.multiple_of(c * chunk, chunk)
        win = u_s[pl.ds(r0, chunk + 2 * CONV_HALO), :]
        rows = chunk + 2 * CONV_HALO
        acc = jnp.zeros((chunk // SUBLANE, SUBLANE, BRANCH), F32) + b_ref[...]
        for r in range(SUBLANE):
            taps = [j for j in range(CONV_K) if (CONV_HALO - pad + j) % SUBLANE == r]
            shifted = win if r == 0 else pltpu.roll(win, rows - r, 0)
            for j in taps:
                base = CONV_HALO - pad + j - r
                tap = shifted[base:base + chunk, :].reshape(chunk // SUBLANE, SUBLANE, BRANCH)
                acc = acc + w_ref[j] * tap
        acc = acc.reshape(chunk, BRANCH)
        mu = jnp.mean(acc, axis=-1, keepdims=True)
        d = acc - mu
        var = jnp.mean(d * d, axis=-1, keepdims=True)
        y = d * lax.rsqrt(var + LN_EPS) * lg_ref[...] + lb_ref[...]
        o_ref[pl.ds(r0, chunk), :] = _silu(y).astype(o_ref.dtype)
        return carry

    lax.fori_loop(0, ts // chunk, body, 0)


def _conv_branch(z, w, b, ln_g, ln_b):
    B, S, _ = z.shape
    ts = min(512, S)
    chunk = min(256, ts)
    hb = ts // CONV_HALO
    last = S // CONV_HALO - 1
    cur = lambda blk: pl.BlockSpec((None, ts, BRANCH), lambda b_, i: (b_, i, blk))
    prev = lambda blk: pl.BlockSpec((None, CONV_HALO, BRANCH),
                                    lambda b_, i: (b_, jnp.maximum(i * hb - 1, 0), blk))
    nxt = lambda blk: pl.BlockSpec((None, CONV_HALO, BRANCH),
                                   lambda b_, i: (b_, jnp.minimum((i + 1) * hb, last), blk))
    const = lambda shape: pl.BlockSpec(shape, lambda b_, i: (0, 0))
    return pl.pallas_call(
        functools.partial(_conv_kernel, ts=ts, chunk=chunk),
        out_shape=jax.ShapeDtypeStruct((B, S, BRANCH), BF16),
        grid=(B, S // ts),
        in_specs=[cur(BLK_CA), cur(BLK_CG), prev(BLK_CA), prev(BLK_CG), nxt(BLK_CA), nxt(BLK_CG),
                  pl.BlockSpec((CONV_K, SUBLANE, BRANCH), lambda b_, i: (0, 0, 0)),
                  const((1, BRANCH)), const((1, BRANCH)), const((1, BRANCH))],
        out_specs=pl.BlockSpec((None, ts, BRANCH), lambda b_, i: (b_, i, 0)),
        scratch_shapes=[pltpu.VMEM((ts + 2 * CONV_HALO, BRANCH), F32)],
        compiler_params=_params("parallel", "parallel"),
        name="conv_branch",
    )(z, z, z, z, z, z, jnp.broadcast_to(w[:, None, :], (CONV_K, SUBLANE, BRANCH)), b, ln_g, ln_b)


def _head_rms(x, ones_bd):
    x2 = x * x
    hi = x2.astype(BF16)
    lo = (x2 - hi.astype(F32)).astype(BF16)
    ssum = (jnp.dot(hi, ones_bd, preferred_element_type=F32)
            + jnp.dot(lo, ones_bd, preferred_element_type=F32))
    return x * lax.rsqrt(ssum * (1.0 / HEAD_DIM) + EPS)


def _rope(x, cosf, sins):
    width = x.shape[1]
    reps = width // LANE
    c = jnp.concatenate([cosf] * reps, axis=1) if reps > 1 else cosf
    s = jnp.concatenate([sins] * reps, axis=1) if reps > 1 else sins
    lane = lax.broadcasted_iota(jnp.int32, x.shape, 1)
    first_half = (lane & (HEAD_DIM - 1)) < HEAD_DIM // 2
    partner = jnp.where(first_half, pltpu.roll(x, width - HEAD_DIM // 2, 1), pltpu.roll(x, HEAD_DIM // 2, 1))
    return x * c + partner * s


def _prep_kernel(gq_ref, dq_ref, dk_ref, dv_ref, kv_ref, cos_ref, sin_ref, qn_ref, kn_ref, bd_ref, *rest):
    qg_ref, kg_ref, vg_ref, qd_ref, kd_ref, vd_ref = rest[-6:]
    cosf, sins = cos_ref[...], sin_ref[...]
    bd = bd_ref[...]
    scale = HEAD_DIM ** -0.5 * math.log2(math.e)
    ts = gq_ref.shape[0]

    half = jnp.zeros((HEAD_DIM, ts), qg_ref.dtype)
    q = _rope(_head_rms(gq_ref[...].astype(F32), bd) * qn_ref[...], cosf, sins) * scale
    for p in range(BRANCH // LANE):
        qt = q[:, p * LANE:(p + 1) * LANE].T
        for e in range(2):
            h = 2 * p + e
            kvh = h // GQA_GROUP
            base = h * LANE
            qg_ref[base + kvh * HEAD_DIM:base + (kvh + 1) * HEAD_DIM, :] = (
                qt[e * HEAD_DIM:(e + 1) * HEAD_DIM, :].astype(qg_ref.dtype))
            qg_ref[base + (1 - kvh) * HEAD_DIM:base + (2 - kvh) * HEAD_DIM, :] = half

    kvz = kv_ref[...].astype(F32)
    k = _rope(_head_rms(kvz[:, :LANE], bd[:LANE, :LANE]) * kn_ref[:, :LANE], cosf, sins)
    kg_ref[...] = k.astype(kg_ref.dtype)
    vt = kvz[:, LANE:2 * LANE].T
    row = lax.broadcasted_iota(jnp.int32, (GQA_VT_ROWS - HEAD_DIM, ts), 0)
    tail = jnp.where(row == 0, 1.0, 0.0).astype(vg_ref.dtype)
    for g in range(GQA_KV_HEADS):
        vg_ref[g * GQA_VT_ROWS:g * GQA_VT_ROWS + HEAD_DIM, :] = vt[g * HEAD_DIM:(g + 1) * HEAD_DIM, :].astype(vg_ref.dtype)
        vg_ref[g * GQA_VT_ROWS + HEAD_DIM:(g + 1) * GQA_VT_ROWS, :] = tail

    qd = _rope(dq_ref[...].astype(F32), cosf, sins) * scale
    for h in range(DIFF_HEADS):
        qt = qd[:, h * LANE:(h + 1) * LANE].T.astype(qd_ref.dtype)
        base = 2 * h * LANE
        qd_ref[base:base + HEAD_DIM, :] = qt[:HEAD_DIM, :]
        qd_ref[base + HEAD_DIM:base + LANE, :] = half
        qd_ref[base + LANE:base + LANE + HEAD_DIM, :] = half
        qd_ref[base + LANE + HEAD_DIM:base + 2 * LANE, :] = qt[HEAD_DIM:, :]
        vd_ref[h * DIFF_VT_ROWS:h * DIFF_VT_ROWS + LANE, :] = (
            dv_ref[:, h * LANE:(h + 1) * LANE].astype(F32).T.astype(vd_ref.dtype))
        vd_ref[h * DIFF_VT_ROWS + LANE:(h + 1) * DIFF_VT_ROWS, :] = tail
    kd_ref[...] = _rope(dk_ref[...].astype(F32), cosf, sins).astype(kd_ref.dtype)


def _prep(z, cosf, sins, qn, kn, ones_bd, n_keys, key_start, kv_bufs=None):
    B, S, _ = z.shape
    ts = min(256, S)
    assert key_start % ts == 0
    k0 = key_start // ts
    slab = lambda blk: pl.BlockSpec((None, ts, BRANCH), lambda b, i: (b, i, blk))
    tab = pl.BlockSpec((ts, LANE), lambda b, i: (i, 0))
    const = lambda shape: pl.BlockSpec(shape, lambda b, i: (0, 0))
    q_cols = lambda chans: pl.BlockSpec((None, chans, ts), lambda b, i: (b, 0, i))
    k_rows = lambda width: pl.BlockSpec((None, ts, width), lambda b, i: (b, k0 + i, 0))
    v_cols = lambda chans: pl.BlockSpec((None, chans, ts), lambda b, i: (b, 0, k0 + i))
    vg_rows = GQA_KV_HEADS * GQA_VT_ROWS
    shapes = [(B, 2 * BRANCH, S), (B, n_keys, LANE), (B, vg_rows, n_keys),
              (B, 2 * BRANCH, S), (B, n_keys, BRANCH), (B, DIFF_HEADS * DIFF_VT_ROWS, n_keys)]
    specs = [q_cols(2 * BRANCH), k_rows(LANE), v_cols(vg_rows), q_cols(2 * BRANCH), k_rows(BRANCH), v_cols(DIFF_HEADS * DIFF_VT_ROWS)]
    in_specs = [slab(BLK_GQ), slab(BLK_DQ), slab(BLK_DK), slab(BLK_DV), slab(BLK_KV), tab, tab,
                const((1, BRANCH)), const((1, BRANCH)), const((BRANCH, BRANCH))]
    args = [z, z, z, z, z, cosf, sins, qn, kn, ones_bd]
    aliases = {}
    if kv_bufs is not None:
        for out_idx, buf in zip((1, 2, 4, 5), kv_bufs):
            aliases[len(args)] = out_idx
            in_specs.append(pl.BlockSpec(memory_space=pl.ANY))
            args.append(buf)
    return pl.pallas_call(
        _prep_kernel,
        out_shape=[jax.ShapeDtypeStruct(s, BF16) for s in shapes],
        grid=(B, S // ts),
        in_specs=in_specs,
        out_specs=specs,
        input_output_aliases=aliases,
        compiler_params=_params("parallel", "parallel"),
        name="qk_prep",
    )(*args)


def _attn_kernel(*refs, sets, tq, tk, mode, lam_init):
    if mode == "diff":
        q_ref, k_ref, v_ref, l1q, l1k, l2q, l2k, dn_ref, o_ref, q_s, acc_s = refs
    else:
        q_ref, k_ref, v_ref, o_ref, q_s, acc_s = refs
    n_tiles = k_ref.shape[0] // tk
    width = sets * tq
    for r in range(sets):
        q_s[:, r * tq:(r + 1) * tq] = q_ref[r * LANE:(r + 1) * LANE, :]

    def finish():
        if mode == "diff":
            ot = acc_s[:LANE, :] / acc_s[LANE:LANE + 1, :]
            o = [ot[:, r * tq:(r + 1) * tq].T for r in range(sets)]
            lam = (jnp.exp(jnp.sum(l1q[...] * l1k[...], axis=1, keepdims=True))
                   - jnp.exp(jnp.sum(l2q[...] * l2k[...], axis=1, keepdims=True)) + lam_init)
            d = o[0] - lam * o[1]
            d = d * lax.rsqrt(jnp.mean(d * d, axis=1, keepdims=True) + EPS) * dn_ref[...]
            o_ref[...] = (d * (1.0 - lam_init)).astype(o_ref.dtype)
        else:
            ot = acc_s[:HEAD_DIM, :] / acc_s[HEAD_DIM:HEAD_DIM + 1, :]
            for pair in range(sets // 2):
                both = jnp.concatenate([ot[:, (2 * pair) * tq:(2 * pair + 1) * tq],
                                        ot[:, (2 * pair + 1) * tq:(2 * pair + 2) * tq]], axis=0)
                o_ref[:, pair * LANE:(pair + 1) * LANE] = both.T.astype(o_ref.dtype)

    m_run = None
    for t in range(n_tiles):
        st = jnp.dot(k_ref[t * tk:(t + 1) * tk, :], q_s[...], preferred_element_type=F32)
        m_tile = jnp.max(st, axis=0, keepdims=True)
        pt = jnp.exp2(st - (m_tile if t == 0 else m_run))
        pv = jnp.dot(v_ref[:, t * tk:(t + 1) * tk], pt.astype(BF16), preferred_element_type=F32)
        if t == 0:
            m_run = m_tile
            acc_s[...] = pv
        else:
            m_new = jnp.maximum(m_run, m_tile)
            alpha = jnp.exp2(m_run - m_new)
            acc_s[...] = (acc_s[...] + pv) * alpha
            m_run = m_new
    finish()
    overflowed = jnp.max(jnp.where(jnp.isfinite(acc_s[...]), 0.0, 1.0)) > 0.0

    @pl.when(overflowed)
    def _exact():
        def body(t, m_old):
            st = jnp.dot(k_ref[pl.ds(pl.multiple_of(t * tk, tk), tk), :], q_s[...], preferred_element_type=F32)
            m_new = jnp.maximum(m_old, jnp.max(st, axis=0, keepdims=True))
            alpha = jnp.exp2(m_old - m_new)
            pt = jnp.exp2(st - m_new)
            v_tile = v_ref[:, pl.ds(pl.multiple_of(t * tk, LANE), tk)]
            acc_s[...] = alpha * acc_s[...] + jnp.dot(v_tile, pt.astype(BF16), preferred_element_type=F32)
            return m_new

        acc_s[...] = jnp.zeros(acc_s.shape, F32)
        lax.fori_loop(0, n_tiles, body, jnp.full((1, width), -jnp.inf, F32))
        finish()


def _attention(qt, k, vt, mode, key_start, NK, lam_vecs=None, diff_norm=None, lam_init=0.0):
    B, _, S = qt.shape
    assert key_start % NK == 0
    kb = key_start // NK
    tq = min(512 if mode == "diff" else 256, S)
    tk = 768
    tk = tk if NK % tk == 0 else 256
    if mode == "diff":
        groups, sets, out_w, vrows = DIFF_HEADS, 2, LANE, DIFF_VT_ROWS
        kmap = lambda b, g, i: (b, kb, g)
    else:
        groups, sets, out_w, vrows = GQA_KV_HEADS, GQA_GROUP, GQA_GROUP * HEAD_DIM, GQA_VT_ROWS
        kmap = lambda b, g, i: (b, kb, 0)
    in_specs = [pl.BlockSpec((None, sets * LANE, tq), lambda b, g, i: (b, g, i)),
                pl.BlockSpec((None, NK, LANE), kmap),
                pl.BlockSpec((None, vrows, NK), lambda b, g, i: (b, g, kb))]
    args = [qt, k, vt]
    if mode == "diff":
        in_specs += [pl.BlockSpec((1, HEAD_DIM), lambda b, g, i: (0, 0))] * 4
        in_specs += [pl.BlockSpec((1, LANE), lambda b, g, i: (0, 0))]
        args += list(lam_vecs) + [diff_norm]
    return pl.pallas_call(
        functools.partial(_attn_kernel, sets=sets, tq=tq, tk=tk, mode=mode, lam_init=lam_init),
        out_shape=jax.ShapeDtypeStruct((B, S, BRANCH), BF16),
        grid=(B, groups, S // tq),
        in_specs=in_specs,
        out_specs=pl.BlockSpec((None, tq, out_w), lambda b, g, i: (b, i, g)),
        scratch_shapes=[pltpu.VMEM((LANE, sets * tq), BF16),
                        pltpu.VMEM((vrows, sets * tq), F32)],
        compiler_params=_params("parallel", "parallel", "parallel"),
        name="attn_" + mode,
    )(*args)


def _dft_cos_sin(n):
    idx = np.arange(n)
    ang = 2.0 * np.pi * ((idx[:, None] * idx[None, :]) % n) / n
    return np.cos(ang), np.sin(ang)


def _channel_dft_matrix():
    c, s = _dft_cos_sin(GRID_W)
    groups = BRANCH // GRID_W
    eye = np.eye(groups)
    return np.concatenate([np.kron(eye, c), -np.kron(eye, s)], axis=1) / math.sqrt(GRID_W)


def _fourier_a_kernel(z_ref, w0_ref, d1_ref, tc_ref, ts_ref, o_ref):
    n1, nb, _ = z_ref.shape
    u = jnp.dot(z_ref[...].reshape(n1 * nb, BRANCH), w0_ref[...], preferred_element_type=F32)
    u = pltpu.einshape("kjm->jkm", u.reshape(n1, nb, 2 * BRANCH)).astype(BF16)
    reps = BRANCH // LANE
    out_r, out_i = [], []
    for j in range(nb):
        p = jnp.dot(d1_ref[...], u[j], preferred_element_type=F32)
        vr = p[:n1, :BRANCH] + p[n1:, BRANCH:]
        vi = p[:n1, BRANCH:] - p[n1:, :BRANCH]
        c = jnp.concatenate([tc_ref[j]] * reps, axis=1)
        s = jnp.concatenate([ts_ref[j]] * reps, axis=1)
        out_r.append(vr * c + vi * s)
        out_i.append(vi * c - vr * s)
    o_ref[0] = pltpu.einshape("jkm->kjm", jnp.stack(out_r)).astype(o_ref.dtype)
    o_ref[1] = pltpu.einshape("jkm->kjm", jnp.stack(out_i)).astype(o_ref.dtype)


def _fourier_b_kernel(v_ref, c3_ref, s3_ref, o_ref):
    nb = v_ref.shape[1]
    outs = [jnp.dot(c3_ref[...], v_ref[0, j], preferred_element_type=F32)
            + jnp.dot(s3_ref[...], v_ref[1, j], preferred_element_type=F32) for j in range(nb)]
    o_ref[...] = pltpu.einshape("jkm->kjm", jnp.stack(outs)).astype(o_ref.dtype)


def _bmm_kernel(a_ref, x_ref, o_ref):
    o_ref[...] = jnp.dot(a_ref[...], x_ref[...], preferred_element_type=F32).astype(o_ref.dtype)


def _bmm(a, x, tn):
    M, K = a.shape
    B, _, N = x.shape
    tn = min(tn, N)
    return pl.pallas_call(
        _bmm_kernel,
        out_shape=jax.ShapeDtypeStruct((B, M, N), BF16),
        grid=(B, N // tn),
        in_specs=[pl.BlockSpec((M, K), lambda b, j: (0, 0)),
                  pl.BlockSpec((None, K, tn), lambda b, j: (b, 0, j))],
        out_specs=pl.BlockSpec((None, M, tn), lambda b, j: (b, 0, j)),
        compiler_params=_params("parallel", "parallel"),
        name="bmm",
    )(a, x)


def _slab_matmul_kernel(z_ref, w_ref, o_ref):
    o_ref[...] = jnp.dot(z_ref[...], w_ref[...], preferred_element_type=F32).astype(o_ref.dtype)


def _slab_matmul(z, blk, w):
    B, S, _ = z.shape
    K, N = w.shape
    ts = min(256, S)
    return pl.pallas_call(
        _slab_matmul_kernel,
        out_shape=jax.ShapeDtypeStruct((B, S, N), BF16),
        grid=(B, S // ts),
        in_specs=[pl.BlockSpec((None, ts, K), lambda b, i: (b, i, blk)),
                  pl.BlockSpec((K, N), lambda b, i: (0, 0))],
        out_specs=pl.BlockSpec((None, ts, N), lambda b, i: (b, i, 0)),
        compiler_params=_params("parallel", "parallel"),
        name="slab_matmul",
    )(z, w)


def _fourier_long(z, w0):
    B, S, _ = z.shape
    n2 = GRID_W
    n1 = S // n2
    nb = FOURIER_NB
    assert n1 % nb == 0 and n2 % nb == 0
    c1, s1 = _dft_cos_sin(n1)
    d1 = jnp.asarray(np.concatenate([c1, s1], axis=0) / math.sqrt(n1), F32).astype(BF16)
    ang = 2.0 * np.pi * (np.arange(n2)[:, None] * np.arange(n1)[None, :]) / S
    twc = jnp.asarray(np.repeat(np.cos(ang)[:, :, None], LANE, axis=2), F32)
    tws = jnp.asarray(np.repeat(np.sin(ang)[:, :, None], LANE, axis=2), F32)
    v = pl.pallas_call(
        _fourier_a_kernel,
        out_shape=jax.ShapeDtypeStruct((B, 2, n1, n2, BRANCH), BF16),
        grid=(B, n2 // nb),
        in_specs=[pl.BlockSpec((None, n1, nb, BRANCH), lambda b, j: (b, 0, j, BLK_FOUR)),
                  pl.BlockSpec((BRANCH, 2 * BRANCH), lambda b, j: (0, 0)),
                  pl.BlockSpec((2 * n1, n1), lambda b, j: (0, 0)),
                  pl.BlockSpec((nb, n1, LANE), lambda b, j: (j, 0, 0)),
                  pl.BlockSpec((nb, n1, LANE), lambda b, j: (j, 0, 0))],
        out_specs=pl.BlockSpec((None, 2, n1, nb, BRANCH), lambda b, j: (b, 0, 0, j, 0)),
        compiler_params=_params("parallel", "parallel"),
        name="fourier_a",
    )(z.reshape(B, n1, n2, ZC), w0, d1, twc, tws)
    c3, s3 = _dft_cos_sin(n2)
    c3 = jnp.asarray(c3 / math.sqrt(n2), F32).astype(BF16)
    s3 = jnp.asarray(s3 / math.sqrt(n2), F32).astype(BF16)
    f = pl.pallas_call(
        _fourier_b_kernel,
        out_shape=jax.ShapeDtypeStruct((B, n2, n1, BRANCH), BF16),
        grid=(B, n1 // nb),
        in_specs=[pl.BlockSpec((None, 2, nb, n2, BRANCH), lambda b, i: (b, 0, i, 0, 0)),
                  pl.BlockSpec((n2, n2), lambda b, i: (0, 0)),
                  pl.BlockSpec((n2, n2), lambda b, i: (0, 0))],
        out_specs=pl.BlockSpec((None, n2, nb, BRANCH), lambda b, i: (b, 0, i, 0)),
        compiler_params=_params("parallel", "parallel"),
        name="fourier_b",
    )(v, c3, s3)
    return f.reshape(B, S, BRANCH)


def _fourier_short(z, w0):
    B, S, _ = z.shape
    u = _slab_matmul(z, BLK_FOUR, w0)
    ust = jnp.concatenate([u[:, :, :BRANCH], u[:, :, BRANCH:]], axis=1)
    c, s = _dft_cos_sin(S)
    dl = jnp.asarray(np.concatenate([c, s], axis=1) / math.sqrt(S), F32).astype(BF16)
    return _bmm(dl, ust, BRANCH)


def _rope_tables(n_rows):
    half = HEAD_DIM // 4
    inv = ROPE_BASE ** (-np.arange(0, HEAD_DIM // 2, 2, dtype=np.float64) / (HEAD_DIM // 2))
    row = np.repeat(np.arange(n_rows, dtype=np.float64), GRID_W)
    col = np.tile(np.arange(GRID_W, dtype=np.float64), n_rows)
    ang = np.concatenate([row[:, None] * inv, col[:, None] * inv], axis=-1)
    assert ang.shape[1] == 2 * half
    cos, sin = np.cos(ang), np.sin(ang)
    cosf = np.tile(np.concatenate([cos, cos], axis=1), (1, LANE // HEAD_DIM))
    sins = np.tile(np.concatenate([-sin, sin], axis=1), (1, LANE // HEAD_DIM))
    return jnp.asarray(cosf, F32), jnp.asarray(sins, F32)


def _permute_w_in(w):
    D = w.shape[0]
    four, conv, gq, gk, gv, dq, dk, dv, gates = jnp.split(
        w, np.cumsum([BRANCH, 2 * BRANCH, BRANCH, LANE, LANE, BRANCH, BRANCH, BRANCH])[:], axis=1)
    pad = jnp.zeros((D, BRANCH - 2 * LANE), w.dtype)
    out = jnp.concatenate([four, conv, gq, dq, dk, dv, gk, gv, pad, gates], axis=1)
    assert out.shape[1] == ZC
    return out.astype(BF16)


def _mixer_branches(z, lw, w0, long_seq):
    f = _fourier_long(z, w0) if long_seq else _fourier_short(z, w0)
    cv = _conv_branch(z, lw["conv_w"], lw["conv_b"], lw["conv_ln_g"], lw["conv_ln_b"])
    return f, cv


def kernel(x, c, ctx, c_ctx, w_ada, b_ada, norm_mix, w_in, w_four, conv_w, conv_b, conv_ln_g, conv_ln_b, w_conv, q_norm, k_norm, w_gqa, lam_q1, lam_k1, lam_q2, lam_k2, diff_norm, w_diff, w_out, norm_ffn, w_ffn1, w_ffn3, w_ffn2, final_norm):
    B, S, D = x.shape
    Sc = ctx.shape[1]
    depth = w_ada.shape[0]
    assert B + 1 <= 8 and S % GRID_W == 0 and D == 2 * BRANCH

    cond = jnp.zeros((8, D), F32).at[:B].set(c).at[B].set(c_ctx)
    mods = _ada_mod(cond, w_ada, b_ada)

    rope_x = _rope_tables(S // GRID_W)
    rope_c = (jnp.ones((Sc, LANE), F32), jnp.zeros((Sc, LANE), F32))
    ones_bd = jnp.asarray(np.kron(np.eye(BRANCH // HEAD_DIM), np.ones((HEAD_DIM, HEAD_DIM))), F32).astype(BF16)
    w0 = jnp.asarray(_channel_dft_matrix(), F32).astype(BF16)
    tile2 = lambda v: jnp.tile(v, LANE // HEAD_DIM * (BRANCH // LANE)).reshape(1, BRANCH)

    for l in range(depth):
        last = l == depth - 1
        lam_init = 0.8 - 0.6 * math.exp(-0.3 * l)
        mx = [m.reshape(B, 1, D) for m in jnp.split(mods[l, :B], 6, axis=-1)]
        mc = [jnp.broadcast_to(m.reshape(1, 1, D), (B, 1, D)) for m in jnp.split(mods[l, B], 6, axis=-1)]
        lw = dict(conv_w=conv_w[l], conv_b=conv_b[l].reshape(1, BRANCH),
                  conv_ln_g=conv_ln_g[l].reshape(1, BRANCH), conv_ln_b=conv_ln_b[l].reshape(1, BRANCH),
                  q_norm=tile2(q_norm[l]), k_norm=tile2(k_norm[l]))
        w_in_l = _permute_w_in(w_in[l])
        wf, wc, wa, wd, wo = (w.astype(BF16) for w in (w_four[l], w_conv[l], w_gqa[l], w_diff[l], w_out[l]))
        w1, w3, w2 = (w.astype(BF16) for w in (w_ffn1[l], w_ffn3[l], w_ffn2[l]))
        gain_m, gain_f = norm_mix[l].reshape(1, D), norm_ffn[l].reshape(1, D)
        lam_vecs = [v[l].reshape(1, HEAD_DIM) for v in (lam_q1, lam_k1, lam_q2, lam_k2)]
        dn = diff_norm[l].reshape(1, LANE)
        fin = final_norm.reshape(1, D)

        zx = _inproj(x, gain_m, mx[0], mx[1], w_in_l)
        zc = _inproj(ctx, gain_m, mc[0], mc[1], w_in_l)
        fx, cvx = _mixer_branches(zx, lw, w0, True)
        fc, cvc = _mixer_branches(zc, lw, w0, False)
        qgx, kg, vg, qdx, kd, vd = _prep(zx, rope_x[0], rope_x[1], lw["q_norm"], lw["k_norm"], ones_bd, S + Sc, 0)
        qgc, kg, vg, qdc, kd, vd = _prep(zc, rope_c[0], rope_c[1], lw["q_norm"], lw["k_norm"], ones_bd, S + Sc, S,
                                         kv_bufs=(kg, vg, kd, vd))

        ogx = _attention(qgx, kg, vg, "gqa", 0, S + Sc)
        odx = _attention(qdx, kd, vd, "diff", 0, S + Sc, lam_vecs, dn, lam_init)
        x = _merge(x, mx[2], fx, cvx, ogx, odx, zx, wf, wc, wa, wd, wo)
        if not last:
            ogc = _attention(qgc, kg, vg, "gqa", S, Sc)
            odc = _attention(qdc, kd, vd, "diff", S, Sc, lam_vecs, dn, lam_init)
            ctx = _merge(ctx, mc[2], fc, cvc, ogc, odc, zc, wf, wc, wa, wd, wo)
            ctx = _ffn(ctx, gain_f, mc[3], mc[4], mc[5], w1, w3, w2, fin, False)
        x = _ffn(x, gain_f, mx[3], mx[4], mx[5], w1, w3, w2, fin, last)
    return x
```

```python
import functools
import math

import numpy as np
import jax
import jax.numpy as jnp
from jax import lax
from jax.experimental import pallas as pl
from jax.experimental.pallas import tpu as pltpu

F32 = jnp.float32
BF16 = jnp.bfloat16

HEAD_DIM = 64
BRANCH = 512
GQA_GROUP = 4
GQA_KV_HEADS = 2
DIFF_VT_ROWS = 144
GQA_VT_ROWS = 80
DIFF_HEADS = 4
N_BRANCH = 4
GRID_W = 64
CONV_K = 31
ROPE_BASE = 10000.0
EPS = 1e-6
LN_EPS = 1e-5

LANE = 128
SUBLANE = 8
VMEM_LIMIT = 56 * 1024 * 1024

ZC = 8192
BLK_FOUR, BLK_CA, BLK_CG, BLK_GQ, BLK_DQ, BLK_DK, BLK_DV, BLK_KV = range(8)
GATE_COL0 = 4096
FOURIER_NB = 16
CONV_HALO = 16


def _params(*sem):
    return pltpu.CompilerParams(dimension_semantics=sem, vmem_limit_bytes=VMEM_LIMIT)


def _silu(x):
    return x * jax.nn.sigmoid(x)


def _rms_mod(x, g, shift, scale):
    y = x * lax.rsqrt(jnp.mean(x * x, axis=-1, keepdims=True) + EPS) * g
    return y * (1.0 + scale) + shift


def _ada_kernel(c_ref, w_ref, b_ref, o_ref):
    a = _silu(c_ref[...])
    o_ref[...] = jnp.dot(a, w_ref[...], precision=lax.Precision.HIGHEST,
                         preferred_element_type=F32) + b_ref[...]


def _ada_mod(cond, w_ada, b_ada):
    L, D, N = w_ada.shape
    tn = 1536
    return pl.pallas_call(
        _ada_kernel,
        out_shape=jax.ShapeDtypeStruct((L, 8, N), F32),
        grid=(L, N // tn),
        in_specs=[pl.BlockSpec((8, D), lambda l, j: (0, 0)),
                  pl.BlockSpec((None, D, tn), lambda l, j: (l, 0, j)),
                  pl.BlockSpec((None, 1, tn), lambda l, j: (l, 0, j))],
        out_specs=pl.BlockSpec((None, 8, tn), lambda l, j: (l, 0, j)),
        compiler_params=_params("parallel", "parallel"),
        name="ada_mod",
    )(cond, w_ada, b_ada.reshape(L, 1, N))


def _inproj_kernel(x_ref, g_ref, sh_ref, sc_ref, w_ref, o_ref):
    h = _rms_mod(x_ref[...], g_ref[...], sh_ref[...], sc_ref[...])
    o_ref[...] = jnp.dot(h.astype(BF16), w_ref[...], preferred_element_type=F32).astype(o_ref.dtype)


def _inproj(x, gain, shift, scale, w):
    B, S, D = x.shape
    N = w.shape[1]
    tm = min(512, S)
    tn = 4096
    return pl.pallas_call(
        _inproj_kernel,
        out_shape=jax.ShapeDtypeStruct((B, S, N), BF16),
        grid=(N // tn, B, S // tm),
        in_specs=[pl.BlockSpec((None, tm, D), lambda j, b, i: (b, i, 0)),
                  pl.BlockSpec((1, D), lambda j, b, i: (0, 0)),
                  pl.BlockSpec((None, 1, D), lambda j, b, i: (b, 0, 0)),
                  pl.BlockSpec((None, 1, D), lambda j, b, i: (b, 0, 0)),
                  pl.BlockSpec((D, tn), lambda j, b, i: (0, j))],
        out_specs=pl.BlockSpec((None, tm, tn), lambda j, b, i: (b, i, j)),
        compiler_params=_params("parallel", "parallel", "parallel"),
        name="in_proj",
    )(x, gain, shift, scale, w)


def _inproj_qkv_kernel(x_ref, g_ref, sh_ref, sc_ref, w_ref, cos_ref, sin_ref, qn_ref, kn_ref, bd_ref, *rest):
    z_ref, qg_ref, kg_ref, vg_ref, qd_ref, kd_ref, vd_ref = rest[-7:]
    h = _rms_mod(x_ref[...], g_ref[...], sh_ref[...], sc_ref[...])
    acc = jnp.dot(h.astype(BF16), w_ref[...], preferred_element_type=F32)
    z_ref[...] = acc[:, :z_ref.shape[1]].astype(z_ref.dtype)
    slab = lambda blk: acc[:, blk * BRANCH:(blk + 1) * BRANCH]
    _qkv_store(slab(BLK_GQ), slab(BLK_DQ), slab(BLK_DK), slab(BLK_DV), slab(BLK_KV),
               cos_ref[...], sin_ref[...], qn_ref[...], kn_ref[...], bd_ref[...],
               qg_ref, kg_ref, vg_ref, qd_ref, kd_ref, vd_ref)


def _inproj_qkv(x, gain, shift, scale, w, cosf, sins, qn, kn, ones_bd, n_keys, key_start, kv_bufs=None):
    B, S, D = x.shape
    N = w.shape[1]
    tm = min(512, S)
    assert key_start % tm == 0
    k0 = key_start // tm
    const = lambda shape: pl.BlockSpec(shape, lambda b, i: (0, 0))
    per_b = pl.BlockSpec((None, 1, D), lambda b, i: (b, 0, 0))
    tab = pl.BlockSpec((tm, LANE), lambda b, i: (i, 0))
    rows = lambda width: pl.BlockSpec((None, tm, width), lambda b, i: (b, i, 0))
    q_cols = lambda chans: pl.BlockSpec((None, chans, tm), lambda b, i: (b, 0, i))
    k_rows = lambda width: pl.BlockSpec((None, tm, width), lambda b, i: (b, k0 + i, 0))
    v_cols = lambda chans: pl.BlockSpec((None, chans, tm), lambda b, i: (b, 0, k0 + i))
    vg_rows, vd_rows = GQA_KV_HEADS * GQA_VT_ROWS, DIFF_HEADS * DIFF_VT_ROWS
    shapes = [(B, S, 3 * BRANCH), (B, 2 * BRANCH, S), (B, n_keys, LANE), (B, vg_rows, n_keys),
              (B, 2 * BRANCH, S), (B, n_keys, BRANCH), (B, vd_rows, n_keys)]
    specs = [rows(3 * BRANCH), q_cols(2 * BRANCH), k_rows(LANE), v_cols(vg_rows),
             q_cols(2 * BRANCH), k_rows(BRANCH), v_cols(vd_rows)]
    in_specs = [rows(D), const((1, D)), per_b, per_b, const((D, N)), tab, tab,
                const((1, BRANCH)), const((1, BRANCH)), const((BRANCH, BRANCH))]
    args = [x, gain, shift, scale, w, cosf, sins, qn, kn, ones_bd]
    aliases = {}
    if kv_bufs is not None:
        for out_idx, buf in zip((2, 3, 5, 6), kv_bufs):
            aliases[len(args)] = out_idx
            in_specs.append(pl.BlockSpec(memory_space=pl.ANY))
            args.append(buf)
    return pl.pallas_call(
        _inproj_qkv_kernel,
        out_shape=[jax.ShapeDtypeStruct(s, BF16) for s in shapes],
        grid=(B, S // tm),
        in_specs=in_specs,
        out_specs=specs,
        input_output_aliases=aliases,
        compiler_params=_params("parallel", "parallel"),
        name="in_proj_qkv",
    )(*args)


def _ffn_kernel(x_ref, g_ref, sh_ref, sc_ref, gate_ref, w1_ref, w3_ref, w2_ref, fg_ref, o_ref, *, final):
    x = x_ref[...]
    h = _rms_mod(x, g_ref[...], sh_ref[...], sc_ref[...]).astype(BF16)
    a = jnp.dot(h, w1_ref[...], preferred_element_type=F32)
    b = jnp.dot(h, w3_ref[...], preferred_element_type=F32)
    u = (_silu(a) * b).astype(BF16)
    y = x + gate_ref[...] * jnp.dot(u, w2_ref[...], preferred_element_type=F32)
    if final:
        y = y * lax.rsqrt(jnp.mean(y * y, axis=-1, keepdims=True) + EPS) * fg_ref[...]
    o_ref[...] = y


def _ffn(x, gain, shift, scale, gate, w1, w3, w2, final_gain, final):
    B, S, D = x.shape
    H = w1.shape[1]
    tm = min(512, S)
    const = lambda shape: pl.BlockSpec(shape, lambda b, i: (0,) * len(shape), pipeline_mode=pl.Buffered(1))
    per_b = pl.BlockSpec((None, 1, D), lambda b, i: (b, 0, 0))
    return pl.pallas_call(
        functools.partial(_ffn_kernel, final=final),
        out_shape=jax.ShapeDtypeStruct((B, S, D), F32),
        grid=(B, S // tm),
        in_specs=[pl.BlockSpec((None, tm, D), lambda b, i: (b, i, 0)),
                  const((1, D)), per_b, per_b, per_b,
                  const((D, H)), const((D, H)), const((H, D)), const((1, D))],
        out_specs=pl.BlockSpec((None, tm, D), lambda b, i: (b, i, 0)),
        compiler_params=_params("parallel", "parallel"),
        name="ffn",
    )(x, gain, shift, scale, gate, w1, w3, w2, final_gain)


def _merge_kernel(x_ref, gm_ref, f_ref, c_ref, a_ref, d_ref, zg_ref,
                  wf_ref, wc_ref, wa_ref, wd_ref, wo_ref, o_ref):
    D = x_ref.shape[-1]
    acc = None
    for b, (br, w) in enumerate(((f_ref, wf_ref), (c_ref, wc_ref), (a_ref, wa_ref), (d_ref, wd_ref))):
        y = jnp.dot(br[...], w[...], preferred_element_type=F32)
        gate = jax.nn.sigmoid(zg_ref[:, b * D:(b + 1) * D].astype(F32))
        acc = gate * y if acc is None else acc + gate * y
    out = jnp.dot(acc.astype(BF16), wo_ref[...], preferred_element_type=F32)
    o_ref[...] = x_ref[...] + gm_ref[...] * out


def _merge(x, gm, f, cv, og, od, zg, wf, wc, wa, wd, wo):
    B, S, D = x.shape
    tm = min(512, S)
    row = lambda width: pl.BlockSpec((None, tm, width), lambda b, i: (b, i, 0))
    const = lambda shape: pl.BlockSpec(shape, lambda b, i: (0, 0))
    return pl.pallas_call(
        _merge_kernel,
        out_shape=jax.ShapeDtypeStruct((B, S, D), F32),
        grid=(B, S // tm),
        in_specs=[row(D), pl.BlockSpec((None, 1, D), lambda b, i: (b, 0, 0)),
                  row(BRANCH), row(BRANCH), row(BRANCH), row(BRANCH),
                  pl.BlockSpec((None, tm, N_BRANCH * D), lambda b, i: (b, i, 0)),
                  const((BRANCH, D)), const((BRANCH, D)), const((BRANCH, D)), const((BRANCH, D)),
                  const((D, D))],
        out_specs=row(D),
        compiler_params=_params("parallel", "parallel"),
        name="merge",
    )(x, gm, f, cv, og, od, zg, wf, wc, wa, wd, wo)


def _conv_kernel(a_ref, g_ref, ap_ref, gp_ref, an_ref, gn_ref, w_ref, b_ref, lg_ref, lb_ref, o_ref, u_s,
                 *, ts, chunk):
    i = pl.program_id(1)
    n = pl.num_programs(1)

    def glu(a, g):
        return a.astype(F32) * jax.nn.sigmoid(g.astype(F32))

    u_s[CONV_HALO:CONV_HALO + ts, :] = glu(a_ref[...], g_ref[...])
    u_s[0:CONV_HALO, :] = jnp.where(i > 0, glu(ap_ref[...], gp_ref[...]), 0.0)
    u_s[CONV_HALO + ts:2 * CONV_HALO + ts, :] = jnp.where(i < n - 1, glu(an_ref[...], gn_ref[...]), 0.0)
    pad = CONV_K // 2

    def body(c, carry):
        r0 = pl.multiple_of(c * chunk, chunk)
        win = u_s[pl.ds(r0, chunk + 2 * CONV_HALO), :]
        rows = chunk + 2 * CONV_HALO
        acc = jnp.zeros((chunk // SUBLANE, SUBLANE, BRANCH), F32) + b_ref[...]
        for r in range(SUBLANE):
            taps = [j for j in range(CONV_K) if (CONV_HALO - pad + j) % SUBLANE == r]
            shifted = win if r == 0 else pltpu.roll(win, rows - r, 0)
            for j in taps:
                base = CONV_HALO - pad + j - r
                tap = shifted[base:base + chunk, :].reshape(chunk // SUBLANE, SUBLANE, BRANCH)
                acc = acc + w_ref[j] * tap
        acc = acc.reshape(chunk, BRANCH)
        mu = jnp.mean(acc, axis=-1, keepdims=True)
        d = acc - mu
        var = jnp.mean(d * d, axis=-1, keepdims=True)
        y = d * lax.rsqrt(var + LN_EPS) * lg_ref[...] + lb_ref[...]
        o_ref[pl.ds(r0, chunk), :] = _silu(y).astype(o_ref.dtype)
        return carry

    lax.fori_loop(0, ts // chunk, body, 0)


def _conv_branch(z, w, b, ln_g, ln_b):
    B, S, _ = z.shape
    ts = min(512, S)
    chunk = min(256, ts)
    hb = ts // CONV_HALO
    last = S // CONV_HALO - 1
    cur = lambda blk: pl.BlockSpec((None, ts, BRANCH), lambda b_, i: (b_, i, blk))
    prev = lambda blk: pl.BlockSpec((None, CONV_HALO, BRANCH),
                                    lambda b_, i: (b_, jnp.maximum(i * hb - 1, 0), blk))
    nxt = lambda blk: pl.BlockSpec((None, CONV_HALO, BRANCH),
                                   lambda b_, i: (b_, jnp.minimum((i + 1) * hb, last), blk))
    const = lambda shape: pl.BlockSpec(shape, lambda b_, i: (0, 0))
    return pl.pallas_call(
        functools.partial(_conv_kernel, ts=ts, chunk=chunk),
        out_shape=jax.ShapeDtypeStruct((B, S, BRANCH), BF16),
        grid=(B, S // ts),
        in_specs=[cur(BLK_CA), cur(BLK_CG), prev(BLK_CA), prev(BLK_CG), nxt(BLK_CA), nxt(BLK_CG),
                  pl.BlockSpec((CONV_K, SUBLANE, BRANCH), lambda b_, i: (0, 0, 0)),
                  const((1, BRANCH)), const((1, BRANCH)), const((1, BRANCH))],
        out_specs=pl.BlockSpec((None, ts, BRANCH), lambda b_, i: (b_, i, 0)),
        scratch_shapes=[pltpu.VMEM((ts + 2 * CONV_HALO, BRANCH), F32)],
        compiler_params=_params("parallel", "parallel"),
        name="conv_branch",
    )(z, z, z, z, z, z, jnp.broadcast_to(w[:, None, :], (CONV_K, SUBLANE, BRANCH)), b, ln_g, ln_b)


def _head_rms(x, ones_bd):
    x2 = x * x
    hi = x2.astype(BF16)
    lo = (x2 - hi.astype(F32)).astype(BF16)
    ssum = (jnp.dot(hi, ones_bd, preferred_element_type=F32)
            + jnp.dot(lo, ones_bd, preferred_element_type=F32))
    return x * lax.rsqrt(ssum * (1.0 / HEAD_DIM) + EPS)


def _rope(x, cosf, sins):
    width = x.shape[1]
    reps = width // LANE
    c = jnp.concatenate([cosf] * reps, axis=1) if reps > 1 else cosf
    s = jnp.concatenate([sins] * reps, axis=1) if reps > 1 else sins
    lane = lax.broadcasted_iota(jnp.int32, x.shape, 1)
    first_half = (lane & (HEAD_DIM - 1)) < HEAD_DIM // 2
    partner = jnp.where(first_half, pltpu.roll(x, width - HEAD_DIM // 2, 1), pltpu.roll(x, HEAD_DIM // 2, 1))
    return x * c + partner * s


def _qkv_store(gq, dq, dk, dv, kvz, cosf, sins, qn, kn, bd, qg_ref, kg_ref, vg_ref, qd_ref, kd_ref, vd_ref):
    scale = HEAD_DIM ** -0.5 * math.log2(math.e)
    ts = gq.shape[0]

    half = jnp.zeros((HEAD_DIM, ts), qg_ref.dtype)
    q = _rope(_head_rms(gq, bd) * qn, cosf, sins) * scale
    for p in range(BRANCH // LANE):
        qt = q[:, p * LANE:(p + 1) * LANE].T
        for e in range(2):
            h = 2 * p + e
            kvh = h // GQA_GROUP
            base = h * LANE
            qg_ref[base + kvh * HEAD_DIM:base + (kvh + 1) * HEAD_DIM, :] = (
                qt[e * HEAD_DIM:(e + 1) * HEAD_DIM, :].astype(qg_ref.dtype))
            qg_ref[base + (1 - kvh) * HEAD_DIM:base + (2 - kvh) * HEAD_DIM, :] = half

    k = _rope(_head_rms(kvz[:, :LANE], bd[:LANE, :LANE]) * kn[:, :LANE], cosf, sins)
    kg_ref[...] = k.astype(kg_ref.dtype)
    vt = kvz[:, LANE:2 * LANE].T
    row = lax.broadcasted_iota(jnp.int32, (GQA_VT_ROWS - HEAD_DIM, ts), 0)
    tail = jnp.where(row == 0, 1.0, 0.0).astype(vg_ref.dtype)
    for g in range(GQA_KV_HEADS):
        vg_ref[g * GQA_VT_ROWS:g * GQA_VT_ROWS + HEAD_DIM, :] = vt[g * HEAD_DIM:(g + 1) * HEAD_DIM, :].astype(vg_ref.dtype)
        vg_ref[g * GQA_VT_ROWS + HEAD_DIM:(g + 1) * GQA_VT_ROWS, :] = tail

    qd = _rope(dq, cosf, sins) * scale
    for h in range(DIFF_HEADS):
        qt = qd[:, h * LANE:(h + 1) * LANE].T.astype(qd_ref.dtype)
        base = 2 * h * LANE
        qd_ref[base:base + HEAD_DIM, :] = qt[:HEAD_DIM, :]
        qd_ref[base + HEAD_DIM:base + LANE, :] = half
        qd_ref[base + LANE:base + LANE + HEAD_DIM, :] = half
        qd_ref[base + LANE + HEAD_DIM:base + 2 * LANE, :] = qt[HEAD_DIM:, :]
        vd_ref[h * DIFF_VT_ROWS:h * DIFF_VT_ROWS + LANE, :] = (
            dv[:, h * LANE:(h + 1) * LANE].T.astype(vd_ref.dtype))
        vd_ref[h * DIFF_VT_ROWS + LANE:(h + 1) * DIFF_VT_ROWS, :] = tail
    kd_ref[...] = _rope(dk, cosf, sins).astype(kd_ref.dtype)


def _attn_kernel(*refs, sets, tq, tk, mode, lam_init):
    if mode == "diff":
        q_ref, k_ref, v_ref, l1q, l1k, l2q, l2k, dn_ref, o_ref, q_s, acc_s = refs
    else:
        q_ref, k_ref, v_ref, o_ref, q_s, acc_s = refs
    n_tiles = k_ref.shape[0] // tk
    width = sets * tq
    for r in range(sets):
        q_s[:, r * tq:(r + 1) * tq] = q_ref[r * LANE:(r + 1) * LANE, :]

    def finish():
        if mode == "diff":
            ot = acc_s[:LANE, :] / acc_s[LANE:LANE + 1, :]
            o = [ot[:, r * tq:(r + 1) * tq].T for r in range(sets)]
            lam = (jnp.exp(jnp.sum(l1q[...] * l1k[...], axis=1, keepdims=True))
                   - jnp.exp(jnp.sum(l2q[...] * l2k[...], axis=1, keepdims=True)) + lam_init)
            d = o[0] - lam * o[1]
            d = d * lax.rsqrt(jnp.mean(d * d, axis=1, keepdims=True) + EPS) * dn_ref[...]
            o_ref[...] = (d * (1.0 - lam_init)).astype(o_ref.dtype)
        else:
            ot = acc_s[:HEAD_DIM, :] / acc_s[HEAD_DIM:HEAD_DIM + 1, :]
            for pair in range(sets // 2):
                both = jnp.concatenate([ot[:, (2 * pair) * tq:(2 * pair + 1) * tq],
                                        ot[:, (2 * pair + 1) * tq:(2 * pair + 2) * tq]], axis=0)
                o_ref[:, pair * LANE:(pair + 1) * LANE] = both.T.astype(o_ref.dtype)

    m_run = None
    for t in range(n_tiles):
        st = jnp.dot(k_ref[t * tk:(t + 1) * tk, :], q_s[...], preferred_element_type=F32)
        m_tile = jnp.max(st, axis=0, keepdims=True)
        pt = jnp.exp2(st - (m_tile if t == 0 else m_run))
        pv = jnp.dot(v_ref[:, t * tk:(t + 1) * tk], pt.astype(BF16), preferred_element_type=F32)
        if t == 0:
            m_run = m_tile
            acc_s[...] = pv
        else:
            m_new = jnp.maximum(m_run, m_tile)
            alpha = jnp.exp2(m_run - m_new)
            acc_s[...] = (acc_s[...] + pv) * alpha
            m_run = m_new
    finish()
    overflowed = jnp.max(jnp.where(jnp.isfinite(acc_s[...]), 0.0, 1.0)) > 0.0

    @pl.when(overflowed)
    def _exact():
        def body(t, m_old):
            st = jnp.dot(k_ref[pl.ds(pl.multiple_of(t * tk, tk), tk), :], q_s[...], preferred_element_type=F32)
            m_new = jnp.maximum(m_old, jnp.max(st, axis=0, keepdims=True))
            alpha = jnp.exp2(m_old - m_new)
            pt = jnp.exp2(st - m_new)
            v_tile = v_ref[:, pl.ds(pl.multiple_of(t * tk, LANE), tk)]
            acc_s[...] = alpha * acc_s[...] + jnp.dot(v_tile, pt.astype(BF16), preferred_element_type=F32)
            return m_new

        acc_s[...] = jnp.zeros(acc_s.shape, F32)
        lax.fori_loop(0, n_tiles, body, jnp.full((1, width), -jnp.inf, F32))
        finish()


def _attention(qt, k, vt, mode, key_start, NK, lam_vecs=None, diff_norm=None, lam_init=0.0):
    B, _, S = qt.shape
    assert key_start % NK == 0
    kb = key_start // NK
    tq = min(512 if mode == "diff" else 256, S)
    tk = 768
    tk = tk if NK % tk == 0 else 256
    if mode == "diff":
        groups, sets, out_w, vrows = DIFF_HEADS, 2, LANE, DIFF_VT_ROWS
        kmap = lambda b, g, i: (b, kb, g)
    else:
        groups, sets, out_w, vrows = GQA_KV_HEADS, GQA_GROUP, GQA_GROUP * HEAD_DIM, GQA_VT_ROWS
        kmap = lambda b, g, i: (b, kb, 0)
    in_specs = [pl.BlockSpec((None, sets * LANE, tq), lambda b, g, i: (b, g, i)),
                pl.BlockSpec((None, NK, LANE), kmap),
                pl.BlockSpec((None, vrows, NK), lambda b, g, i: (b, g, kb))]
    args = [qt, k, vt]
    if mode == "diff":
        in_specs += [pl.BlockSpec((1, HEAD_DIM), lambda b, g, i: (0, 0))] * 4
        in_specs += [pl.BlockSpec((1, LANE), lambda b, g, i: (0, 0))]
        args += list(lam_vecs) + [diff_norm]
    return pl.pallas_call(
        functools.partial(_attn_kernel, sets=sets, tq=tq, tk=tk, mode=mode, lam_init=lam_init),
        out_shape=jax.ShapeDtypeStruct((B, S, BRANCH), BF16),
        grid=(B, groups, S // tq),
        in_specs=in_specs,
        out_specs=pl.BlockSpec((None, tq, out_w), lambda b, g, i: (b, i, g)),
        scratch_shapes=[pltpu.VMEM((LANE, sets * tq), BF16),
                        pltpu.VMEM((vrows, sets * tq), F32)],
        compiler_params=_params("parallel", "parallel", "parallel"),
        name="attn_" + mode,
    )(*args)


def _dft_cos_sin(n):
    idx = np.arange(n)
    ang = 2.0 * np.pi * ((idx[:, None] * idx[None, :]) % n) / n
    return np.cos(ang), np.sin(ang)


def _channel_dft_matrix():
    c, s = _dft_cos_sin(GRID_W)
    groups = BRANCH // GRID_W
    eye = np.eye(groups)
    return np.concatenate([np.kron(eye, c), -np.kron(eye, s)], axis=1) / math.sqrt(GRID_W)


def _fourier_a_kernel(z_ref, w0_ref, d1_ref, tc_ref, ts_ref, o_ref):
    n1, nb, _ = z_ref.shape
    u = jnp.dot(z_ref[...].reshape(n1 * nb, BRANCH), w0_ref[...], preferred_element_type=F32)
    u = pltpu.einshape("kjm->jkm", u.reshape(n1, nb, 2 * BRANCH)).astype(BF16)
    reps = BRANCH // LANE
    out_r, out_i = [], []
    for j in range(nb):
        p = jnp.dot(d1_ref[...], u[j], preferred_element_type=F32)
        vr = p[:n1, :BRANCH] + p[n1:, BRANCH:]
        vi = p[:n1, BRANCH:] - p[n1:, :BRANCH]
        c = jnp.concatenate([tc_ref[j]] * reps, axis=1)
        s = jnp.concatenate([ts_ref[j]] * reps, axis=1)
        out_r.append(vr * c + vi * s)
        out_i.append(vi * c - vr * s)
    o_ref[0] = pltpu.einshape("jkm->kjm", jnp.stack(out_r)).astype(o_ref.dtype)
    o_ref[1] = pltpu.einshape("jkm->kjm", jnp.stack(out_i)).astype(o_ref.dtype)


def _fourier_b_kernel(v_ref, c3_ref, s3_ref, o_ref):
    nb = v_ref.shape[1]
    outs = [jnp.dot(c3_ref[...], v_ref[0, j], preferred_element_type=F32)
            + jnp.dot(s3_ref[...], v_ref[1, j], preferred_element_type=F32) for j in range(nb)]
    o_ref[...] = pltpu.einshape("jkm->kjm", jnp.stack(outs)).astype(o_ref.dtype)


def _bmm_kernel(a_ref, x_ref, o_ref):
    o_ref[...] = jnp.dot(a_ref[...], x_ref[...], preferred_element_type=F32).astype(o_ref.dtype)


def _bmm(a, x, tn):
    M, K = a.shape
    B, _, N = x.shape
    tn = min(tn, N)
    return pl.pallas_call(
        _bmm_kernel,
        out_shape=jax.ShapeDtypeStruct((B, M, N), BF16),
        grid=(B, N // tn),
        in_specs=[pl.BlockSpec((M, K), lambda b, j: (0, 0)),
                  pl.BlockSpec((None, K, tn), lambda b, j: (b, 0, j))],
        out_specs=pl.BlockSpec((None, M, tn), lambda b, j: (b, 0, j)),
        compiler_params=_params("parallel", "parallel"),
        name="bmm",
    )(a, x)


def _slab_matmul_kernel(z_ref, w_ref, o_ref):
    o_ref[...] = jnp.dot(z_ref[...], w_ref[...], preferred_element_type=F32).astype(o_ref.dtype)


def _slab_matmul(z, blk, w):
    B, S, _ = z.shape
    K, N = w.shape
    ts = min(256, S)
    return pl.pallas_call(
        _slab_matmul_kernel,
        out_shape=jax.ShapeDtypeStruct((B, S, N), BF16),
        grid=(B, S // ts),
        in_specs=[pl.BlockSpec((None, ts, K), lambda b, i: (b, i, blk)),
                  pl.BlockSpec((K, N), lambda b, i: (0, 0))],
        out_specs=pl.BlockSpec((None, ts, N), lambda b, i: (b, i, 0)),
        compiler_params=_params("parallel", "parallel"),
        name="slab_matmul",
    )(z, w)


def _fourier_long(z, w0):
    B, S, _ = z.shape
    n2 = GRID_W
    n1 = S // n2
    nb = FOURIER_NB
    assert n1 % nb == 0 and n2 % nb == 0
    c1, s1 = _dft_cos_sin(n1)
    d1 = jnp.asarray(np.concatenate([c1, s1], axis=0) / math.sqrt(n1), F32).astype(BF16)
    ang = 2.0 * np.pi * (np.arange(n2)[:, None] * np.arange(n1)[None, :]) / S
    twc = jnp.asarray(np.repeat(np.cos(ang)[:, :, None], LANE, axis=2), F32)
    tws = jnp.asarray(np.repeat(np.sin(ang)[:, :, None], LANE, axis=2), F32)
    v = pl.pallas_call(
        _fourier_a_kernel,
        out_shape=jax.ShapeDtypeStruct((B, 2, n1, n2, BRANCH), BF16),
        grid=(B, n2 // nb),
        in_specs=[pl.BlockSpec((None, n1, nb, BRANCH), lambda b, j: (b, 0, j, BLK_FOUR)),
                  pl.BlockSpec((BRANCH, 2 * BRANCH), lambda b, j: (0, 0)),
                  pl.BlockSpec((2 * n1, n1), lambda b, j: (0, 0)),
                  pl.BlockSpec((nb, n1, LANE), lambda b, j: (j, 0, 0)),
                  pl.BlockSpec((nb, n1, LANE), lambda b, j: (j, 0, 0))],
        out_specs=pl.BlockSpec((None, 2, n1, nb, BRANCH), lambda b, j: (b, 0, 0, j, 0)),
        compiler_params=_params("parallel", "parallel"),
        name="fourier_a",
    )(z.reshape(B, n1, n2, z.shape[-1]), w0, d1, twc, tws)
    c3, s3 = _dft_cos_sin(n2)
    c3 = jnp.asarray(c3 / math.sqrt(n2), F32).astype(BF16)
    s3 = jnp.asarray(s3 / math.sqrt(n2), F32).astype(BF16)
    f = pl.pallas_call(
        _fourier_b_kernel,
        out_shape=jax.ShapeDtypeStruct((B, n2, n1, BRANCH), BF16),
        grid=(B, n1 // nb),
        in_specs=[pl.BlockSpec((None, 2, nb, n2, BRANCH), lambda b, i: (b, 0, i, 0, 0)),
                  pl.BlockSpec((n2, n2), lambda b, i: (0, 0)),
                  pl.BlockSpec((n2, n2), lambda b, i: (0, 0))],
        out_specs=pl.BlockSpec((None, n2, nb, BRANCH), lambda b, i: (b, 0, i, 0)),
        compiler_params=_params("parallel", "parallel"),
        name="fourier_b",
    )(v, c3, s3)
    return f.reshape(B, S, BRANCH)


def _fourier_short(z, w0):
    B, S, _ = z.shape
    u = _slab_matmul(z, BLK_FOUR, w0)
    ust = jnp.concatenate([u[:, :, :BRANCH], u[:, :, BRANCH:]], axis=1)
    c, s = _dft_cos_sin(S)
    dl = jnp.asarray(np.concatenate([c, s], axis=1) / math.sqrt(S), F32).astype(BF16)
    return _bmm(dl, ust, BRANCH)


def _rope_tables(n_rows):
    half = HEAD_DIM // 4
    inv = ROPE_BASE ** (-np.arange(0, HEAD_DIM // 2, 2, dtype=np.float64) / (HEAD_DIM // 2))
    row = np.repeat(np.arange(n_rows, dtype=np.float64), GRID_W)
    col = np.tile(np.arange(GRID_W, dtype=np.float64), n_rows)
    ang = np.concatenate([row[:, None] * inv, col[:, None] * inv], axis=-1)
    assert ang.shape[1] == 2 * half
    cos, sin = np.cos(ang), np.sin(ang)
    cosf = np.tile(np.concatenate([cos, cos], axis=1), (1, LANE // HEAD_DIM))
    sins = np.tile(np.concatenate([-sin, sin], axis=1), (1, LANE // HEAD_DIM))
    return jnp.asarray(cosf, F32), jnp.asarray(sins, F32)


def _permute_w_in(w):
    D = w.shape[0]
    four, conv, gq, gk, gv, dq, dk, dv, gates = jnp.split(
        w, np.cumsum([BRANCH, 2 * BRANCH, BRANCH, LANE, LANE, BRANCH, BRANCH, BRANCH])[:], axis=1)
    pad = jnp.zeros((D, BRANCH - 2 * LANE), w.dtype)
    out = jnp.concatenate([four, conv, gq, dq, dk, dv, gk, gv, pad, gates], axis=1)
    assert out.shape[1] == ZC
    return out.astype(BF16)


def _mixer_branches(z, lw, w0, long_seq):
    f = _fourier_long(z, w0) if long_seq else _fourier_short(z, w0)
    cv = _conv_branch(z, lw["conv_w"], lw["conv_b"], lw["conv_ln_g"], lw["conv_ln_b"])
    return f, cv


def kernel(x, c, ctx, c_ctx, w_ada, b_ada, norm_mix, w_in, w_four, conv_w, conv_b, conv_ln_g, conv_ln_b, w_conv, q_norm, k_norm, w_gqa, lam_q1, lam_k1, lam_q2, lam_k2, diff_norm, w_diff, w_out, norm_ffn, w_ffn1, w_ffn3, w_ffn2, final_norm):
    B, S, D = x.shape
    Sc = ctx.shape[1]
    depth = w_ada.shape[0]
    assert B + 1 <= 8 and S % GRID_W == 0 and D == 2 * BRANCH

    cond = jnp.zeros((8, D), F32).at[:B].set(c).at[B].set(c_ctx)
    mods = _ada_mod(cond, w_ada, b_ada)

    rope_x = _rope_tables(S // GRID_W)
    rope_c = (jnp.ones((Sc, LANE), F32), jnp.zeros((Sc, LANE), F32))
    ones_bd = jnp.asarray(np.kron(np.eye(BRANCH // HEAD_DIM), np.ones((HEAD_DIM, HEAD_DIM))), F32).astype(BF16)
    w0 = jnp.asarray(_channel_dft_matrix(), F32).astype(BF16)
    tile2 = lambda v: jnp.tile(v, LANE // HEAD_DIM * (BRANCH // LANE)).reshape(1, BRANCH)

    for l in range(depth):
        last = l == depth - 1
        lam_init = 0.8 - 0.6 * math.exp(-0.3 * l)
        mx = [m.reshape(B, 1, D) for m in jnp.split(mods[l, :B], 6, axis=-1)]
        mc = [jnp.broadcast_to(m.reshape(1, 1, D), (B, 1, D)) for m in jnp.split(mods[l, B], 6, axis=-1)]
        lw = dict(conv_w=conv_w[l], conv_b=conv_b[l].reshape(1, BRANCH),
                  conv_ln_g=conv_ln_g[l].reshape(1, BRANCH), conv_ln_b=conv_ln_b[l].reshape(1, BRANCH),
                  q_norm=tile2(q_norm[l]), k_norm=tile2(k_norm[l]))
        w_in_l = _permute_w_in(w_in[l])
        w_lo, w_gate = w_in_l[:, :GATE_COL0], w_in_l[:, GATE_COL0:]
        wf, wc, wa, wd, wo = (w.astype(BF16) for w in (w_four[l], w_conv[l], w_gqa[l], w_diff[l], w_out[l]))
        w1, w3, w2 = (w.astype(BF16) for w in (w_ffn1[l], w_ffn3[l], w_ffn2[l]))
        gain_m, gain_f = norm_mix[l].reshape(1, D), norm_ffn[l].reshape(1, D)
        lam_vecs = [v[l].reshape(1, HEAD_DIM) for v in (lam_q1, lam_k1, lam_q2, lam_k2)]
        dn = diff_norm[l].reshape(1, LANE)
        fin = final_norm.reshape(1, D)

        zx, qgx, kg, vg, qdx, kd, vd = _inproj_qkv(x, gain_m, mx[0], mx[1], w_lo, rope_x[0], rope_x[1],
                                                   lw["q_norm"], lw["k_norm"], ones_bd, S + Sc, 0)
        zc, qgc, kg, vg, qdc, kd, vd = _inproj_qkv(ctx, gain_m, mc[0], mc[1], w_lo, rope_c[0], rope_c[1],
                                                   lw["q_norm"], lw["k_norm"], ones_bd, S + Sc, S,
                                                   kv_bufs=(kg, vg, kd, vd))
        zgx = _inproj(x, gain_m, mx[0], mx[1], w_gate)
        zgc = _inproj(ctx, gain_m, mc[0], mc[1], w_gate)
        fx, cvx = _mixer_branches(zx, lw, w0, True)
        fc, cvc = _mixer_branches(zc, lw, w0, False)

        ogx = _attention(qgx, kg, vg, "gqa", 0, S + Sc)
        odx = _attention(qdx, kd, vd, "diff", 0, S + Sc, lam_vecs, dn, lam_init)
        x = _merge(x, mx[2], fx, cvx, ogx, odx, zgx, wf, wc, wa, wd, wo)
        if not last:
            ogc = _attention(qgc, kg, vg, "gqa", S, Sc)
            odc = _attention(qdc, kd, vd, "diff", S, Sc, lam_vecs, dn, lam_init)
            ctx = _merge(ctx, mc[2], fc, cvc, ogc, odc, zgc, wf, wc, wa, wd, wo)
            ctx = _ffn(ctx, gain_f, mc[3], mc[4], mc[5], w1, w3, w2, fin, False)
        x = _ffn(x, gain_f, mx[3], mx[4], mx[5], w1, w3, w2, fin, last)
    return x
```

```python
import functools
import math

import numpy as np
import jax
import jax.numpy as jnp
from jax import lax
from jax.experimental import pallas as pl
from jax.experimental.pallas import tpu as pltpu

F32 = jnp.float32
BF16 = jnp.bfloat16

HEAD_DIM = 64
BRANCH = 512
GQA_GROUP = 4
GQA_KV_HEADS = 2
DIFF_VT_ROWS = 144
GQA_VT_ROWS = 80
DIFF_HEADS = 4
N_BRANCH = 4
GRID_W = 64
CONV_K = 31
ROPE_BASE = 10000.0
EPS = 1e-6
LN_EPS = 1e-5

LANE = 128
SUBLANE = 8
VMEM_LIMIT = 56 * 1024 * 1024

BLK_FOUR, BLK_CA, BLK_CG = range(3)
COL_GQ, COL_KV, COL_DQ, COL_DK, COL_DV, GATE_COL0 = 1536, 2048, 2304, 2816, 3328, 3840
FOURIER_NB = 16
CONV_HALO = 16


def _params(*sem):
    return pltpu.CompilerParams(dimension_semantics=sem, vmem_limit_bytes=VMEM_LIMIT)


def _silu(x):
    return x * jax.nn.sigmoid(x)


def _rms_mod(x, g, shift, scale):
    y = x * lax.rsqrt(jnp.mean(x * x, axis=-1, keepdims=True) + EPS) * g
    return y * (1.0 + scale) + shift


def _ada_kernel(c_ref, w_ref, b_ref, o_ref):
    a = _silu(c_ref[...])
    o_ref[...] = jnp.dot(a, w_ref[...], precision=lax.Precision.HIGHEST,
                         preferred_element_type=F32) + b_ref[...]


def _ada_mod(cond, w_ada, b_ada):
    L, D, N = w_ada.shape
    tn = 1536
    return pl.pallas_call(
        _ada_kernel,
        out_shape=jax.ShapeDtypeStruct((L, 8, N), F32),
        grid=(L, N // tn),
        in_specs=[pl.BlockSpec((8, D), lambda l, j: (0, 0)),
                  pl.BlockSpec((None, D, tn), lambda l, j: (l, 0, j)),
                  pl.BlockSpec((None, 1, tn), lambda l, j: (l, 0, j))],
        out_specs=pl.BlockSpec((None, 8, tn), lambda l, j: (l, 0, j)),
        compiler_params=_params("parallel", "parallel"),
        name="ada_mod",
    )(cond, w_ada, b_ada.reshape(L, 1, N))


def _inproj_kernel(x_ref, g_ref, sh_ref, sc_ref, w_ref, o_ref):
    h = _rms_mod(x_ref[...], g_ref[...], sh_ref[...], sc_ref[...])
    o_ref[...] = jnp.dot(h.astype(BF16), w_ref[...], preferred_element_type=F32).astype(o_ref.dtype)


def _inproj(x, gain, shift, scale, w):
    B, S, D = x.shape
    N = w.shape[1]
    tm = min(512, S)
    tn = 4096
    return pl.pallas_call(
        _inproj_kernel,
        out_shape=jax.ShapeDtypeStruct((B, S, N), BF16),
        grid=(N // tn, B, S // tm),
        in_specs=[pl.BlockSpec((None, tm, D), lambda j, b, i: (b, i, 0)),
                  pl.BlockSpec((1, D), lambda j, b, i: (0, 0)),
                  pl.BlockSpec((None, 1, D), lambda j, b, i: (b, 0, 0)),
                  pl.BlockSpec((None, 1, D), lambda j, b, i: (b, 0, 0)),
                  pl.BlockSpec((D, tn), lambda j, b, i: (0, j))],
        out_specs=pl.BlockSpec((None, tm, tn), lambda j, b, i: (b, i, j)),
        compiler_params=_params("parallel", "parallel", "parallel"),
        name="in_proj",
    )(x, gain, shift, scale, w)


def _inproj_qkv_kernel(x_ref, g_ref, sh_ref, sc_ref, wz_ref, wa_ref, cos_ref, sin_ref, qn_ref, kn_ref, bd_ref, *rest):
    z_ref, qg_ref, kg_ref, vg_ref, qd_ref, kd_ref, vd_ref = rest[-7:]
    h = _rms_mod(x_ref[...], g_ref[...], sh_ref[...], sc_ref[...]).astype(BF16)
    acc = jnp.dot(h, wa_ref[...], preferred_element_type=F32)
    slab = lambda col, width=BRANCH: acc[:, col - COL_GQ:col - COL_GQ + width]
    _qkv_store(slab(COL_GQ), slab(COL_DQ), slab(COL_DK), slab(COL_DV), slab(COL_KV, 2 * LANE),
               cos_ref[...], sin_ref[...], qn_ref[...], kn_ref[...], bd_ref[...],
               qg_ref, kg_ref, vg_ref, qd_ref, kd_ref, vd_ref)
    z_ref[...] = jnp.dot(h, wz_ref[...], preferred_element_type=F32).astype(z_ref.dtype)


def _inproj_qkv(x, gain, shift, scale, w_z, w_att, cosf, sins, qn, kn, ones_bd, n_keys, key_start, kv_bufs=None):
    B, S, D = x.shape
    tm = min(512, S)
    assert key_start % tm == 0
    k0 = key_start // tm
    const = lambda shape: pl.BlockSpec(shape, lambda b, i: (0, 0))
    per_b = pl.BlockSpec((None, 1, D), lambda b, i: (b, 0, 0))
    tab = pl.BlockSpec((tm, LANE), lambda b, i: (i, 0))
    rows = lambda width: pl.BlockSpec((None, tm, width), lambda b, i: (b, i, 0))
    q_cols = lambda chans: pl.BlockSpec((None, chans, tm), lambda b, i: (b, 0, i))
    k_rows = lambda width: pl.BlockSpec((None, tm, width), lambda b, i: (b, k0 + i, 0))
    v_cols = lambda chans: pl.BlockSpec((None, chans, tm), lambda b, i: (b, 0, k0 + i))
    vg_rows, vd_rows = GQA_KV_HEADS * GQA_VT_ROWS, DIFF_HEADS * DIFF_VT_ROWS
    shapes = [(B, S, 3 * BRANCH), (B, 2 * BRANCH, S), (B, n_keys, LANE), (B, vg_rows, n_keys),
              (B, 2 * BRANCH, S), (B, n_keys, BRANCH), (B, vd_rows, n_keys)]
    specs = [rows(3 * BRANCH), q_cols(2 * BRANCH), k_rows(LANE), v_cols(vg_rows),
             q_cols(2 * BRANCH), k_rows(BRANCH), v_cols(vd_rows)]
    in_specs = [rows(D), const((1, D)), per_b, per_b, const(w_z.shape), const(w_att.shape), tab, tab,
                const((1, BRANCH)), const((1, BRANCH)), const((BRANCH, BRANCH))]
    args = [x, gain, shift, scale, w_z, w_att, cosf, sins, qn, kn, ones_bd]
    aliases = {}
    if kv_bufs is not None:
        for out_idx, buf in zip((2, 3, 5, 6), kv_bufs):
            aliases[len(args)] = out_idx
            in_specs.append(pl.BlockSpec(memory_space=pl.ANY))
            args.append(buf)
    return pl.pallas_call(
        _inproj_qkv_kernel,
        out_shape=[jax.ShapeDtypeStruct(s, BF16) for s in shapes],
        grid=(B, S // tm),
        in_specs=in_specs,
        out_specs=specs,
        input_output_aliases=aliases,
        compiler_params=_params("parallel", "parallel"),
        name="in_proj_qkv",
    )(*args)


def _ffn_kernel(x_ref, g_ref, sh_ref, sc_ref, gate_ref, w1_ref, w3_ref, w2_ref, fg_ref, o_ref, *, final):
    x = x_ref[...]
    h = _rms_mod(x, g_ref[...], sh_ref[...], sc_ref[...]).astype(BF16)
    a = jnp.dot(h, w1_ref[...], preferred_element_type=F32)
    b = jnp.dot(h, w3_ref[...], preferred_element_type=F32)
    u = (_silu(a) * b).astype(BF16)
    y = x + gate_ref[...] * jnp.dot(u, w2_ref[...], preferred_element_type=F32)
    if final:
        y = y * lax.rsqrt(jnp.mean(y * y, axis=-1, keepdims=True) + EPS) * fg_ref[...]
    o_ref[...] = y


def _ffn(x, gain, shift, scale, gate, w1, w3, w2, final_gain, final):
    B, S, D = x.shape
    H = w1.shape[1]
    tm = min(512, S)
    const = lambda shape: pl.BlockSpec(shape, lambda b, i: (0,) * len(shape), pipeline_mode=pl.Buffered(1))
    per_b = pl.BlockSpec((None, 1, D), lambda b, i: (b, 0, 0))
    return pl.pallas_call(
        functools.partial(_ffn_kernel, final=final),
        out_shape=jax.ShapeDtypeStruct((B, S, D), F32),
        grid=(B, S // tm),
        in_specs=[pl.BlockSpec((None, tm, D), lambda b, i: (b, i, 0)),
                  const((1, D)), per_b, per_b, per_b,
                  const((D, H)), const((D, H)), const((H, D)), const((1, D))],
        out_specs=pl.BlockSpec((None, tm, D), lambda b, i: (b, i, 0)),
        compiler_params=_params("parallel", "parallel"),
        name="ffn",
    )(x, gain, shift, scale, gate, w1, w3, w2, final_gain)


def _merge_kernel(x_ref, gm_ref, f_ref, c_ref, a_ref, d_ref, zg_ref,
                  wf_ref, wc_ref, wa_ref, wd_ref, wo_ref, o_ref):
    D = x_ref.shape[-1]
    acc = None
    for b, (br, w) in enumerate(((f_ref, wf_ref), (c_ref, wc_ref), (a_ref, wa_ref), (d_ref, wd_ref))):
        y = jnp.dot(br[...], w[...], preferred_element_type=F32)
        gate = jax.nn.sigmoid(zg_ref[:, b * D:(b + 1) * D].astype(F32))
        acc = gate * y if acc is None else acc + gate * y
    out = jnp.dot(acc.astype(BF16), wo_ref[...], preferred_element_type=F32)
    o_ref[...] = x_ref[...] + gm_ref[...] * out


def _merge(x, gm, f, cv, og, od, zg, wf, wc, wa, wd, wo):
    B, S, D = x.shape
    tm = min(512, S)
    row = lambda width: pl.BlockSpec((None, tm, width), lambda b, i: (b, i, 0))
    const = lambda shape: pl.BlockSpec(shape, lambda b, i: (0, 0))
    return pl.pallas_call(
        _merge_kernel,
        out_shape=jax.ShapeDtypeStruct((B, S, D), F32),
        grid=(B, S // tm),
        in_specs=[row(D), pl.BlockSpec((None, 1, D), lambda b, i: (b, 0, 0)),
                  row(BRANCH), row(BRANCH), row(BRANCH), row(BRANCH),
                  pl.BlockSpec((None, tm, N_BRANCH * D), lambda b, i: (b, i, 0)),
                  const((BRANCH, D)), const((BRANCH, D)), const((BRANCH, D)), const((BRANCH, D)),
                  const((D, D))],
        out_specs=row(D),
        compiler_params=_params("parallel", "parallel"),
        name="merge",
    )(x, gm, f, cv, og, od, zg, wf, wc, wa, wd, wo)


def _conv_kernel(a_ref, g_ref, ap_ref, gp_ref, an_ref, gn_ref, w_ref, b_ref, lg_ref, lb_ref, o_ref, u_s,
                 *, ts, chunk):
    i = pl.program_id(1)
    n = pl.num_programs(1)

    def glu(a, g):
        return a.astype(F32) * jax.nn.sigmoid(g.astype(F32))

    u_s[CONV_HALO:CONV_HALO + ts, :] = glu(a_ref[...], g_ref[...])
    u_s[0:CONV_HALO, :] = jnp.where(i > 0, glu(ap_ref[...], gp_ref[...]), 0.0)
    u_s[CONV_HALO + ts:2 * CONV_HALO + ts, :] = jnp.where(i < n - 1, glu(an_ref[...], gn_ref[...]), 0.0)
    pad = CONV_K // 2

    def body(c, carry):
        r0 = pl.multiple_of(c * chunk, chunk)
        win = u_s[pl.ds(r0, chunk + 2 * CONV_HALO), :]
        rows = chunk + 2 * CONV_HALO
        acc = jnp.zeros((chunk // SUBLANE, SUBLANE, BRANCH), F32) + b_ref[...]
        for r in range(SUBLANE):
            taps = [j for j in range(CONV_K) if (CONV_HALO - pad + j) % SUBLANE == r]
            shifted = win if r == 0 else pltpu.roll(win, rows - r, 0)
            for j in taps:
                base = CONV_HALO - pad + j - r
                tap = shifted[base:base + chunk, :].reshape(chunk // SUBLANE, SUBLANE, BRANCH)
                acc = acc + w_ref[j] * tap
        acc = acc.reshape(chunk, BRANCH)
        mu = jnp.mean(acc, axis=-1, keepdims=True)
        d = acc - mu
        var = jnp.mean(d * d, axis=-1, keepdims=True)
        y = d * lax.rsqrt(var + LN_EPS) * lg_ref[...] + lb_ref[...]
        o_ref[pl.ds(r0, chunk), :] = _silu(y).astype(o_ref.dtype)
        return carry

    lax.fori_loop(0, ts // chunk, body, 0)


def _conv_branch(z, w, b, ln_g, ln_b):
    B, S, _ = z.shape
    ts = min(512, S)
    chunk = min(256, ts)
    hb = ts // CONV_HALO
    last = S // CONV_HALO - 1
    cur = lambda blk: pl.BlockSpec((None, ts, BRANCH), lambda b_, i: (b_, i, blk))
    prev = lambda blk: pl.BlockSpec((None, CONV_HALO, BRANCH),
                                    lambda b_, i: (b_, jnp.maximum(i * hb - 1, 0), blk))
    nxt = lambda blk: pl.BlockSpec((None, CONV_HALO, BRANCH),
                                   lambda b_, i: (b_, jnp.minimum((i + 1) * hb, last), blk))
    const = lambda shape: pl.BlockSpec(shape, lambda b_, i: (0, 0))
    return pl.pallas_call(
        functools.partial(_conv_kernel, ts=ts, chunk=chunk),
        out_shape=jax.ShapeDtypeStruct((B, S, BRANCH), BF16),
        grid=(B, S // ts),
        in_specs=[cur(BLK_CA), cur(BLK_CG), prev(BLK_CA), prev(BLK_CG), nxt(BLK_CA), nxt(BLK_CG),
                  pl.BlockSpec((CONV_K, SUBLANE, BRANCH), lambda b_, i: (0, 0, 0)),
                  const((1, BRANCH)), const((1, BRANCH)), const((1, BRANCH))],
        out_specs=pl.BlockSpec((None, ts, BRANCH), lambda b_, i: (b_, i, 0)),
        scratch_shapes=[pltpu.VMEM((ts + 2 * CONV_HALO, BRANCH), F32)],
        compiler_params=_params("parallel", "parallel"),
        name="conv_branch",
    )(z, z, z, z, z, z, jnp.broadcast_to(w[:, None, :], (CONV_K, SUBLANE, BRANCH)), b, ln_g, ln_b)


def _head_rms(x, ones_bd):
    x2 = x * x
    hi = x2.astype(BF16)
    lo = (x2 - hi.astype(F32)).astype(BF16)
    ssum = (jnp.dot(hi, ones_bd, preferred_element_type=F32)
            + jnp.dot(lo, ones_bd, preferred_element_type=F32))
    return x * lax.rsqrt(ssum * (1.0 / HEAD_DIM) + EPS)


def _rope(x, cosf, sins):
    width = x.shape[1]
    reps = width // LANE
    c = jnp.concatenate([cosf] * reps, axis=1) if reps > 1 else cosf
    s = jnp.concatenate([sins] * reps, axis=1) if reps > 1 else sins
    lane = lax.broadcasted_iota(jnp.int32, x.shape, 1)
    first_half = (lane & (HEAD_DIM - 1)) < HEAD_DIM // 2
    partner = jnp.where(first_half, pltpu.roll(x, width - HEAD_DIM // 2, 1), pltpu.roll(x, HEAD_DIM // 2, 1))
    return x * c + partner * s


def _qkv_store(gq, dq, dk, dv, kvz, cosf, sins, qn, kn, bd, qg_ref, kg_ref, vg_ref, qd_ref, kd_ref, vd_ref):
    scale = HEAD_DIM ** -0.5 * math.log2(math.e)
    ts = gq.shape[0]

    half = jnp.zeros((HEAD_DIM, ts), qg_ref.dtype)
    q = _rope(_head_rms(gq, bd) * qn, cosf, sins) * scale
    for p in range(BRANCH // LANE):
        qt = q[:, p * LANE:(p + 1) * LANE].T
        for e in range(2):
            h = 2 * p + e
            kvh = h // GQA_GROUP
            base = h * LANE
            qg_ref[base + kvh * HEAD_DIM:base + (kvh + 1) * HEAD_DIM, :] = (
                qt[e * HEAD_DIM:(e + 1) * HEAD_DIM, :].astype(qg_ref.dtype))
            qg_ref[base + (1 - kvh) * HEAD_DIM:base + (2 - kvh) * HEAD_DIM, :] = half

    k = _rope(_head_rms(kvz[:, :LANE], bd[:LANE, :LANE]) * kn[:, :LANE], cosf, sins)
    kg_ref[...] = k.astype(kg_ref.dtype)
    vt = kvz[:, LANE:2 * LANE].T
    row = lax.broadcasted_iota(jnp.int32, (GQA_VT_ROWS - HEAD_DIM, ts), 0)
    tail = jnp.where(row == 0, 1.0, 0.0).astype(vg_ref.dtype)
    for g in range(GQA_KV_HEADS):
        vg_ref[g * GQA_VT_ROWS:g * GQA_VT_ROWS + HEAD_DIM, :] = vt[g * HEAD_DIM:(g + 1) * HEAD_DIM, :].astype(vg_ref.dtype)
        vg_ref[g * GQA_VT_ROWS + HEAD_DIM:(g + 1) * GQA_VT_ROWS, :] = tail

    qd = _rope(dq, cosf, sins) * scale
    for h in range(DIFF_HEADS):
        qt = qd[:, h * LANE:(h + 1) * LANE].T.astype(qd_ref.dtype)
        base = 2 * h * LANE
        qd_ref[base:base + HEAD_DIM, :] = qt[:HEAD_DIM, :]
        qd_ref[base + HEAD_DIM:base + LANE, :] = half
        qd_ref[base + LANE:base + LANE + HEAD_DIM, :] = half
        qd_ref[base + LANE + HEAD_DIM:base + 2 * LANE, :] = qt[HEAD_DIM:, :]
        vd_ref[h * DIFF_VT_ROWS:h * DIFF_VT_ROWS + LANE, :] = (
            dv[:, h * LANE:(h + 1) * LANE].T.astype(vd_ref.dtype))
        vd_ref[h * DIFF_VT_ROWS + LANE:(h + 1) * DIFF_VT_ROWS, :] = tail
    kd_ref[...] = _rope(dk, cosf, sins).astype(kd_ref.dtype)


def _attn_kernel(*refs, sets, tq, tk, mode, lam_init):
    if mode == "diff":
        q_ref, k_ref, v_ref, l1q, l1k, l2q, l2k, dn_ref, o_ref, q_s, acc_s = refs
    else:
        q_ref, k_ref, v_ref, o_ref, q_s, acc_s = refs
    n_tiles = k_ref.shape[0] // tk
    width = sets * tq
    for r in range(sets):
        q_s[:, r * tq:(r + 1) * tq] = q_ref[r * LANE:(r + 1) * LANE, :]

    def finish():
        if mode == "diff":
            ot = acc_s[:LANE, :] / acc_s[LANE:LANE + 1, :]
            o = [ot[:, r * tq:(r + 1) * tq].T for r in range(sets)]
            lam = (jnp.exp(jnp.sum(l1q[...] * l1k[...], axis=1, keepdims=True))
                   - jnp.exp(jnp.sum(l2q[...] * l2k[...], axis=1, keepdims=True)) + lam_init)
            d = o[0] - lam * o[1]
            d = d * lax.rsqrt(jnp.mean(d * d, axis=1, keepdims=True) + EPS) * dn_ref[...]
            o_ref[...] = (d * (1.0 - lam_init)).astype(o_ref.dtype)
        else:
            ot = acc_s[:HEAD_DIM, :] / acc_s[HEAD_DIM:HEAD_DIM + 1, :]
            for pair in range(sets // 2):
                both = jnp.concatenate([ot[:, (2 * pair) * tq:(2 * pair + 1) * tq],
                                        ot[:, (2 * pair + 1) * tq:(2 * pair + 2) * tq]], axis=0)
                o_ref[:, pair * LANE:(pair + 1) * LANE] = both.T.astype(o_ref.dtype)

    m_run = None
    for t in range(n_tiles):
        st = jnp.dot(k_ref[t * tk:(t + 1) * tk, :], q_s[...], preferred_element_type=F32)
        m_tile = jnp.max(st, axis=0, keepdims=True)
        pt = jnp.exp2(st - (m_tile if t == 0 else m_run))
        pv = jnp.dot(v_ref[:, t * tk:(t + 1) * tk], pt.astype(BF16), preferred_element_type=F32)
        if t == 0:
            m_run = m_tile
            acc_s[...] = pv
        else:
            m_new = jnp.maximum(m_run, m_tile)
            alpha = jnp.exp2(m_run - m_new)
            acc_s[...] = (acc_s[...] + pv) * alpha
            m_run = m_new
    finish()
    overflowed = jnp.max(jnp.where(jnp.isfinite(acc_s[...]), 0.0, 1.0)) > 0.0

    @pl.when(overflowed)
    def _exact():
        def body(t, m_old):
            st = jnp.dot(k_ref[pl.ds(pl.multiple_of(t * tk, tk), tk), :], q_s[...], preferred_element_type=F32)
            m_new = jnp.maximum(m_old, jnp.max(st, axis=0, keepdims=True))
            alpha = jnp.exp2(m_old - m_new)
            pt = jnp.exp2(st - m_new)
            v_tile = v_ref[:, pl.ds(pl.multiple_of(t * tk, LANE), tk)]
            acc_s[...] = alpha * acc_s[...] + jnp.dot(v_tile, pt.astype(BF16), preferred_element_type=F32)
            return m_new

        acc_s[...] = jnp.zeros(acc_s.shape, F32)
        lax.fori_loop(0, n_tiles, body, jnp.full((1, width), -jnp.inf, F32))
        finish()


def _attention(qt, k, vt, mode, key_start, NK, lam_vecs=None, diff_norm=None, lam_init=0.0):
    B, _, S = qt.shape
    assert key_start % NK == 0
    kb = key_start // NK
    tq = min(512 if mode == "diff" else 256, S)
    tk = 768
    tk = tk if NK % tk == 0 else 256
    if mode == "diff":
        groups, sets, out_w, vrows = DIFF_HEADS, 2, LANE, DIFF_VT_ROWS
        kmap = lambda b, g, i: (b, kb, g)
    else:
        groups, sets, out_w, vrows = GQA_KV_HEADS, GQA_GROUP, GQA_GROUP * HEAD_DIM, GQA_VT_ROWS
        kmap = lambda b, g, i: (b, kb, 0)
    in_specs = [pl.BlockSpec((None, sets * LANE, tq), lambda b, g, i: (b, g, i)),
                pl.BlockSpec((None, NK, LANE), kmap),
                pl.BlockSpec((None, vrows, NK), lambda b, g, i: (b, g, kb))]
    args = [qt, k, vt]
    if mode == "diff":
        in_specs += [pl.BlockSpec((1, HEAD_DIM), lambda b, g, i: (0, 0))] * 4
        in_specs += [pl.BlockSpec((1, LANE), lambda b, g, i: (0, 0))]
        args += list(lam_vecs) + [diff_norm]
    return pl.pallas_call(
        functools.partial(_attn_kernel, sets=sets, tq=tq, tk=tk, mode=mode, lam_init=lam_init),
        out_shape=jax.ShapeDtypeStruct((B, S, BRANCH), BF16),
        grid=(B, groups, S // tq),
        in_specs=in_specs,
        out_specs=pl.BlockSpec((None, tq, out_w), lambda b, g, i: (b, i, g)),
        scratch_shapes=[pltpu.VMEM((LANE, sets * tq), BF16),
                        pltpu.VMEM((vrows, sets * tq), F32)],
        compiler_params=_params("parallel", "parallel", "parallel"),
        name="attn_" + mode,
    )(*args)


def _dft_cos_sin(n):
    idx = np.arange(n)
    ang = 2.0 * np.pi * ((idx[:, None] * idx[None, :]) % n) / n
    return np.cos(ang), np.sin(ang)


def _channel_dft_matrix():
    c, s = _dft_cos_sin(GRID_W)
    groups = BRANCH // GRID_W
    eye = np.eye(groups)
    return np.concatenate([np.kron(eye, c), -np.kron(eye, s)], axis=1) / math.sqrt(GRID_W)


def _fourier_a_kernel(z_ref, w0_ref, d1_ref, tc_ref, ts_ref, o_ref):
    n1, nb, _ = z_ref.shape
    u = jnp.dot(z_ref[...].reshape(n1 * nb, BRANCH), w0_ref[...], preferred_element_type=F32)
    u = pltpu.einshape("kjm->jkm", u.reshape(n1, nb, 2 * BRANCH)).astype(BF16)
    reps = BRANCH // LANE
    out_r, out_i = [], []
    for j in range(nb):
        p = jnp.dot(d1_ref[...], u[j], preferred_element_type=F32)
        vr = p[:n1, :BRANCH] + p[n1:, BRANCH:]
        vi = p[:n1, BRANCH:] - p[n1:, :BRANCH]
        c = jnp.concatenate([tc_ref[j]] * reps, axis=1)
        s = jnp.concatenate([ts_ref[j]] * reps, axis=1)
        out_r.append(vr * c + vi * s)
        out_i.append(vi * c - vr * s)
    o_ref[0] = pltpu.einshape("jkm->kjm", jnp.stack(out_r)).astype(o_ref.dtype)
    o_ref[1] = pltpu.einshape("jkm->kjm", jnp.stack(out_i)).astype(o_ref.dtype)


def _fourier_b_kernel(v_ref, c3_ref, s3_ref, o_ref):
    nb = v_ref.shape[1]
    outs = [jnp.dot(c3_ref[...], v_ref[0, j], preferred_element_type=F32)
            + jnp.dot(s3_ref[...], v_ref[1, j], preferred_element_type=F32) for j in range(nb)]
    o_ref[...] = pltpu.einshape("jkm->kjm", jnp.stack(outs)).astype(o_ref.dtype)


def _bmm_kernel(a_ref, x_ref, o_ref):
    o_ref[...] = jnp.dot(a_ref[...], x_ref[...], preferred_element_type=F32).astype(o_ref.dtype)


def _bmm(a, x, tn):
    M, K = a.shape
    B, _, N = x.shape
    tn = min(tn, N)
    return pl.pallas_call(
        _bmm_kernel,
        out_shape=jax.ShapeDtypeStruct((B, M, N), BF16),
        grid=(B, N // tn),
        in_specs=[pl.BlockSpec((M, K), lambda b, j: (0, 0)),
                  pl.BlockSpec((None, K, tn), lambda b, j: (b, 0, j))],
        out_specs=pl.BlockSpec((None, M, tn), lambda b, j: (b, 0, j)),
        compiler_params=_params("parallel", "parallel"),
        name="bmm",
    )(a, x)


def _slab_matmul_kernel(z_ref, w_ref, o_ref):
    o_ref[...] = jnp.dot(z_ref[...], w_ref[...], preferred_element_type=F32).astype(o_ref.dtype)


def _slab_matmul(z, blk, w):
    B, S, _ = z.shape
    K, N = w.shape
    ts = min(256, S)
    return pl.pallas_call(
        _slab_matmul_kernel,
        out_shape=jax.ShapeDtypeStruct((B, S, N), BF16),
        grid=(B, S // ts),
        in_specs=[pl.BlockSpec((None, ts, K), lambda b, i: (b, i, blk)),
                  pl.BlockSpec((K, N), lambda b, i: (0, 0))],
        out_specs=pl.BlockSpec((None, ts, N), lambda b, i: (b, i, 0)),
        compiler_params=_params("parallel", "parallel"),
        name="slab_matmul",
    )(z, w)


def _fourier_long(z, w0):
    B, S, _ = z.shape
    n2 = GRID_W
    n1 = S // n2
    nb = FOURIER_NB
    assert n1 % nb == 0 and n2 % nb == 0
    c1, s1 = _dft_cos_sin(n1)
    d1 = jnp.asarray(np.concatenate([c1, s1], axis=0) / math.sqrt(n1), F32).astype(BF16)
    ang = 2.0 * np.pi * (np.arange(n2)[:, None] * np.arange(n1)[None, :]) / S
    twc = jnp.asarray(np.repeat(np.cos(ang)[:, :, None], LANE, axis=2), F32)
    tws = jnp.asarray(np.repeat(np.sin(ang)[:, :, None], LANE, axis=2), F32)
    v = pl.pallas_call(
        _fourier_a_kernel,
        out_shape=jax.ShapeDtypeStruct((B, 2, n1, n2, BRANCH), BF16),
        grid=(B, n2 // nb),
        in_specs=[pl.BlockSpec((None, n1, nb, BRANCH), lambda b, j: (b, 0, j, BLK_FOUR)),
                  pl.BlockSpec((BRANCH, 2 * BRANCH), lambda b, j: (0, 0)),
                  pl.BlockSpec((2 * n1, n1), lambda b, j: (0, 0)),
                  pl.BlockSpec((nb, n1, LANE), lambda b, j: (j, 0, 0)),
                  pl.BlockSpec((nb, n1, LANE), lambda b, j: (j, 0, 0))],
        out_specs=pl.BlockSpec((None, 2, n1, nb, BRANCH), lambda b, j: (b, 0, 0, j, 0)),
        compiler_params=_params("parallel", "parallel"),
        name="fourier_a",
    )(z.reshape(B, n1, n2, z.shape[-1]), w0, d1, twc, tws)
    c3, s3 = _dft_cos_sin(n2)
    c3 = jnp.asarray(c3 / math.sqrt(n2), F32).astype(BF16)
    s3 = jnp.asarray(s3 / math.sqrt(n2), F32).astype(BF16)
    f = pl.pallas_call(
        _fourier_b_kernel,
        out_shape=jax.ShapeDtypeStruct((B, n2, n1, BRANCH), BF16),
        grid=(B, n1 // nb),
        in_specs=[pl.BlockSpec((None, 2, nb, n2, BRANCH), lambda b, i: (b, 0, i, 0, 0)),
                  pl.BlockSpec((n2, n2), lambda b, i: (0, 0)),
                  pl.BlockSpec((n2, n2), lambda b, i: (0, 0))],
        out_specs=pl.BlockSpec((None, n2, nb, BRANCH), lambda b, i: (b, 0, i, 0)),
        compiler_params=_params("parallel", "parallel"),
        name="fourier_b",
    )(v, c3, s3)
    return f.reshape(B, S, BRANCH)


def _fourier_short(z, w0):
    B, S, _ = z.shape
    u = _slab_matmul(z, BLK_FOUR, w0)
    ust = jnp.concatenate([u[:, :, :BRANCH], u[:, :, BRANCH:]], axis=1)
    c, s = _dft_cos_sin(S)
    dl = jnp.asarray(np.concatenate([c, s], axis=1) / math.sqrt(S), F32).astype(BF16)
    return _bmm(dl, ust, BRANCH)


def _rope_tables(n_rows):
    half = HEAD_DIM // 4
    inv = ROPE_BASE ** (-np.arange(0, HEAD_DIM // 2, 2, dtype=np.float64) / (HEAD_DIM // 2))
    row = np.repeat(np.arange(n_rows, dtype=np.float64), GRID_W)
    col = np.tile(np.arange(GRID_W, dtype=np.float64), n_rows)
    ang = np.concatenate([row[:, None] * inv, col[:, None] * inv], axis=-1)
    assert ang.shape[1] == 2 * half
    cos, sin = np.cos(ang), np.sin(ang)
    cosf = np.tile(np.concatenate([cos, cos], axis=1), (1, LANE // HEAD_DIM))
    sins = np.tile(np.concatenate([-sin, sin], axis=1), (1, LANE // HEAD_DIM))
    return jnp.asarray(cosf, F32), jnp.asarray(sins, F32)


def _mixer_branches(z, lw, w0, long_seq):
    f = _fourier_long(z, w0) if long_seq else _fourier_short(z, w0)
    cv = _conv_branch(z, lw["conv_w"], lw["conv_b"], lw["conv_ln_g"], lw["conv_ln_b"])
    return f, cv


def kernel(x, c, ctx, c_ctx, w_ada, b_ada, norm_mix, w_in, w_four, conv_w, conv_b, conv_ln_g, conv_ln_b, w_conv, q_norm, k_norm, w_gqa, lam_q1, lam_k1, lam_q2, lam_k2, diff_norm, w_diff, w_out, norm_ffn, w_ffn1, w_ffn3, w_ffn2, final_norm):
    B, S, D = x.shape
    Sc = ctx.shape[1]
    depth = w_ada.shape[0]
    assert B + 1 <= 8 and S % GRID_W == 0 and D == 2 * BRANCH

    cond = jnp.zeros((8, D), F32).at[:B].set(c).at[B].set(c_ctx)
    mods = _ada_mod(cond, w_ada, b_ada)

    rope_x = _rope_tables(S // GRID_W)
    rope_c = (jnp.ones((Sc, LANE), F32), jnp.zeros((Sc, LANE), F32))
    ones_bd = jnp.asarray(np.kron(np.eye(BRANCH // HEAD_DIM), np.ones((HEAD_DIM, HEAD_DIM))), F32).astype(BF16)
    w0 = jnp.asarray(_channel_dft_matrix(), F32).astype(BF16)
    tile2 = lambda v: jnp.tile(v, LANE // HEAD_DIM * (BRANCH // LANE)).reshape(1, BRANCH)

    for l in range(depth):
        last = l == depth - 1
        lam_init = 0.8 - 0.6 * math.exp(-0.3 * l)
        mx = [m.reshape(B, 1, D) for m in jnp.split(mods[l, :B], 6, axis=-1)]
        mc = [jnp.broadcast_to(m.reshape(1, 1, D), (B, 1, D)) for m in jnp.split(mods[l, B], 6, axis=-1)]
        lw = dict(conv_w=conv_w[l], conv_b=conv_b[l].reshape(1, BRANCH),
                  conv_ln_g=conv_ln_g[l].reshape(1, BRANCH), conv_ln_b=conv_ln_b[l].reshape(1, BRANCH),
                  q_norm=tile2(q_norm[l]), k_norm=tile2(k_norm[l]))
        w_z, w_att, w_gate = (w_in[l][:, a:b].astype(BF16) for a, b in ((0, COL_GQ), (COL_GQ, GATE_COL0), (GATE_COL0, None)))
        wf, wc, wa, wd, wo = (w.astype(BF16) for w in (w_four[l], w_conv[l], w_gqa[l], w_diff[l], w_out[l]))
        w1, w3, w2 = (w.astype(BF16) for w in (w_ffn1[l], w_ffn3[l], w_ffn2[l]))
        gain_m, gain_f = norm_mix[l].reshape(1, D), norm_ffn[l].reshape(1, D)
        lam_vecs = [v[l].reshape(1, HEAD_DIM) for v in (lam_q1, lam_k1, lam_q2, lam_k2)]
        dn = diff_norm[l].reshape(1, LANE)
        fin = final_norm.reshape(1, D)

        zx, qgx, kg, vg, qdx, kd, vd = _inproj_qkv(x, gain_m, mx[0], mx[1], w_z, w_att, rope_x[0], rope_x[1],
                                                   lw["q_norm"], lw["k_norm"], ones_bd, S + Sc, 0)
        zc, qgc, kg, vg, qdc, kd, vd = _inproj_qkv(ctx, gain_m, mc[0], mc[1], w_z, w_att, rope_c[0], rope_c[1],
                                                   lw["q_norm"], lw["k_norm"], ones_bd, S + Sc, S,
                                                   kv_bufs=(kg, vg, kd, vd))
        zgx = _inproj(x, gain_m, mx[0], mx[1], w_gate)
        zgc = _inproj(ctx, gain_m, mc[0], mc[1], w_gate)
        fx, cvx = _mixer_branches(zx, lw, w0, True)
        fc, cvc = _mixer_branches(zc, lw, w0, False)

        ogx = _attention(qgx, kg, vg, "gqa", 0, S + Sc)
        odx = _attention(qdx, kd, vd, "diff", 0, S + Sc, lam_vecs, dn, lam_init)
        x = _merge(x, mx[2], fx, cvx, ogx, odx, zgx, wf, wc, wa, wd, wo)
        if not last:
            ogc = _attention(qgc, kg, vg, "gqa", S, Sc)
            odc = _attention(qdc, kd, vd, "diff", S, Sc, lam_vecs, dn, lam_init)
            ctx = _merge(ctx, mc[2], fc, cvc, ogc, odc, zgc, wf, wc, wa, wd, wo)
            ctx = _ffn(ctx, gain_f, mc[3], mc[4], mc[5], w1, w3, w2, fin, False)
        x = _ffn(x, gain_f, mx[3], mx[4], mx[5], w1, w3, w2, fin, last)
    return x
```

```python
import functools
import math

import numpy as np
import jax
import jax.numpy as jnp
from jax import lax
from jax.experimental import pallas as pl
from jax.experimental.pallas import tpu as pltpu

F32 = jnp.float32
BF16 = jnp.bfloat16

HEAD_DIM = 64
BRANCH = 512
GQA_GROUP = 4
GQA_KV_HEADS = 2
DIFF_VT_ROWS = 144
GQA_VT_ROWS = 80
DIFF_HEADS = 4
N_BRANCH = 4
GRID_W = 64
CONV_K = 31
ROPE_BASE = 10000.0
EPS = 1e-6
LN_EPS = 1e-5

LANE = 128
SUBLANE = 8
VMEM_LIMIT = 56 * 1024 * 1024

BLK_FOUR, BLK_CA, BLK_CG = range(3)
COL_GQ, COL_KV, COL_DQ, COL_DK, COL_DV, GATE_COL0 = 1536, 2048, 2304, 2816, 3328, 3840
FOURIER_NB = 16
CONV_HALO = 16


def _params(*sem):
    return pltpu.CompilerParams(dimension_semantics=sem, vmem_limit_bytes=VMEM_LIMIT)


def _silu(x):
    return x * jax.nn.sigmoid(x)


def _rms_mod(x, g, shift, scale):
    y = x * lax.rsqrt(jnp.mean(x * x, axis=-1, keepdims=True) + EPS) * g
    return y * (1.0 + scale) + shift


def _ada_kernel(c_ref, w_ref, b_ref, o_ref):
    a = _silu(c_ref[...])
    o_ref[...] = jnp.dot(a, w_ref[...], precision=lax.Precision.HIGHEST,
                         preferred_element_type=F32) + b_ref[...]


def _ada_mod(cond, w_ada, b_ada):
    L, D, N = w_ada.shape
    tn = 1536
    return pl.pallas_call(
        _ada_kernel,
        out_shape=jax.ShapeDtypeStruct((L, 8, N), F32),
        grid=(L, N // tn),
        in_specs=[pl.BlockSpec((8, D), lambda l, j: (0, 0)),
                  pl.BlockSpec((None, D, tn), lambda l, j: (l, 0, j)),
                  pl.BlockSpec((None, 1, tn), lambda l, j: (l, 0, j))],
        out_specs=pl.BlockSpec((None, 8, tn), lambda l, j: (l, 0, j)),
        compiler_params=_params("parallel", "parallel"),
        name="ada_mod",
    )(cond, w_ada, b_ada.reshape(L, 1, N))


def _inproj_kernel(x_ref, g_ref, sh_ref, sc_ref, w_ref, o_ref):
    h = _rms_mod(x_ref[...], g_ref[...], sh_ref[...], sc_ref[...])
    o_ref[...] = jnp.dot(h.astype(BF16), w_ref[...], preferred_element_type=F32).astype(o_ref.dtype)


def _inproj(x, gain, shift, scale, w):
    B, S, D = x.shape
    N = w.shape[1]
    tm = min(512, S)
    tn = 4096
    return pl.pallas_call(
        _inproj_kernel,
        out_shape=jax.ShapeDtypeStruct((B, S, N), BF16),
        grid=(N // tn, B, S // tm),
        in_specs=[pl.BlockSpec((None, tm, D), lambda j, b, i: (b, i, 0)),
                  pl.BlockSpec((1, D), lambda j, b, i: (0, 0)),
                  pl.BlockSpec((None, 1, D), lambda j, b, i: (b, 0, 0)),
                  pl.BlockSpec((None, 1, D), lambda j, b, i: (b, 0, 0)),
                  pl.BlockSpec((D, tn), lambda j, b, i: (0, j))],
        out_specs=pl.BlockSpec((None, tm, tn), lambda j, b, i: (b, i, j)),
        compiler_params=_params("parallel", "parallel", "parallel"),
        name="in_proj",
    )(x, gain, shift, scale, w)


def _inproj_qkv_kernel(x_ref, g_ref, sh_ref, sc_ref, wz_ref, wa_ref, cos_ref, sin_ref, qn_ref, kn_ref, bd_ref, *rest):
    z_ref, qg_ref, kg_ref, vg_ref, qd_ref, kd_ref, vd_ref = rest[-7:]
    h = _rms_mod(x_ref[...], g_ref[...], sh_ref[...], sc_ref[...]).astype(BF16)
    acc = jnp.dot(h, wa_ref[...], preferred_element_type=F32)
    slab = lambda col, width=BRANCH: acc[:, col - COL_GQ:col - COL_GQ + width]
    _qkv_store(slab(COL_GQ), slab(COL_DQ), slab(COL_DK), slab(COL_DV), slab(COL_KV, 2 * LANE),
               cos_ref[...], sin_ref[...], qn_ref[...], kn_ref[...], bd_ref[...],
               qg_ref, kg_ref, vg_ref, qd_ref, kd_ref, vd_ref)
    z_ref[...] = jnp.dot(h, wz_ref[...], preferred_element_type=F32).astype(z_ref.dtype)


def _kv_buffers(B, n_keys):
    vg_rows, vd_rows = GQA_KV_HEADS * GQA_VT_ROWS, DIFF_HEADS * DIFF_VT_ROWS
    shapes = ((B, n_keys, LANE), (B, vg_rows, n_keys), (B, n_keys, BRANCH), (B, vd_rows, n_keys))
    return tuple(jnp.zeros(s, BF16) for s in shapes)


def _inproj_qkv(x, gain, shift, scale, w_z, w_att, cosf, sins, qn, kn, ones_bd, kv_bufs, key_start):
    B, S, D = x.shape
    n_keys = kv_bufs[0].shape[1]
    tm = min(512, S)
    assert key_start % tm == 0
    k0 = key_start // tm
    const = lambda shape: pl.BlockSpec(shape, lambda b, i: (0, 0))
    per_b = pl.BlockSpec((None, 1, D), lambda b, i: (b, 0, 0))
    tab = pl.BlockSpec((tm, LANE), lambda b, i: (i, 0))
    rows = lambda width: pl.BlockSpec((None, tm, width), lambda b, i: (b, i, 0))
    q_cols = lambda chans: pl.BlockSpec((None, chans, tm), lambda b, i: (b, 0, i))
    k_rows = lambda width: pl.BlockSpec((None, tm, width), lambda b, i: (b, k0 + i, 0))
    v_cols = lambda chans: pl.BlockSpec((None, chans, tm), lambda b, i: (b, 0, k0 + i))
    vg_rows, vd_rows = GQA_KV_HEADS * GQA_VT_ROWS, DIFF_HEADS * DIFF_VT_ROWS
    shapes = [(B, S, 3 * BRANCH), (B, 2 * BRANCH, S), (B, n_keys, LANE), (B, vg_rows, n_keys),
              (B, 2 * BRANCH, S), (B, n_keys, BRANCH), (B, vd_rows, n_keys)]
    specs = [rows(3 * BRANCH), q_cols(2 * BRANCH), k_rows(LANE), v_cols(vg_rows),
             q_cols(2 * BRANCH), k_rows(BRANCH), v_cols(vd_rows)]
    in_specs = [rows(D), const((1, D)), per_b, per_b, const(w_z.shape), const(w_att.shape), tab, tab,
                const((1, BRANCH)), const((1, BRANCH)), const((BRANCH, BRANCH))]
    args = [x, gain, shift, scale, w_z, w_att, cosf, sins, qn, kn, ones_bd]
    aliases = {}
    for out_idx, buf in zip((2, 3, 5, 6), kv_bufs):
        aliases[len(args)] = out_idx
        in_specs.append(pl.BlockSpec(memory_space=pl.ANY))
        args.append(buf)
    return pl.pallas_call(
        _inproj_qkv_kernel,
        out_shape=[jax.ShapeDtypeStruct(s, BF16) for s in shapes],
        grid=(B, S // tm),
        in_specs=in_specs,
        out_specs=specs,
        input_output_aliases=aliases,
        compiler_params=_params("parallel", "parallel"),
        name="in_proj_qkv",
    )(*args)


def _ffn_kernel(x_ref, g_ref, sh_ref, sc_ref, gate_ref, w1_ref, w3_ref, w2_ref, fg_ref, o_ref, *, final):
    x = x_ref[...]
    h = _rms_mod(x, g_ref[...], sh_ref[...], sc_ref[...]).astype(BF16)
    a = jnp.dot(h, w1_ref[...], preferred_element_type=F32)
    b = jnp.dot(h, w3_ref[...], preferred_element_type=F32)
    u = (_silu(a) * b).astype(BF16)
    y = x + gate_ref[...] * jnp.dot(u, w2_ref[...], preferred_element_type=F32)
    if final:
        y = y * lax.rsqrt(jnp.mean(y * y, axis=-1, keepdims=True) + EPS) * fg_ref[...]
    o_ref[...] = y


def _ffn(x, gain, shift, scale, gate, w1, w3, w2, final_gain, final):
    B, S, D = x.shape
    H = w1.shape[1]
    tm = min(512, S)
    const = lambda shape: pl.BlockSpec(shape, lambda b, i: (0,) * len(shape), pipeline_mode=pl.Buffered(1))
    per_b = pl.BlockSpec((None, 1, D), lambda b, i: (b, 0, 0))
    return pl.pallas_call(
        functools.partial(_ffn_kernel, final=final),
        out_shape=jax.ShapeDtypeStruct((B, S, D), F32),
        grid=(B, S // tm),
        in_specs=[pl.BlockSpec((None, tm, D), lambda b, i: (b, i, 0)),
                  const((1, D)), per_b, per_b, per_b,
                  const((D, H)), const((D, H)), const((H, D)), const((1, D))],
        out_specs=pl.BlockSpec((None, tm, D), lambda b, i: (b, i, 0)),
        compiler_params=_params("parallel", "parallel"),
        name="ffn",
    )(x, gain, shift, scale, gate, w1, w3, w2, final_gain)


def _merge_kernel(x_ref, gm_ref, f_ref, c_ref, a_ref, d_ref, zg_ref,
                  wf_ref, wc_ref, wa_ref, wd_ref, wo_ref, o_ref):
    D = x_ref.shape[-1]
    acc = None
    for b, (br, w) in enumerate(((f_ref, wf_ref), (c_ref, wc_ref), (a_ref, wa_ref), (d_ref, wd_ref))):
        y = jnp.dot(br[...], w[...], preferred_element_type=F32)
        gate = jax.nn.sigmoid(zg_ref[:, b * D:(b + 1) * D].astype(F32))
        acc = gate * y if acc is None else acc + gate * y
    out = jnp.dot(acc.astype(BF16), wo_ref[...], preferred_element_type=F32)
    o_ref[...] = x_ref[...] + gm_ref[...] * out


def _merge(x, gm, f, cv, og, od, zg, wf, wc, wa, wd, wo):
    B, S, D = x.shape
    tm = min(512, S)
    row = lambda width: pl.BlockSpec((None, tm, width), lambda b, i: (b, i, 0))
    const = lambda shape: pl.BlockSpec(shape, lambda b, i: (0, 0))
    return pl.pallas_call(
        _merge_kernel,
        out_shape=jax.ShapeDtypeStruct((B, S, D), F32),
        grid=(B, S // tm),
        in_specs=[row(D), pl.BlockSpec((None, 1, D), lambda b, i: (b, 0, 0)),
                  row(BRANCH), row(BRANCH), row(BRANCH), row(BRANCH),
                  pl.BlockSpec((None, tm, N_BRANCH * D), lambda b, i: (b, i, 0)),
                  const((BRANCH, D)), const((BRANCH, D)), const((BRANCH, D)), const((BRANCH, D)),
                  const((D, D))],
        out_specs=row(D),
        compiler_params=_params("parallel", "parallel"),
        name="merge",
    )(x, gm, f, cv, og, od, zg, wf, wc, wa, wd, wo)


def _conv_kernel(a_ref, g_ref, ap_ref, gp_ref, an_ref, gn_ref, w_ref, b_ref, lg_ref, lb_ref, o_ref, u_s,
                 *, ts, chunk):
    i = pl.program_id(1)
    n = pl.num_programs(1)

    def glu(a, g):
        return a.astype(F32) * jax.nn.sigmoid(g.astype(F32))

    u_s[CONV_HALO:CONV_HALO + ts, :] = glu(a_ref[...], g_ref[...])
    u_s[0:CONV_HALO, :] = jnp.where(i > 0, glu(ap_ref[...], gp_ref[...]), 0.0)
    u_s[CONV_HALO + ts:2 * CONV_HALO + ts, :] = jnp.where(i < n - 1, glu(an_ref[...], gn_ref[...]), 0.0)
    pad = CONV_K // 2

    def body(c, carry):
        r0 = pl.multiple_of(c * chunk, chunk)
        win = u_s[pl.ds(r0, chunk + 2 * CONV_HALO), :]
        rows = chunk + 2 * CONV_HALO
        acc = jnp.zeros((chunk // SUBLANE, SUBLANE, BRANCH), F32) + b_ref[...]
        for r in range(SUBLANE):
            taps = [j for j in range(CONV_K) if (CONV_HALO - pad + j) % SUBLANE == r]
            shifted = win if r == 0 else pltpu.roll(win, rows - r, 0)
            for j in taps:
                base = CONV_HALO - pad + j - r
                tap = shifted[base:base + chunk, :].reshape(chunk // SUBLANE, SUBLANE, BRANCH)
                acc = acc + w_ref[j] * tap
        acc = acc.reshape(chunk, BRANCH)
        mu = jnp.mean(acc, axis=-1, keepdims=True)
        d = acc - mu
        var = jnp.mean(d * d, axis=-1, keepdims=True)
        y = d * lax.rsqrt(var + LN_EPS) * lg_ref[...] + lb_ref[...]
        o_ref[pl.ds(r0, chunk), :] = _silu(y).astype(o_ref.dtype)
        return carry

    lax.fori_loop(0, ts // chunk, body, 0)


def _conv_branch(z, w, b, ln_g, ln_b):
    B, S, _ = z.shape
    ts = min(512, S)
    chunk = min(256, ts)
    hb = ts // CONV_HALO
    last = S // CONV_HALO - 1
    cur = lambda blk: pl.BlockSpec((None, ts, BRANCH), lambda b_, i: (b_, i, blk))
    prev = lambda blk: pl.BlockSpec((None, CONV_HALO, BRANCH),
                                    lambda b_, i: (b_, jnp.maximum(i * hb - 1, 0), blk))
    nxt = lambda blk: pl.BlockSpec((None, CONV_HALO, BRANCH),
                                   lambda b_, i: (b_, jnp.minimum((i + 1) * hb, last), blk))
    const = lambda shape: pl.BlockSpec(shape, lambda b_, i: (0, 0))
    return pl.pallas_call(
        functools.partial(_conv_kernel, ts=ts, chunk=chunk),
        out_shape=jax.ShapeDtypeStruct((B, S, BRANCH), BF16),
        grid=(B, S // ts),
        in_specs=[cur(BLK_CA), cur(BLK_CG), prev(BLK_CA), prev(BLK_CG), nxt(BLK_CA), nxt(BLK_CG),
                  pl.BlockSpec((CONV_K, SUBLANE, BRANCH), lambda b_, i: (0, 0, 0)),
                  const((1, BRANCH)), const((1, BRANCH)), const((1, BRANCH))],
        out_specs=pl.BlockSpec((None, ts, BRANCH), lambda b_, i: (b_, i, 0)),
        scratch_shapes=[pltpu.VMEM((ts + 2 * CONV_HALO, BRANCH), F32)],
        compiler_params=_params("parallel", "parallel"),
        name="conv_branch",
    )(z, z, z, z, z, z, jnp.broadcast_to(w[:, None, :], (CONV_K, SUBLANE, BRANCH)), b, ln_g, ln_b)


def _head_rms(x, ones_bd):
    x2 = x * x
    hi = x2.astype(BF16)
    lo = (x2 - hi.astype(F32)).astype(BF16)
    ssum = (jnp.dot(hi, ones_bd, preferred_element_type=F32)
            + jnp.dot(lo, ones_bd, preferred_element_type=F32))
    return x * lax.rsqrt(ssum * (1.0 / HEAD_DIM) + EPS)


def _rope(x, cosf, sins):
    width = x.shape[1]
    reps = width // LANE
    c = jnp.concatenate([cosf] * reps, axis=1) if reps > 1 else cosf
    s = jnp.concatenate([sins] * reps, axis=1) if reps > 1 else sins
    lane = lax.broadcasted_iota(jnp.int32, x.shape, 1)
    first_half = (lane & (HEAD_DIM - 1)) < HEAD_DIM // 2
    partner = jnp.where(first_half, pltpu.roll(x, width - HEAD_DIM // 2, 1), pltpu.roll(x, HEAD_DIM // 2, 1))
    return x * c + partner * s


def _qkv_store(gq, dq, dk, dv, kvz, cosf, sins, qn, kn, bd, qg_ref, kg_ref, vg_ref, qd_ref, kd_ref, vd_ref):
    scale = HEAD_DIM ** -0.5 * math.log2(math.e)
    ts = gq.shape[0]

    half = jnp.zeros((HEAD_DIM, ts), qg_ref.dtype)
    q = _rope(_head_rms(gq, bd) * qn, cosf, sins) * scale
    for p in range(BRANCH // LANE):
        qt = q[:, p * LANE:(p + 1) * LANE].T
        for e in range(2):
            h = 2 * p + e
            kvh = h // GQA_GROUP
            base = h * LANE
            qg_ref[base + kvh * HEAD_DIM:base + (kvh + 1) * HEAD_DIM, :] = (
                qt[e * HEAD_DIM:(e + 1) * HEAD_DIM, :].astype(qg_ref.dtype))
            qg_ref[base + (1 - kvh) * HEAD_DIM:base + (2 - kvh) * HEAD_DIM, :] = half

    k = _rope(_head_rms(kvz[:, :LANE], bd[:LANE, :LANE]) * kn[:, :LANE], cosf, sins)
    kg_ref[...] = k.astype(kg_ref.dtype)
    vt = kvz[:, LANE:2 * LANE].T
    row = lax.broadcasted_iota(jnp.int32, (GQA_VT_ROWS - HEAD_DIM, ts), 0)
    tail = jnp.where(row == 0, 1.0, 0.0).astype(vg_ref.dtype)
    for g in range(GQA_KV_HEADS):
        vg_ref[g * GQA_VT_ROWS:g * GQA_VT_ROWS + HEAD_DIM, :] = vt[g * HEAD_DIM:(g + 1) * HEAD_DIM, :].astype(vg_ref.dtype)
        vg_ref[g * GQA_VT_ROWS + HEAD_DIM:(g + 1) * GQA_VT_ROWS, :] = tail

    qd = _rope(dq, cosf, sins) * scale
    for h in range(DIFF_HEADS):
        qt = qd[:, h * LANE:(h + 1) * LANE].T.astype(qd_ref.dtype)
        base = 2 * h * LANE
        qd_ref[base:base + HEAD_DIM, :] = qt[:HEAD_DIM, :]
        qd_ref[base + HEAD_DIM:base + LANE, :] = half
        qd_ref[base + LANE:base + LANE + HEAD_DIM, :] = half
        qd_ref[base + LANE + HEAD_DIM:base + 2 * LANE, :] = qt[HEAD_DIM:, :]
        vd_ref[h * DIFF_VT_ROWS:h * DIFF_VT_ROWS + LANE, :] = (
            dv[:, h * LANE:(h + 1) * LANE].T.astype(vd_ref.dtype))
        vd_ref[h * DIFF_VT_ROWS + LANE:(h + 1) * DIFF_VT_ROWS, :] = tail
    kd_ref[...] = _rope(dk, cosf, sins).astype(kd_ref.dtype)


def _attn_kernel(*refs, sets, tq, tk, mode, lam_init):
    if mode == "diff":
        q_ref, k_ref, v_ref, l1q, l1k, l2q, l2k, dn_ref, o_ref, q_s, acc_s = refs
    else:
        q_ref, k_ref, v_ref, o_ref, q_s, acc_s = refs
    n_tiles = k_ref.shape[0] // tk
    width = sets * tq
    for r in range(sets):
        q_s[:, r * tq:(r + 1) * tq] = q_ref[r * LANE:(r + 1) * LANE, :]

    def finish():
        if mode == "diff":
            ot = acc_s[:LANE, :] / acc_s[LANE:LANE + 1, :]
            o = [ot[:, r * tq:(r + 1) * tq].T for r in range(sets)]
            lam = (jnp.exp(jnp.sum(l1q[...] * l1k[...], axis=1, keepdims=True))
                   - jnp.exp(jnp.sum(l2q[...] * l2k[...], axis=1, keepdims=True)) + lam_init)
            d = o[0] - lam * o[1]
            d = d * lax.rsqrt(jnp.mean(d * d, axis=1, keepdims=True) + EPS) * dn_ref[...]
            o_ref[...] = (d * (1.0 - lam_init)).astype(o_ref.dtype)
        else:
            ot = acc_s[:HEAD_DIM, :] / acc_s[HEAD_DIM:HEAD_DIM + 1, :]
            for pair in range(sets // 2):
                both = jnp.concatenate([ot[:, (2 * pair) * tq:(2 * pair + 1) * tq],
                                        ot[:, (2 * pair + 1) * tq:(2 * pair + 2) * tq]], axis=0)
                o_ref[:, pair * LANE:(pair + 1) * LANE] = both.T.astype(o_ref.dtype)

    m_run = None
    for t in range(n_tiles):
        st = jnp.dot(k_ref[t * tk:(t + 1) * tk, :], q_s[...], preferred_element_type=F32)
        m_tile = jnp.max(st, axis=0, keepdims=True)
        pt = jnp.exp2(st - (m_tile if t == 0 else m_run))
        pv = jnp.dot(v_ref[:, t * tk:(t + 1) * tk], pt.astype(BF16), preferred_element_type=F32)
        if t == 0:
            m_run = m_tile
            acc_s[...] = pv
        else:
            m_new = jnp.maximum(m_run, m_tile)
            alpha = jnp.exp2(m_run - m_new)
            acc_s[...] = (acc_s[...] + pv) * alpha
            m_run = m_new
    finish()
    overflowed = jnp.max(jnp.where(jnp.isfinite(acc_s[...]), 0.0, 1.0)) > 0.0

    @pl.when(overflowed)
    def _exact():
        def body(t, m_old):
            st = jnp.dot(k_ref[pl.ds(pl.multiple_of(t * tk, tk), tk), :], q_s[...], preferred_element_type=F32)
            m_new = jnp.maximum(m_old, jnp.max(st, axis=0, keepdims=True))
            alpha = jnp.exp2(m_old - m_new)
            pt = jnp.exp2(st - m_new)
            v_tile = v_ref[:, pl.ds(pl.multiple_of(t * tk, LANE), tk)]
            acc_s[...] = alpha * acc_s[...] + jnp.dot(v_tile, pt.astype(BF16), preferred_element_type=F32)
            return m_new

        acc_s[...] = jnp.zeros(acc_s.shape, F32)
        lax.fori_loop(0, n_tiles, body, jnp.full((1, width), -jnp.inf, F32))
        finish()


def _attention(qt, k, vt, mode, key_start, NK, lam_vecs=None, diff_norm=None, lam_init=0.0):
    B, _, S = qt.shape
    assert key_start % NK == 0
    kb = key_start // NK
    tq = min(512 if mode == "diff" else 256, S)
    tk = 768
    tk = tk if NK % tk == 0 else 256
    if mode == "diff":
        groups, sets, out_w, vrows = DIFF_HEADS, 2, LANE, DIFF_VT_ROWS
        kmap = lambda b, g, i: (b, kb, g)
    else:
        groups, sets, out_w, vrows = GQA_KV_HEADS, GQA_GROUP, GQA_GROUP * HEAD_DIM, GQA_VT_ROWS
        kmap = lambda b, g, i: (b, kb, 0)
    in_specs = [pl.BlockSpec((None, sets * LANE, tq), lambda b, g, i: (b, g, i)),
                pl.BlockSpec((None, NK, LANE), kmap),
                pl.BlockSpec((None, vrows, NK), lambda b, g, i: (b, g, kb))]
    args = [qt, k, vt]
    if mode == "diff":
        in_specs += [pl.BlockSpec((1, HEAD_DIM), lambda b, g, i: (0, 0))] * 4
        in_specs += [pl.BlockSpec((1, LANE), lambda b, g, i: (0, 0))]
        args += list(lam_vecs) + [diff_norm]
    return pl.pallas_call(
        functools.partial(_attn_kernel, sets=sets, tq=tq, tk=tk, mode=mode, lam_init=lam_init),
        out_shape=jax.ShapeDtypeStruct((B, S, BRANCH), BF16),
        grid=(B, groups, S // tq),
        in_specs=in_specs,
        out_specs=pl.BlockSpec((None, tq, out_w), lambda b, g, i: (b, i, g)),
        scratch_shapes=[pltpu.VMEM((LANE, sets * tq), BF16),
                        pltpu.VMEM((vrows, sets * tq), F32)],
        compiler_params=_params("parallel", "parallel", "parallel"),
        name="attn_" + mode,
    )(*args)


def _dft_cos_sin(n):
    idx = np.arange(n)
    ang = 2.0 * np.pi * ((idx[:, None] * idx[None, :]) % n) / n
    return np.cos(ang), np.sin(ang)


def _channel_dft_matrix():
    c, s = _dft_cos_sin(GRID_W)
    groups = BRANCH // GRID_W
    eye = np.eye(groups)
    return np.concatenate([np.kron(eye, c), -np.kron(eye, s)], axis=1) / math.sqrt(GRID_W)


def _fourier_a_kernel(z_ref, w0_ref, d1_ref, tc_ref, ts_ref, o_ref):
    n1, nb, _ = z_ref.shape
    u = jnp.dot(z_ref[...].reshape(n1 * nb, BRANCH), w0_ref[...], preferred_element_type=F32)
    u = pltpu.einshape("kjm->jkm", u.reshape(n1, nb, 2 * BRANCH)).astype(BF16)
    reps = BRANCH // LANE
    out_r, out_i = [], []
    for j in range(nb):
        p = jnp.dot(d1_ref[...], u[j], preferred_element_type=F32)
        vr = p[:n1, :BRANCH] + p[n1:, BRANCH:]
        vi = p[:n1, BRANCH:] - p[n1:, :BRANCH]
        c = jnp.concatenate([tc_ref[j]] * reps, axis=1)
        s = jnp.concatenate([ts_ref[j]] * reps, axis=1)
        out_r.append(vr * c + vi * s)
        out_i.append(vi * c - vr * s)
    o_ref[0] = pltpu.einshape("jkm->kjm", jnp.stack(out_r)).astype(o_ref.dtype)
    o_ref[1] = pltpu.einshape("jkm->kjm", jnp.stack(out_i)).astype(o_ref.dtype)


def _fourier_b_kernel(v_ref, c3_ref, s3_ref, o_ref):
    nb = v_ref.shape[1]
    outs = [jnp.dot(c3_ref[...], v_ref[0, j], preferred_element_type=F32)
            + jnp.dot(s3_ref[...], v_ref[1, j], preferred_element_type=F32) for j in range(nb)]
    o_ref[...] = pltpu.einshape("jkm->kjm", jnp.stack(outs)).astype(o_ref.dtype)


def _bmm_kernel(a_ref, x_ref, o_ref):
    o_ref[...] = jnp.dot(a_ref[...], x_ref[...], preferred_element_type=F32).astype(o_ref.dtype)


def _bmm(a, x, tn):
    M, K = a.shape
    B, _, N = x.shape
    tn = min(tn, N)
    return pl.pallas_call(
        _bmm_kernel,
        out_shape=jax.ShapeDtypeStruct((B, M, N), BF16),
        grid=(B, N // tn),
        in_specs=[pl.BlockSpec((M, K), lambda b, j: (0, 0)),
                  pl.BlockSpec((None, K, tn), lambda b, j: (b, 0, j))],
        out_specs=pl.BlockSpec((None, M, tn), lambda b, j: (b, 0, j)),
        compiler_params=_params("parallel", "parallel"),
        name="bmm",
    )(a, x)


def _slab_matmul_kernel(z_ref, w_ref, o_ref):
    o_ref[...] = jnp.dot(z_ref[...], w_ref[...], preferred_element_type=F32).astype(o_ref.dtype)


def _slab_matmul(z, blk, w):
    B, S, _ = z.shape
    K, N = w.shape
    ts = min(256, S)
    return pl.pallas_call(
        _slab_matmul_kernel,
        out_shape=jax.ShapeDtypeStruct((B, S, N), BF16),
        grid=(B, S // ts),
        in_specs=[pl.BlockSpec((None, ts, K), lambda b, i: (b, i, blk)),
                  pl.BlockSpec((K, N), lambda b, i: (0, 0))],
        out_specs=pl.BlockSpec((None, ts, N), lambda b, i: (b, i, 0)),
        compiler_params=_params("parallel", "parallel"),
        name="slab_matmul",
    )(z, w)


def _fourier_long(z, w0):
    B, S, _ = z.shape
    n2 = GRID_W
    n1 = S // n2
    nb = FOURIER_NB
    assert n1 % nb == 0 and n2 % nb == 0
    c1, s1 = _dft_cos_sin(n1)
    d1 = jnp.asarray(np.concatenate([c1, s1], axis=0) / math.sqrt(n1), F32).astype(BF16)
    ang = 2.0 * np.pi * (np.arange(n2)[:, None] * np.arange(n1)[None, :]) / S
    twc = jnp.asarray(np.repeat(np.cos(ang)[:, :, None], LANE, axis=2), F32)
    tws = jnp.asarray(np.repeat(np.sin(ang)[:, :, None], LANE, axis=2), F32)
    v = pl.pallas_call(
        _fourier_a_kernel,
        out_shape=jax.ShapeDtypeStruct((B, 2, n1, n2, BRANCH), BF16),
        grid=(B, n2 // nb),
        in_specs=[pl.BlockSpec((None, n1, nb, BRANCH), lambda b, j: (b, 0, j, BLK_FOUR)),
                  pl.BlockSpec((BRANCH, 2 * BRANCH), lambda b, j: (0, 0)),
                  pl.BlockSpec((2 * n1, n1), lambda b, j: (0, 0)),
                  pl.BlockSpec((nb, n1, LANE), lambda b, j: (j, 0, 0)),
                  pl.BlockSpec((nb, n1, LANE), lambda b, j: (j, 0, 0))],
        out_specs=pl.BlockSpec((None, 2, n1, nb, BRANCH), lambda b, j: (b, 0, 0, j, 0)),
        compiler_params=_params("parallel", "parallel"),
        name="fourier_a",
    )(z.reshape(B, n1, n2, z.shape[-1]), w0, d1, twc, tws)
    c3, s3 = _dft_cos_sin(n2)
    c3 = jnp.asarray(c3 / math.sqrt(n2), F32).astype(BF16)
    s3 = jnp.asarray(s3 / math.sqrt(n2), F32).astype(BF16)
    f = pl.pallas_call(
        _fourier_b_kernel,
        out_shape=jax.ShapeDtypeStruct((B, n2, n1, BRANCH), BF16),
        grid=(B, n1 // nb),
        in_specs=[pl.BlockSpec((None, 2, nb, n2, BRANCH), lambda b, i: (b, 0, i, 0, 0)),
                  pl.BlockSpec((n2, n2), lambda b, i: (0, 0)),
                  pl.BlockSpec((n2, n2), lambda b, i: (0, 0))],
        out_specs=pl.BlockSpec((None, n2, nb, BRANCH), lambda b, i: (b, 0, i, 0)),
        compiler_params=_params("parallel", "parallel"),
        name="fourier_b",
    )(v, c3, s3)
    return f.reshape(B, S, BRANCH)


def _fourier_short(z, w0):
    B, S, _ = z.shape
    u = _slab_matmul(z, BLK_FOUR, w0)
    ust = jnp.concatenate([u[:, :, :BRANCH], u[:, :, BRANCH:]], axis=1)
    c, s = _dft_cos_sin(S)
    dl = jnp.asarray(np.concatenate([c, s], axis=1) / math.sqrt(S), F32).astype(BF16)
    return _bmm(dl, ust, BRANCH)


def _rope_tables(n_rows):
    half = HEAD_DIM // 4
    inv = ROPE_BASE ** (-np.arange(0, HEAD_DIM // 2, 2, dtype=np.float64) / (HEAD_DIM // 2))
    row = np.repeat(np.arange(n_rows, dtype=np.float64), GRID_W)
    col = np.tile(np.arange(GRID_W, dtype=np.float64), n_rows)
    ang = np.concatenate([row[:, None] * inv, col[:, None] * inv], axis=-1)
    assert ang.shape[1] == 2 * half
    cos, sin = np.cos(ang), np.sin(ang)
    cosf = np.tile(np.concatenate([cos, cos], axis=1), (1, LANE // HEAD_DIM))
    sins = np.tile(np.concatenate([-sin, sin], axis=1), (1, LANE // HEAD_DIM))
    return jnp.asarray(cosf, F32), jnp.asarray(sins, F32)


def _mixer_branches(z, lw, w0, long_seq):
    f = _fourier_long(z, w0) if long_seq else _fourier_short(z, w0)
    cv = _conv_branch(z, lw["conv_w"], lw["conv_b"], lw["conv_ln_g"], lw["conv_ln_b"])
    return f, cv


def kernel(x, c, ctx, c_ctx, w_ada, b_ada, norm_mix, w_in, w_four, conv_w, conv_b, conv_ln_g, conv_ln_b, w_conv, q_norm, k_norm, w_gqa, lam_q1, lam_k1, lam_q2, lam_k2, diff_norm, w_diff, w_out, norm_ffn, w_ffn1, w_ffn3, w_ffn2, final_norm):
    B, S, D = x.shape
    Sc = ctx.shape[1]
    depth = w_ada.shape[0]
    assert B + 1 <= 8 and S % GRID_W == 0 and D == 2 * BRANCH

    cond = jnp.zeros((8, D), F32).at[:B].set(c).at[B].set(c_ctx)
    mods = _ada_mod(cond, w_ada, b_ada)

    rope_x = _rope_tables(S // GRID_W)
    rope_c = (jnp.ones((Sc, LANE), F32), jnp.zeros((Sc, LANE), F32))
    ones_bd = jnp.asarray(np.kron(np.eye(BRANCH // HEAD_DIM), np.ones((HEAD_DIM, HEAD_DIM))), F32).astype(BF16)
    w0 = jnp.asarray(_channel_dft_matrix(), F32).astype(BF16)
    tile2 = lambda v: jnp.tile(v, LANE // HEAD_DIM * (BRANCH // LANE)).reshape(1, BRANCH)

    kv = _kv_buffers(B, S + Sc)
    for l in range(depth):
        last = l == depth - 1
        lam_init = 0.8 - 0.6 * math.exp(-0.3 * l)
        mx = [m.reshape(B, 1, D) for m in jnp.split(mods[l, :B], 6, axis=-1)]
        mc = [jnp.broadcast_to(m.reshape(1, 1, D), (B, 1, D)) for m in jnp.split(mods[l, B], 6, axis=-1)]
        lw = dict(conv_w=conv_w[l], conv_b=conv_b[l].reshape(1, BRANCH),
                  conv_ln_g=conv_ln_g[l].reshape(1, BRANCH), conv_ln_b=conv_ln_b[l].reshape(1, BRANCH),
                  q_norm=tile2(q_norm[l]), k_norm=tile2(k_norm[l]))
        w_z, w_att, w_gate = (w_in[l][:, a:b].astype(BF16) for a, b in ((0, COL_GQ), (COL_GQ, GATE_COL0), (GATE_COL0, None)))
        wf, wc, wa, wd, wo = (w.astype(BF16) for w in (w_four[l], w_conv[l], w_gqa[l], w_diff[l], w_out[l]))
        w1, w3, w2 = (w.astype(BF16) for w in (w_ffn1[l], w_ffn3[l], w_ffn2[l]))
        gain_m, gain_f = norm_mix[l].reshape(1, D), norm_ffn[l].reshape(1, D)
        lam_vecs = [v[l].reshape(1, HEAD_DIM) for v in (lam_q1, lam_k1, lam_q2, lam_k2)]
        dn = diff_norm[l].reshape(1, LANE)
        fin = final_norm.reshape(1, D)

        zx, qgx, kg, vg, qdx, kd, vd = _inproj_qkv(x, gain_m, mx[0], mx[1], w_z, w_att, rope_x[0], rope_x[1],
                                                   lw["q_norm"], lw["k_norm"], ones_bd, kv, 0)
        zc, qgc, kg, vg, qdc, kd, vd = _inproj_qkv(ctx, gain_m, mc[0], mc[1], w_z, w_att, rope_c[0], rope_c[1],
                                                   lw["q_norm"], lw["k_norm"], ones_bd, (kg, vg, kd, vd), S)
        kv = (kg, vg, kd, vd)
        zgx = _inproj(x, gain_m, mx[0], mx[1], w_gate)
        zgc = _inproj(ctx, gain_m, mc[0], mc[1], w_gate)
        fx, cvx = _mixer_branches(zx, lw, w0, True)
        fc, cvc = _mixer_branches(zc, lw, w0, False)

        ogx = _attention(qgx, kg, vg, "gqa", 0, S + Sc)
        odx = _attention(qdx, kd, vd, "diff", 0, S + Sc, lam_vecs, dn, lam_init)
        x = _merge(x, mx[2], fx, cvx, ogx, odx, zgx, wf, wc, wa, wd, wo)
        if not last:
            ogc = _attention(qgc, kg, vg, "gqa", S, Sc)
            odc = _attention(qdc, kd, vd, "diff", S, Sc, lam_vecs, dn, lam_init)
            ctx = _merge(ctx, mc[2], fc, cvc, ogc, odc, zgc, wf, wc, wa, wd, wo)
            ctx = _ffn(ctx, gain_f, mc[3], mc[4], mc[5], w1, w3, w2, fin, False)
        x = _ffn(x, gain_f, mx[3], mx[4], mx[5], w1, w3, w2, fin, last)
    return x
```

```python
import functools
import math

import numpy as np
import jax
import jax.numpy as jnp
from jax import lax
from jax.experimental import pallas as pl
from jax.experimental.pallas import tpu as pltpu

F32 = jnp.float32
BF16 = jnp.bfloat16

HEAD_DIM = 64
BRANCH = 512
GQA_GROUP = 4
GQA_KV_HEADS = 2
DIFF_VT_ROWS = 144
GQA_VT_ROWS = 80
DIFF_HEADS = 4
N_BRANCH = 4
GRID_W = 64
CONV_K = 31
ROPE_BASE = 10000.0
EPS = 1e-6
LN_EPS = 1e-5

LANE = 128
SUBLANE = 8
VMEM_LIMIT = 56 * 1024 * 1024

BLK_FOUR, BLK_CA, BLK_CG = range(3)
COL_GQ, COL_KV, COL_DQ, COL_DK, COL_DV, GATE_COL0 = 1536, 2048, 2304, 2816, 3328, 3840
FOURIER_NB = 16
CONV_HALO = 16


def _params(*sem):
    return pltpu.CompilerParams(dimension_semantics=sem, vmem_limit_bytes=VMEM_LIMIT)


def _silu(x):
    return x * jax.nn.sigmoid(x)


def _rms_mod(x, g, shift, scale):
    y = x * lax.rsqrt(jnp.mean(x * x, axis=-1, keepdims=True) + EPS) * g
    return y * (1.0 + scale) + shift


def _ada_kernel(c_ref, w_ref, b_ref, o_ref):
    a = _silu(c_ref[...])
    o_ref[...] = jnp.dot(a, w_ref[...], precision=lax.Precision.HIGHEST,
                         preferred_element_type=F32) + b_ref[...]


def _ada_mod(cond, w_ada, b_ada):
    L, D, N = w_ada.shape
    tn = 1536
    return pl.pallas_call(
        _ada_kernel,
        out_shape=jax.ShapeDtypeStruct((L, 8, N), F32),
        grid=(L, N // tn),
        in_specs=[pl.BlockSpec((8, D), lambda l, j: (0, 0)),
                  pl.BlockSpec((None, D, tn), lambda l, j: (l, 0, j)),
                  pl.BlockSpec((None, 1, tn), lambda l, j: (l, 0, j))],
        out_specs=pl.BlockSpec((None, 8, tn), lambda l, j: (l, 0, j)),
        compiler_params=_params("parallel", "parallel"),
        name="ada_mod",
    )(cond, w_ada, b_ada.reshape(L, 1, N))


def _inproj_kernel(x_ref, g_ref, sh_ref, sc_ref, w_ref, o_ref):
    h = _rms_mod(x_ref[...], g_ref[...], sh_ref[...], sc_ref[...])
    o_ref[...] = jnp.dot(h.astype(BF16), w_ref[...], preferred_element_type=F32).astype(o_ref.dtype)


def _inproj(x, gain, shift, scale, w):
    B, S, D = x.shape
    N = w.shape[1]
    tm = min(512, S)
    tn = 4096
    return pl.pallas_call(
        _inproj_kernel,
        out_shape=jax.ShapeDtypeStruct((B, S, N), BF16),
        grid=(N // tn, B, S // tm),
        in_specs=[pl.BlockSpec((None, tm, D), lambda j, b, i: (b, i, 0)),
                  pl.BlockSpec((1, D), lambda j, b, i: (0, 0)),
                  pl.BlockSpec((None, 1, D), lambda j, b, i: (b, 0, 0)),
                  pl.BlockSpec((None, 1, D), lambda j, b, i: (b, 0, 0)),
                  pl.BlockSpec((D, tn), lambda j, b, i: (0, j))],
        out_specs=pl.BlockSpec((None, tm, tn), lambda j, b, i: (b, i, j)),
        compiler_params=_params("parallel", "parallel", "parallel"),
        name="in_proj",
    )(x, gain, shift, scale, w)


def _inproj_qkv_kernel(x_ref, g_ref, sh_ref, sc_ref, wz_ref, wa_ref, cos_ref, sin_ref, qn_ref, kn_ref, bd_ref, *rest):
    z_ref, qg_ref, kg_ref, vg_ref, qd_ref, kd_ref, vd_ref = rest[-7:]
    h = _rms_mod(x_ref[...], g_ref[...], sh_ref[...], sc_ref[...]).astype(BF16)
    acc = jnp.dot(h, wa_ref[...], preferred_element_type=F32)
    slab = lambda col, width=BRANCH: acc[:, col - COL_GQ:col - COL_GQ + width]
    _qkv_store(slab(COL_GQ), slab(COL_DQ), slab(COL_DK), slab(COL_DV), slab(COL_KV, 2 * LANE),
               cos_ref[...], sin_ref[...], qn_ref[...], kn_ref[...], bd_ref[...],
               qg_ref, kg_ref, vg_ref, qd_ref, kd_ref, vd_ref)
    z_ref[...] = jnp.dot(h, wz_ref[...], preferred_element_type=F32).astype(z_ref.dtype)


def _kv_buffers(B, n_keys):
    vg_rows, vd_rows = GQA_KV_HEADS * GQA_VT_ROWS, DIFF_HEADS * DIFF_VT_ROWS
    shapes = ((B, n_keys, LANE), (B, vg_rows, n_keys), (B, n_keys, BRANCH), (B, vd_rows, n_keys))
    return tuple(jnp.zeros(s, BF16) for s in shapes)


def _inproj_qkv(x, gain, shift, scale, w_z, w_att, cosf, sins, qn, kn, ones_bd, kv_bufs, key_start):
    B, S, D = x.shape
    n_keys = kv_bufs[0].shape[1]
    tm = min(512, S)
    assert key_start % tm == 0
    k0 = key_start // tm
    const = lambda shape: pl.BlockSpec(shape, lambda b, i: (0, 0))
    per_b = pl.BlockSpec((None, 1, D), lambda b, i: (b, 0, 0))
    tab = pl.BlockSpec((tm, LANE), lambda b, i: (i, 0))
    rows = lambda width: pl.BlockSpec((None, tm, width), lambda b, i: (b, i, 0))
    q_cols = lambda chans: pl.BlockSpec((None, chans, tm), lambda b, i: (b, 0, i))
    k_rows = lambda width: pl.BlockSpec((None, tm, width), lambda b, i: (b, k0 + i, 0))
    v_cols = lambda chans: pl.BlockSpec((None, chans, tm), lambda b, i: (b, 0, k0 + i))
    vg_rows, vd_rows = GQA_KV_HEADS * GQA_VT_ROWS, DIFF_HEADS * DIFF_VT_ROWS
    shapes = [(B, S, 3 * BRANCH), (B, 2 * BRANCH, S), (B, n_keys, LANE), (B, vg_rows, n_keys),
              (B, 2 * BRANCH, S), (B, n_keys, BRANCH), (B, vd_rows, n_keys)]
    specs = [rows(3 * BRANCH), q_cols(2 * BRANCH), k_rows(LANE), v_cols(vg_rows),
             q_cols(2 * BRANCH), k_rows(BRANCH), v_cols(vd_rows)]
    in_specs = [rows(D), const((1, D)), per_b, per_b, const(w_z.shape), const(w_att.shape), tab, tab,
                const((1, BRANCH)), const((1, BRANCH)), const((BRANCH, BRANCH))]
    args = [x, gain, shift, scale, w_z, w_att, cosf, sins, qn, kn, ones_bd]
    aliases = {}
    for out_idx, buf in zip((2, 3, 5, 6), kv_bufs):
        aliases[len(args)] = out_idx
        in_specs.append(pl.BlockSpec(memory_space=pl.ANY))
        args.append(buf)
    return pl.pallas_call(
        _inproj_qkv_kernel,
        out_shape=[jax.ShapeDtypeStruct(s, BF16) for s in shapes],
        grid=(B, S // tm),
        in_specs=in_specs,
        out_specs=specs,
        input_output_aliases=aliases,
        compiler_params=_params("parallel", "parallel"),
        name="in_proj_qkv",
    )(*args)


def _ffn_kernel(x_ref, g_ref, sh_ref, sc_ref, gate_ref, w1_ref, w3_ref, w2_ref, fg_ref, o_ref, *, final):
    x = x_ref[...]
    h = _rms_mod(x, g_ref[...], sh_ref[...], sc_ref[...]).astype(BF16)
    a = jnp.dot(h, w1_ref[...], preferred_element_type=F32)
    b = jnp.dot(h, w3_ref[...], preferred_element_type=F32)
    u = (_silu(a) * b).astype(BF16)
    y = x + gate_ref[...] * jnp.dot(u, w2_ref[...], preferred_element_type=F32)
    if final:
        y = y * lax.rsqrt(jnp.mean(y * y, axis=-1, keepdims=True) + EPS) * fg_ref[...]
    o_ref[...] = y


def _ffn(x, gain, shift, scale, gate, w1, w3, w2, final_gain, final):
    B, S, D = x.shape
    H = w1.shape[1]
    tm = min(512, S)
    const = lambda shape: pl.BlockSpec(shape, lambda b, i: (0,) * len(shape), pipeline_mode=pl.Buffered(1))
    per_b = pl.BlockSpec((None, 1, D), lambda b, i: (b, 0, 0))
    return pl.pallas_call(
        functools.partial(_ffn_kernel, final=final),
        out_shape=jax.ShapeDtypeStruct((B, S, D), F32),
        grid=(B, S // tm),
        in_specs=[pl.BlockSpec((None, tm, D), lambda b, i: (b, i, 0)),
                  const((1, D)), per_b, per_b, per_b,
                  const((D, H)), const((D, H)), const((H, D)), const((1, D))],
        out_specs=pl.BlockSpec((None, tm, D), lambda b, i: (b, i, 0)),
        compiler_params=_params("parallel", "parallel"),
        name="ffn",
    )(x, gain, shift, scale, gate, w1, w3, w2, final_gain)


def _merge_kernel(x_ref, gm_ref, f_ref, c_ref, a_ref, d_ref, zg_ref,
                  wf_ref, wc_ref, wa_ref, wd_ref, wo_ref, o_ref):
    D = x_ref.shape[-1]
    acc = None
    for b, (br, w) in enumerate(((f_ref, wf_ref), (c_ref, wc_ref), (a_ref, wa_ref), (d_ref, wd_ref))):
        y = jnp.dot(br[...], w[...], preferred_element_type=F32)
        gate = jax.nn.sigmoid(zg_ref[:, b * D:(b + 1) * D].astype(F32))
        acc = gate * y if acc is None else acc + gate * y
    out = jnp.dot(acc.astype(BF16), wo_ref[...], preferred_element_type=F32)
    o_ref[...] = x_ref[...] + gm_ref[...] * out


def _merge(x, gm, f, cv, og, od, zg, wf, wc, wa, wd, wo):
    B, S, D = x.shape
    tm = min(512, S)
    row = lambda width: pl.BlockSpec((None, tm, width), lambda b, i: (b, i, 0))
    const = lambda shape: pl.BlockSpec(shape, lambda b, i: (0, 0))
    return pl.pallas_call(
        _merge_kernel,
        out_shape=jax.ShapeDtypeStruct((B, S, D), F32),
        grid=(B, S // tm),
        in_specs=[row(D), pl.BlockSpec((None, 1, D), lambda b, i: (b, 0, 0)),
                  row(BRANCH), row(BRANCH), row(BRANCH), row(BRANCH),
                  pl.BlockSpec((None, tm, N_BRANCH * D), lambda b, i: (b, i, 0)),
                  const((BRANCH, D)), const((BRANCH, D)), const((BRANCH, D)), const((BRANCH, D)),
                  const((D, D))],
        out_specs=row(D),
        compiler_params=_params("parallel", "parallel"),
        name="merge",
    )(x, gm, f, cv, og, od, zg, wf, wc, wa, wd, wo)


def _conv_kernel(a_ref, g_ref, ap_ref, gp_ref, an_ref, gn_ref, w_ref, b_ref, lg_ref, lb_ref, o_ref, u_s,
                 *, ts, chunk):
    i = pl.program_id(1)
    n = pl.num_programs(1)

    def glu(a, g):
        return a.astype(F32) * jax.nn.sigmoid(g.astype(F32))

    u_s[CONV_HALO:CONV_HALO + ts, :] = glu(a_ref[...], g_ref[...])
    u_s[0:CONV_HALO, :] = jnp.where(i > 0, glu(ap_ref[...], gp_ref[...]), 0.0)
    u_s[CONV_HALO + ts:2 * CONV_HALO + ts, :] = jnp.where(i < n - 1, glu(an_ref[...], gn_ref[...]), 0.0)
    pad = CONV_K // 2

    def body(c, carry):
        r0 = pl.multiple_of(c * chunk, chunk)
        win = u_s[pl.ds(r0, chunk + 2 * CONV_HALO), :]
        rows = chunk + 2 * CONV_HALO
        acc = jnp.zeros((chunk // SUBLANE, SUBLANE, BRANCH), F32) + b_ref[...]
        for r in range(SUBLANE):
            taps = [j for j in range(CONV_K) if (CONV_HALO - pad + j) % SUBLANE == r]
            shifted = win if r == 0 else pltpu.roll(win, rows - r, 0)
            for j in taps:
                base = CONV_HALO - pad + j - r
                tap = shifted[base:base + chunk, :].reshape(chunk // SUBLANE, SUBLANE, BRANCH)
                acc = acc + w_ref[j] * tap
        acc = acc.reshape(chunk, BRANCH)
        mu = jnp.mean(acc, axis=-1, keepdims=True)
        d = acc - mu
        var = jnp.mean(d * d, axis=-1, keepdims=True)
        y = d * lax.rsqrt(var + LN_EPS) * lg_ref[...] + lb_ref[...]
        o_ref[pl.ds(r0, chunk), :] = _silu(y).astype(o_ref.dtype)
        return carry

    lax.fori_loop(0, ts // chunk, body, 0)


def _conv_branch(z, w, b, ln_g, ln_b):
    B, S, _ = z.shape
    ts = min(512, S)
    chunk = min(256, ts)
    hb = ts // CONV_HALO
    last = S // CONV_HALO - 1
    cur = lambda blk: pl.BlockSpec((None, ts, BRANCH), lambda b_, i: (b_, i, blk))
    prev = lambda blk: pl.BlockSpec((None, CONV_HALO, BRANCH),
                                    lambda b_, i: (b_, jnp.maximum(i * hb - 1, 0), blk))
    nxt = lambda blk: pl.BlockSpec((None, CONV_HALO, BRANCH),
                                   lambda b_, i: (b_, jnp.minimum((i + 1) * hb, last), blk))
    const = lambda shape: pl.BlockSpec(shape, lambda b_, i: (0, 0))
    return pl.pallas_call(
        functools.partial(_conv_kernel, ts=ts, chunk=chunk),
        out_shape=jax.ShapeDtypeStruct((B, S, BRANCH), BF16),
        grid=(B, S // ts),
        in_specs=[cur(BLK_CA), cur(BLK_CG), prev(BLK_CA), prev(BLK_CG), nxt(BLK_CA), nxt(BLK_CG),
                  pl.BlockSpec((CONV_K, SUBLANE, BRANCH), lambda b_, i: (0, 0, 0)),
                  const((1, BRANCH)), const((1, BRANCH)), const((1, BRANCH))],
        out_specs=pl.BlockSpec((None, ts, BRANCH), lambda b_, i: (b_, i, 0)),
        scratch_shapes=[pltpu.VMEM((ts + 2 * CONV_HALO, BRANCH), F32)],
        compiler_params=_params("parallel", "parallel"),
        name="conv_branch",
    )(z, z, z, z, z, z, jnp.broadcast_to(w[:, None, :], (CONV_K, SUBLANE, BRANCH)), b, ln_g, ln_b)


def _head_rms(x, ones_bd):
    x2 = x * x
    hi = x2.astype(BF16)
    lo = (x2 - hi.astype(F32)).astype(BF16)
    ssum = (jnp.dot(hi, ones_bd, preferred_element_type=F32)
            + jnp.dot(lo, ones_bd, preferred_element_type=F32))
    return x * lax.rsqrt(ssum * (1.0 / HEAD_DIM) + EPS)


def _rope(x, cosf, sins):
    width = x.shape[1]
    reps = width // LANE
    c = jnp.concatenate([cosf] * reps, axis=1) if reps > 1 else cosf
    s = jnp.concatenate([sins] * reps, axis=1) if reps > 1 else sins
    lane = lax.broadcasted_iota(jnp.int32, x.shape, 1)
    first_half = (lane & (HEAD_DIM - 1)) < HEAD_DIM // 2
    partner = jnp.where(first_half, pltpu.roll(x, width - HEAD_DIM // 2, 1), pltpu.roll(x, HEAD_DIM // 2, 1))
    return x * c + partner * s


def _qkv_store(gq, dq, dk, dv, kvz, cosf, sins, qn, kn, bd, qg_ref, kg_ref, vg_ref, qd_ref, kd_ref, vd_ref):
    scale = HEAD_DIM ** -0.5 * math.log2(math.e)
    ts = gq.shape[0]

    half = jnp.zeros((HEAD_DIM, ts), qg_ref.dtype)
    q = _rope(_head_rms(gq, bd) * qn, cosf, sins) * scale
    for p in range(BRANCH // LANE):
        qt = q[:, p * LANE:(p + 1) * LANE].T
        for e in range(2):
            h = 2 * p + e
            kvh = h // GQA_GROUP
            base = h * LANE
            qg_ref[base + kvh * HEAD_DIM:base + (kvh + 1) * HEAD_DIM, :] = (
                qt[e * HEAD_DIM:(e + 1) * HEAD_DIM, :].astype(qg_ref.dtype))
            qg_ref[base + (1 - kvh) * HEAD_DIM:base + (2 - kvh) * HEAD_DIM, :] = half

    k = _rope(_head_rms(kvz[:, :LANE], bd[:LANE, :LANE]) * kn[:, :LANE], cosf, sins)
    kg_ref[...] = k.astype(kg_ref.dtype)
    vt = kvz[:, LANE:2 * LANE].T
    row = lax.broadcasted_iota(jnp.int32, (GQA_VT_ROWS - HEAD_DIM, ts), 0)
    tail = jnp.where(row == 0, 1.0, 0.0).astype(vg_ref.dtype)
    for g in range(GQA_KV_HEADS):
        vg_ref[g * GQA_VT_ROWS:g * GQA_VT_ROWS + HEAD_DIM, :] = vt[g * HEAD_DIM:(g + 1) * HEAD_DIM, :].astype(vg_ref.dtype)
        vg_ref[g * GQA_VT_ROWS + HEAD_DIM:(g + 1) * GQA_VT_ROWS, :] = tail

    qd = _rope(dq, cosf, sins) * scale
    for h in range(DIFF_HEADS):
        qt = qd[:, h * LANE:(h + 1) * LANE].T.astype(qd_ref.dtype)
        base = 2 * h * LANE
        qd_ref[base:base + HEAD_DIM, :] = qt[:HEAD_DIM, :]
        qd_ref[base + HEAD_DIM:base + LANE, :] = half
        qd_ref[base + LANE:base + LANE + HEAD_DIM, :] = half
        qd_ref[base + LANE + HEAD_DIM:base + 2 * LANE, :] = qt[HEAD_DIM:, :]
        vd_ref[h * DIFF_VT_ROWS:h * DIFF_VT_ROWS + LANE, :] = (
            dv[:, h * LANE:(h + 1) * LANE].T.astype(vd_ref.dtype))
        vd_ref[h * DIFF_VT_ROWS + LANE:(h + 1) * DIFF_VT_ROWS, :] = tail
    kd_ref[...] = _rope(dk, cosf, sins).astype(kd_ref.dtype)


def _attn_kernel(*refs, sets, tq, tk, mode, lam_init):
    if mode == "diff":
        q_ref, k_ref, v_ref, l1q, l1k, l2q, l2k, dn_ref, o_ref, q_s, acc_s = refs
    else:
        q_ref, k_ref, v_ref, o_ref, q_s, acc_s = refs
    n_tiles = k_ref.shape[0] // tk
    width = sets * tq
    for r in range(sets):
        q_s[:, r * tq:(r + 1) * tq] = q_ref[r * LANE:(r + 1) * LANE, :]

    def finish():
        if mode == "diff":
            ot = acc_s[:LANE, :] / acc_s[LANE:LANE + 1, :]
            o = [ot[:, r * tq:(r + 1) * tq].T for r in range(sets)]
            lam = (jnp.exp(jnp.sum(l1q[...] * l1k[...], axis=1, keepdims=True))
                   - jnp.exp(jnp.sum(l2q[...] * l2k[...], axis=1, keepdims=True)) + lam_init)
            d = o[0] - lam * o[1]
            d = d * lax.rsqrt(jnp.mean(d * d, axis=1, keepdims=True) + EPS) * dn_ref[...]
            o_ref[...] = (d * (1.0 - lam_init)).astype(o_ref.dtype)
        else:
            ot = acc_s[:HEAD_DIM, :] / acc_s[HEAD_DIM:HEAD_DIM + 1, :]
            for pair in range(sets // 2):
                both = jnp.concatenate([ot[:, (2 * pair) * tq:(2 * pair + 1) * tq],
                                        ot[:, (2 * pair + 1) * tq:(2 * pair + 2) * tq]], axis=0)
                o_ref[:, pair * LANE:(pair + 1) * LANE] = both.T.astype(o_ref.dtype)

    m_run = None
    for t in range(n_tiles):
        st = jnp.dot(k_ref[t * tk:(t + 1) * tk, :], q_s[...], preferred_element_type=F32)
        m_tile = jnp.max(st, axis=0, keepdims=True)
        pt = jnp.exp2(st - (m_tile if t == 0 else m_run))
        pv = jnp.dot(v_ref[:, t * tk:(t + 1) * tk], pt.astype(BF16), preferred_element_type=F32)
        if t == 0:
            m_run = m_tile
            acc_s[...] = pv
        else:
            m_new = jnp.maximum(m_run, m_tile)
            alpha = jnp.exp2(m_run - m_new)
            acc_s[...] = (acc_s[...] + pv) * alpha
            m_run = m_new
    finish()
    overflowed = jnp.max(jnp.where(jnp.isfinite(acc_s[...]), 0.0, 1.0)) > 0.0

    @pl.when(overflowed)
    def _exact():
        def body(t, m_old):
            st = jnp.dot(k_ref[pl.ds(pl.multiple_of(t * tk, tk), tk), :], q_s[...], preferred_element_type=F32)
            m_new = jnp.maximum(m_old, jnp.max(st, axis=0, keepdims=True))
            alpha = jnp.exp2(m_old - m_new)
            pt = jnp.exp2(st - m_new)
            v_tile = v_ref[:, pl.ds(pl.multiple_of(t * tk, LANE), tk)]
            acc_s[...] = alpha * acc_s[...] + jnp.dot(v_tile, pt.astype(BF16), preferred_element_type=F32)
            return m_new

        acc_s[...] = jnp.zeros(acc_s.shape, F32)
        lax.fori_loop(0, n_tiles, body, jnp.full((1, width), -jnp.inf, F32))
        finish()


def _attention(qt, k, vt, mode, key_start, NK, lam_vecs=None, diff_norm=None, lam_init=0.0):
    B, _, S = qt.shape
    assert key_start % NK == 0
    kb = key_start // NK
    tq = min(512 if mode == "diff" else 256, S)
    tk = 768
    tk = tk if NK % tk == 0 else 256
    if mode == "diff":
        groups, sets, out_w, vrows = DIFF_HEADS, 2, LANE, DIFF_VT_ROWS
        kmap = lambda b, g, i: (b, kb, g)
    else:
        groups, sets, out_w, vrows = GQA_KV_HEADS, GQA_GROUP, GQA_GROUP * HEAD_DIM, GQA_VT_ROWS
        kmap = lambda b, g, i: (b, kb, 0)
    in_specs = [pl.BlockSpec((None, sets * LANE, tq), lambda b, g, i: (b, g, i)),
                pl.BlockSpec((None, NK, LANE), kmap),
                pl.BlockSpec((None, vrows, NK), lambda b, g, i: (b, g, kb))]
    args = [qt, k, vt]
    if mode == "diff":
        in_specs += [pl.BlockSpec((1, HEAD_DIM), lambda b, g, i: (0, 0))] * 4
        in_specs += [pl.BlockSpec((1, LANE), lambda b, g, i: (0, 0))]
        args += list(lam_vecs) + [diff_norm]
    return pl.pallas_call(
        functools.partial(_attn_kernel, sets=sets, tq=tq, tk=tk, mode=mode, lam_init=lam_init),
        out_shape=jax.ShapeDtypeStruct((B, S, BRANCH), BF16),
        grid=(B, groups, S // tq),
        in_specs=in_specs,
        out_specs=pl.BlockSpec((None, tq, out_w), lambda b, g, i: (b, i, g)),
        scratch_shapes=[pltpu.VMEM((LANE, sets * tq), BF16),
                        pltpu.VMEM((vrows, sets * tq), F32)],
        compiler_params=_params("parallel", "parallel", "parallel"),
        name="attn_" + mode,
    )(*args)


def _dft_cos_sin(n):
    idx = np.arange(n)
    ang = 2.0 * np.pi * ((idx[:, None] * idx[None, :]) % n) / n
    return np.cos(ang), np.sin(ang)


def _channel_dft_matrix():
    c, s = _dft_cos_sin(GRID_W)
    groups = BRANCH // GRID_W
    eye = np.eye(groups)
    return np.concatenate([np.kron(eye, c), -np.kron(eye, s)], axis=1) / math.sqrt(GRID_W)


def _fourier_a_kernel(z_ref, w0_ref, d1_ref, tc_ref, ts_ref, o_ref):
    n1, nb, _ = z_ref.shape
    zt = pltpu.einshape("kjm->jkm", z_ref[...])
    u = jnp.dot(zt.reshape(nb * n1, BRANCH), w0_ref[...], preferred_element_type=F32)
    u = u.astype(BF16).reshape(nb, n1, 2 * BRANCH)
    reps = BRANCH // LANE
    out_r, out_i = [], []
    for j in range(nb):
        p = jnp.dot(d1_ref[...], u[j], preferred_element_type=F32)
        vr = p[:n1, :BRANCH] + p[n1:, BRANCH:]
        vi = p[:n1, BRANCH:] - p[n1:, :BRANCH]
        c = jnp.concatenate([tc_ref[j]] * reps, axis=1)
        s = jnp.concatenate([ts_ref[j]] * reps, axis=1)
        out_r.append(vr * c + vi * s)
        out_i.append(vi * c - vr * s)
    o_ref[0] = pltpu.einshape("jkm->kjm", jnp.stack(out_r).astype(o_ref.dtype))
    o_ref[1] = pltpu.einshape("jkm->kjm", jnp.stack(out_i).astype(o_ref.dtype))


def _fourier_b_kernel(v_ref, c3_ref, s3_ref, o_ref):
    nb = v_ref.shape[1]
    outs = [jnp.dot(c3_ref[...], v_ref[0, j], preferred_element_type=F32)
            + jnp.dot(s3_ref[...], v_ref[1, j], preferred_element_type=F32) for j in range(nb)]
    o_ref[...] = pltpu.einshape("jkm->kjm", jnp.stack(outs).astype(o_ref.dtype))


def _bmm_kernel(a_ref, x_ref, o_ref):
    o_ref[...] = jnp.dot(a_ref[...], x_ref[...], preferred_element_type=F32).astype(o_ref.dtype)


def _bmm(a, x, tn):
    M, K = a.shape
    B, _, N = x.shape
    tn = min(tn, N)
    return pl.pallas_call(
        _bmm_kernel,
        out_shape=jax.ShapeDtypeStruct((B, M, N), BF16),
        grid=(B, N // tn),
        in_specs=[pl.BlockSpec((M, K), lambda b, j: (0, 0)),
                  pl.BlockSpec((None, K, tn), lambda b, j: (b, 0, j))],
        out_specs=pl.BlockSpec((None, M, tn), lambda b, j: (b, 0, j)),
        compiler_params=_params("parallel", "parallel"),
        name="bmm",
    )(a, x)


def _slab_matmul_kernel(z_ref, w_ref, o_ref):
    o_ref[...] = jnp.dot(z_ref[...], w_ref[...], preferred_element_type=F32).astype(o_ref.dtype)


def _slab_matmul(z, blk, w):
    B, S, _ = z.shape
    K, N = w.shape
    ts = min(256, S)
    return pl.pallas_call(
        _slab_matmul_kernel,
        out_shape=jax.ShapeDtypeStruct((B, S, N), BF16),
        grid=(B, S // ts),
        in_specs=[pl.BlockSpec((None, ts, K), lambda b, i: (b, i, blk)),
                  pl.BlockSpec((K, N), lambda b, i: (0, 0))],
        out_specs=pl.BlockSpec((None, ts, N), lambda b, i: (b, i, 0)),
        compiler_params=_params("parallel", "parallel"),
        name="slab_matmul",
    )(z, w)


def _fourier_long(z, w0):
    B, S, _ = z.shape
    n2 = GRID_W
    n1 = S // n2
    nb = FOURIER_NB
    assert n1 % nb == 0 and n2 % nb == 0
    c1, s1 = _dft_cos_sin(n1)
    d1 = jnp.asarray(np.concatenate([c1, s1], axis=0) / math.sqrt(n1), F32).astype(BF16)
    ang = 2.0 * np.pi * (np.arange(n2)[:, None] * np.arange(n1)[None, :]) / S
    twc = jnp.asarray(np.repeat(np.cos(ang)[:, :, None], LANE, axis=2), F32)
    tws = jnp.asarray(np.repeat(np.sin(ang)[:, :, None], LANE, axis=2), F32)
    v = pl.pallas_call(
        _fourier_a_kernel,
        out_shape=jax.ShapeDtypeStruct((B, 2, n1, n2, BRANCH), BF16),
        grid=(B, n2 // nb),
        in_specs=[pl.BlockSpec((None, n1, nb, BRANCH), lambda b, j: (b, 0, j, BLK_FOUR)),
                  pl.BlockSpec((BRANCH, 2 * BRANCH), lambda b, j: (0, 0)),
                  pl.BlockSpec((2 * n1, n1), lambda b, j: (0, 0)),
                  pl.BlockSpec((nb, n1, LANE), lambda b, j: (j, 0, 0)),
                  pl.BlockSpec((nb, n1, LANE), lambda b, j: (j, 0, 0))],
        out_specs=pl.BlockSpec((None, 2, n1, nb, BRANCH), lambda b, j: (b, 0, 0, j, 0)),
        compiler_params=_params("parallel", "parallel"),
        name="fourier_a",
    )(z.reshape(B, n1, n2, z.shape[-1]), w0, d1, twc, tws)
    c3, s3 = _dft_cos_sin(n2)
    c3 = jnp.asarray(c3 / math.sqrt(n2), F32).astype(BF16)
    s3 = jnp.asarray(s3 / math.sqrt(n2), F32).astype(BF16)
    f = pl.pallas_call(
        _fourier_b_kernel,
        out_shape=jax.ShapeDtypeStruct((B, n2, n1, BRANCH), BF16),
        grid=(B, n1 // nb),
        in_specs=[pl.BlockSpec((None, 2, nb, n2, BRANCH), lambda b, i: (b, 0, i, 0, 0)),
                  pl.BlockSpec((n2, n2), lambda b, i: (0, 0)),
                  pl.BlockSpec((n2, n2), lambda b, i: (0, 0))],
        out_specs=pl.BlockSpec((None, n2, nb, BRANCH), lambda b, i: (b, 0, i, 0)),
        compiler_params=_params("parallel", "parallel"),
        name="fourier_b",
    )(v, c3, s3)
    return f.reshape(B, S, BRANCH)


def _fourier_short(z, w0):
    B, S, _ = z.shape
    u = _slab_matmul(z, BLK_FOUR, w0)
    ust = jnp.concatenate([u[:, :, :BRANCH], u[:, :, BRANCH:]], axis=1)
    c, s = _dft_cos_sin(S)
    dl = jnp.asarray(np.concatenate([c, s], axis=1) / math.sqrt(S), F32).astype(BF16)
    return _bmm(dl, ust, BRANCH)


def _rope_tables(n_rows):
    half = HEAD_DIM // 4
    inv = ROPE_BASE ** (-np.arange(0, HEAD_DIM // 2, 2, dtype=np.float64) / (HEAD_DIM // 2))
    row = np.repeat(np.arange(n_rows, dtype=np.float64), GRID_W)
    col = np.tile(np.arange(GRID_W, dtype=np.float64), n_rows)
    ang = np.concatenate([row[:, None] * inv, col[:, None] * inv], axis=-1)
    assert ang.shape[1] == 2 * half
    cos, sin = np.cos(ang), np.sin(ang)
    cosf = np.tile(np.concatenate([cos, cos], axis=1), (1, LANE // HEAD_DIM))
    sins = np.tile(np.concatenate([-sin, sin], axis=1), (1, LANE // HEAD_DIM))
    return jnp.asarray(cosf, F32), jnp.asarray(sins, F32)


def _mixer_branches(z, lw, w0, long_seq):
    f = _fourier_long(z, w0) if long_seq else _fourier_short(z, w0)
    cv = _conv_branch(z, lw["conv_w"], lw["conv_b"], lw["conv_ln_g"], lw["conv_ln_b"])
    return f, cv


def kernel(x, c, ctx, c_ctx, w_ada, b_ada, norm_mix, w_in, w_four, conv_w, conv_b, conv_ln_g, conv_ln_b, w_conv, q_norm, k_norm, w_gqa, lam_q1, lam_k1, lam_q2, lam_k2, diff_norm, w_diff, w_out, norm_ffn, w_ffn1, w_ffn3, w_ffn2, final_norm):
    B, S, D = x.shape
    Sc = ctx.shape[1]
    depth = w_ada.shape[0]
    assert B + 1 <= 8 and S % GRID_W == 0 and D == 2 * BRANCH

    cond = jnp.zeros((8, D), F32).at[:B].set(c).at[B].set(c_ctx)
    mods = _ada_mod(cond, w_ada, b_ada)

    rope_x = _rope_tables(S // GRID_W)
    rope_c = (jnp.ones((Sc, LANE), F32), jnp.zeros((Sc, LANE), F32))
    ones_bd = jnp.asarray(np.kron(np.eye(BRANCH // HEAD_DIM), np.ones((HEAD_DIM, HEAD_DIM))), F32).astype(BF16)
    w0 = jnp.asarray(_channel_dft_matrix(), F32).astype(BF16)
    tile2 = lambda v: jnp.tile(v, LANE // HEAD_DIM * (BRANCH // LANE)).reshape(1, BRANCH)

    for l in range(depth):
        last = l == depth - 1
        lam_init = 0.8 - 0.6 * math.exp(-0.3 * l)
        mx = [m.reshape(B, 1, D) for m in jnp.split(mods[l, :B], 6, axis=-1)]
        mc = [jnp.broadcast_to(m.reshape(1, 1, D), (B, 1, D)) for m in jnp.split(mods[l, B], 6, axis=-1)]
        lw = dict(conv_w=conv_w[l], conv_b=conv_b[l].reshape(1, BRANCH),
                  conv_ln_g=conv_ln_g[l].reshape(1, BRANCH), conv_ln_b=conv_ln_b[l].reshape(1, BRANCH),
                  q_norm=tile2(q_norm[l]), k_norm=tile2(k_norm[l]))
        w_z, w_att, w_gate = (w_in[l][:, a:b].astype(BF16) for a, b in ((0, COL_GQ), (COL_GQ, GATE_COL0), (GATE_COL0, None)))
        wf, wc, wa, wd, wo = (w.astype(BF16) for w in (w_four[l], w_conv[l], w_gqa[l], w_diff[l], w_out[l]))
        w1, w3, w2 = (w.astype(BF16) for w in (w_ffn1[l], w_ffn3[l], w_ffn2[l]))
        gain_m, gain_f = norm_mix[l].reshape(1, D), norm_ffn[l].reshape(1, D)
        lam_vecs = [v[l].reshape(1, HEAD_DIM) for v in (lam_q1, lam_k1, lam_q2, lam_k2)]
        dn = diff_norm[l].reshape(1, LANE)
        fin = final_norm.reshape(1, D)

        zx, qgx, kg, vg, qdx, kd, vd = _inproj_qkv(x, gain_m, mx[0], mx[1], w_z, w_att, rope_x[0], rope_x[1],
                                                   lw["q_norm"], lw["k_norm"], ones_bd, _kv_buffers(B, S + Sc), 0)
        zc, qgc, kg, vg, qdc, kd, vd = _inproj_qkv(ctx, gain_m, mc[0], mc[1], w_z, w_att, rope_c[0], rope_c[1],
                                                   lw["q_norm"], lw["k_norm"], ones_bd, (kg, vg, kd, vd), S)
        zgx = _inproj(x, gain_m, mx[0], mx[1], w_gate)
        zgc = _inproj(ctx, gain_m, mc[0], mc[1], w_gate)
        fx, cvx = _mixer_branches(zx, lw, w0, True)
        fc, cvc = _mixer_branches(zc, lw, w0, False)

        ogx = _attention(qgx, kg, vg, "gqa", 0, S + Sc)
        odx = _attention(qdx, kd, vd, "diff", 0, S + Sc, lam_vecs, dn, lam_init)
        x = _merge(x, mx[2], fx, cvx, ogx, odx, zgx, wf, wc, wa, wd, wo)
        if not last:
            ogc = _attention(qgc, kg, vg, "gqa", S, Sc)
            odc = _attention(qdc, kd, vd, "diff", S, Sc, lam_vecs, dn, lam_init)
            ctx = _merge(ctx, mc[2], fc, cvc, ogc, odc, zgc, wf, wc, wa, wd, wo)
            ctx = _ffn(ctx, gain_f, mc[3], mc[4], mc[5], w1, w3, w2, fin, False)
        x = _ffn(x, gain_f, mx[3], mx[4], mx[5], w1, w3, w2, fin, last)
    return x
```

```python
import functools
import math

import numpy as np
import jax
import jax.numpy as jnp
from jax import lax
from jax.experimental import pallas as pl
from jax.experimental.pallas import tpu as pltpu

F32 = jnp.float32
BF16 = jnp.bfloat16

HEAD_DIM = 64
BRANCH = 512
GQA_GROUP = 4
GQA_KV_HEADS = 2
DIFF_VT_ROWS = 144
GQA_VT_ROWS = 80
DIFF_HEADS = 4
N_BRANCH = 4
GRID_W = 64
CONV_K = 31
ROPE_BASE = 10000.0
EPS = 1e-6
LN_EPS = 1e-5

LANE = 128
SUBLANE = 8
VMEM_LIMIT = 56 * 1024 * 1024

BLK_FOUR, BLK_CA, BLK_CG = range(3)
COL_GQ, COL_KV, COL_DQ, COL_DK, COL_DV, GATE_COL0 = 1536, 2048, 2304, 2816, 3328, 3840
FOURIER_NB = 16
CONV_HALO = 16


def _params(*sem):
    return pltpu.CompilerParams(dimension_semantics=sem, vmem_limit_bytes=VMEM_LIMIT)


def _silu(x):
    return x * jax.nn.sigmoid(x)


def _rms_mod(x, g, shift, scale):
    y = x * lax.rsqrt(jnp.mean(x * x, axis=-1, keepdims=True) + EPS) * g
    return y * (1.0 + scale) + shift


def _ada_kernel(c_ref, w_ref, b_ref, o_ref):
    a = _silu(c_ref[...])
    o_ref[...] = jnp.dot(a, w_ref[...], precision=lax.Precision.HIGHEST,
                         preferred_element_type=F32) + b_ref[...]


def _ada_mod(cond, w_ada, b_ada):
    L, D, N = w_ada.shape
    tn = 1536
    return pl.pallas_call(
        _ada_kernel,
        out_shape=jax.ShapeDtypeStruct((L, 8, N), F32),
        grid=(L, N // tn),
        in_specs=[pl.BlockSpec((8, D), lambda l, j: (0, 0)),
                  pl.BlockSpec((None, D, tn), lambda l, j: (l, 0, j)),
                  pl.BlockSpec((None, 1, tn), lambda l, j: (l, 0, j))],
        out_specs=pl.BlockSpec((None, 8, tn), lambda l, j: (l, 0, j)),
        compiler_params=_params("parallel", "parallel"),
        name="ada_mod",
    )(cond, w_ada, b_ada.reshape(L, 1, N))


def _inproj_kernel(x_ref, g_ref, sh_ref, sc_ref, w_ref, o_ref):
    h = _rms_mod(x_ref[...], g_ref[...], sh_ref[...], sc_ref[...])
    o_ref[...] = jnp.dot(h.astype(BF16), w_ref[...], preferred_element_type=F32).astype(o_ref.dtype)


def _inproj(x, gain, shift, scale, w):
    B, S, D = x.shape
    N = w.shape[1]
    tm = min(512, S)
    tn = 4096
    return pl.pallas_call(
        _inproj_kernel,
        out_shape=jax.ShapeDtypeStruct((B, S, N), BF16),
        grid=(N // tn, B, S // tm),
        in_specs=[pl.BlockSpec((None, tm, D), lambda j, b, i: (b, i, 0)),
                  pl.BlockSpec((1, D), lambda j, b, i: (0, 0)),
                  pl.BlockSpec((None, 1, D), lambda j, b, i: (b, 0, 0)),
                  pl.BlockSpec((None, 1, D), lambda j, b, i: (b, 0, 0)),
                  pl.BlockSpec((D, tn), lambda j, b, i: (0, j))],
        out_specs=pl.BlockSpec((None, tm, tn), lambda j, b, i: (b, i, j)),
        compiler_params=_params("parallel", "parallel", "parallel"),
        name="in_proj",
    )(x, gain, shift, scale, w)


def _inproj_qkv_kernel(x_ref, g_ref, sh_ref, sc_ref, wz_ref, wa_ref, cos_ref, sin_ref, qn_ref, kn_ref, bd_ref, *rest):
    z_ref, qg_ref, kg_ref, vg_ref, qd_ref, kd_ref, vd_ref = rest[-7:]
    h = _rms_mod(x_ref[...], g_ref[...], sh_ref[...], sc_ref[...]).astype(BF16)
    acc = jnp.dot(h, wa_ref[...], preferred_element_type=F32)
    slab = lambda col, width=BRANCH: acc[:, col - COL_GQ:col - COL_GQ + width]
    _qkv_store(slab(COL_GQ), slab(COL_DQ), slab(COL_DK), slab(COL_DV), slab(COL_KV, 2 * LANE),
               cos_ref[...], sin_ref[...], qn_ref[...], kn_ref[...], bd_ref[...],
               qg_ref, kg_ref, vg_ref, qd_ref, kd_ref, vd_ref)
    z_ref[...] = jnp.dot(h, wz_ref[...], preferred_element_type=F32).astype(z_ref.dtype)


def _kv_buffers(B, n_keys):
    vg_rows, vd_rows = GQA_KV_HEADS * GQA_VT_ROWS, DIFF_HEADS * DIFF_VT_ROWS
    shapes = ((B, n_keys, LANE), (B, vg_rows, n_keys), (B, n_keys, BRANCH), (B, vd_rows, n_keys))
    return tuple(jnp.zeros(s, BF16) for s in shapes)


def _inproj_qkv(x, gain, shift, scale, w_z, w_att, cosf, sins, qn, kn, ones_bd, kv_bufs, key_start):
    B, S, D = x.shape
    n_keys = kv_bufs[0].shape[1]
    tm = min(512, S)
    assert key_start % tm == 0
    k0 = key_start // tm
    const = lambda shape: pl.BlockSpec(shape, lambda b, i: (0, 0))
    per_b = pl.BlockSpec((None, 1, D), lambda b, i: (b, 0, 0))
    tab = pl.BlockSpec((tm, LANE), lambda b, i: (i, 0))
    rows = lambda width: pl.BlockSpec((None, tm, width), lambda b, i: (b, i, 0))
    q_cols = lambda chans: pl.BlockSpec((None, chans, tm), lambda b, i: (b, 0, i))
    k_rows = lambda width: pl.BlockSpec((None, tm, width), lambda b, i: (b, k0 + i, 0))
    v_cols = lambda chans: pl.BlockSpec((None, chans, tm), lambda b, i: (b, 0, k0 + i))
    vg_rows, vd_rows = GQA_KV_HEADS * GQA_VT_ROWS, DIFF_HEADS * DIFF_VT_ROWS
    shapes = [(B, S, 3 * BRANCH), (B, 2 * BRANCH, S), (B, n_keys, LANE), (B, vg_rows, n_keys),
              (B, 2 * BRANCH, S), (B, n_keys, BRANCH), (B, vd_rows, n_keys)]
    specs = [rows(3 * BRANCH), q_cols(2 * BRANCH), k_rows(LANE), v_cols(vg_rows),
             q_cols(2 * BRANCH), k_rows(BRANCH), v_cols(vd_rows)]
    in_specs = [rows(D), const((1, D)), per_b, per_b, const(w_z.shape), const(w_att.shape), tab, tab,
                const((1, BRANCH)), const((1, BRANCH)), const((BRANCH, BRANCH))]
    args = [x, gain, shift, scale, w_z, w_att, cosf, sins, qn, kn, ones_bd]
    aliases = {}
    for out_idx, buf in zip((2, 3, 5, 6), kv_bufs):
        aliases[len(args)] = out_idx
        in_specs.append(pl.BlockSpec(memory_space=pl.ANY))
        args.append(buf)
    return pl.pallas_call(
        _inproj_qkv_kernel,
        out_shape=[jax.ShapeDtypeStruct(s, BF16) for s in shapes],
        grid=(B, S // tm),
        in_specs=in_specs,
        out_specs=specs,
        input_output_aliases=aliases,
        compiler_params=_params("parallel", "parallel"),
        name="in_proj_qkv",
    )(*args)


def _ffn_kernel(x_ref, g_ref, sh_ref, sc_ref, gate_ref, w1_ref, w3_ref, w2_ref, fg_ref, o_ref, *, final):
    x = x_ref[...]
    h = _rms_mod(x, g_ref[...], sh_ref[...], sc_ref[...]).astype(BF16)
    a = jnp.dot(h, w1_ref[...], preferred_element_type=F32)
    b = jnp.dot(h, w3_ref[...], preferred_element_type=F32)
    u = (_silu(a) * b).astype(BF16)
    y = x + gate_ref[...] * jnp.dot(u, w2_ref[...], preferred_element_type=F32)
    if final:
        y = y * lax.rsqrt(jnp.mean(y * y, axis=-1, keepdims=True) + EPS) * fg_ref[...]
    o_ref[...] = y


def _ffn(x, gain, shift, scale, gate, w1, w3, w2, final_gain, final):
    B, S, D = x.shape
    H = w1.shape[1]
    tm = min(512, S)
    const = lambda shape: pl.BlockSpec(shape, lambda b, i: (0,) * len(shape), pipeline_mode=pl.Buffered(1))
    per_b = pl.BlockSpec((None, 1, D), lambda b, i: (b, 0, 0))
    return pl.pallas_call(
        functools.partial(_ffn_kernel, final=final),
        out_shape=jax.ShapeDtypeStruct((B, S, D), F32),
        grid=(B, S // tm),
        in_specs=[pl.BlockSpec((None, tm, D), lambda b, i: (b, i, 0)),
                  const((1, D)), per_b, per_b, per_b,
                  const((D, H)), const((D, H)), const((H, D)), const((1, D))],
        out_specs=pl.BlockSpec((None, tm, D), lambda b, i: (b, i, 0)),
        compiler_params=_params("parallel", "parallel"),
        name="ffn",
    )(x, gain, shift, scale, gate, w1, w3, w2, final_gain)


def _merge_kernel(x_ref, gm_ref, f_ref, c_ref, a_ref, d_ref, zg_ref,
                  wf_ref, wc_ref, wa_ref, wd_ref, wo_ref, o_ref):
    D = x_ref.shape[-1]
    acc = None
    for b, (br, w) in enumerate(((f_ref, wf_ref), (c_ref, wc_ref), (a_ref, wa_ref), (d_ref, wd_ref))):
        y = jnp.dot(br[...], w[...], preferred_element_type=F32)
        gate = jax.nn.sigmoid(zg_ref[:, b * D:(b + 1) * D].astype(F32))
        acc = gate * y if acc is None else acc + gate * y
    out = jnp.dot(acc.astype(BF16), wo_ref[...], preferred_element_type=F32)
    o_ref[...] = x_ref[...] + gm_ref[...] * out


def _merge(x, gm, f, cv, og, od, zg, wf, wc, wa, wd, wo):
    B, S, D = x.shape
    tm = min(512, S)
    row = lambda width: pl.BlockSpec((None, tm, width), lambda b, i: (b, i, 0))
    const = lambda shape: pl.BlockSpec(shape, lambda b, i: (0, 0))
    return pl.pallas_call(
        _merge_kernel,
        out_shape=jax.ShapeDtypeStruct((B, S, D), F32),
        grid=(B, S // tm),
        in_specs=[row(D), pl.BlockSpec((None, 1, D), lambda b, i: (b, 0, 0)),
                  row(BRANCH), row(BRANCH), row(BRANCH), row(BRANCH),
                  pl.BlockSpec((None, tm, N_BRANCH * D), lambda b, i: (b, i, 0)),
                  const((BRANCH, D)), const((BRANCH, D)), const((BRANCH, D)), const((BRANCH, D)),
                  const((D, D))],
        out_specs=row(D),
        compiler_params=_params("parallel", "parallel"),
        name="merge",
    )(x, gm, f, cv, og, od, zg, wf, wc, wa, wd, wo)


def _conv_kernel(a_ref, g_ref, ap_ref, gp_ref, an_ref, gn_ref, w_ref, b_ref, lg_ref, lb_ref, o_ref, u_s,
                 *, ts, chunk):
    i = pl.program_id(1)
    n = pl.num_programs(1)

    def glu(a, g):
        return a.astype(F32) * jax.nn.sigmoid(g.astype(F32))

    u_s[CONV_HALO:CONV_HALO + ts, :] = glu(a_ref[...], g_ref[...])
    u_s[0:CONV_HALO, :] = jnp.where(i > 0, glu(ap_ref[...], gp_ref[...]), 0.0)
    u_s[CONV_HALO + ts:2 * CONV_HALO + ts, :] = jnp.where(i < n - 1, glu(an_ref[...], gn_ref[...]), 0.0)
    pad = CONV_K // 2

    def body(c, carry):
        r0 = pl.multiple_of(c * chunk, chunk)
        win = u_s[pl.ds(r0, chunk + 2 * CONV_HALO), :]
        rows = chunk + 2 * CONV_HALO
        acc = jnp.zeros((chunk // SUBLANE, SUBLANE, BRANCH), F32) + b_ref[...]
        for r in range(SUBLANE):
            taps = [j for j in range(CONV_K) if (CONV_HALO - pad + j) % SUBLANE == r]
            shifted = win if r == 0 else pltpu.roll(win, rows - r, 0)
            for j in taps:
                base = CONV_HALO - pad + j - r
                tap = shifted[base:base + chunk, :].reshape(chunk // SUBLANE, SUBLANE, BRANCH)
                acc = acc + w_ref[j] * tap
        acc = acc.reshape(chunk, BRANCH)
        mu = jnp.mean(acc, axis=-1, keepdims=True)
        d = acc - mu
        var = jnp.mean(d * d, axis=-1, keepdims=True)
        y = d * lax.rsqrt(var + LN_EPS) * lg_ref[...] + lb_ref[...]
        o_ref[pl.ds(r0, chunk), :] = _silu(y).astype(o_ref.dtype)
        return carry

    lax.fori_loop(0, ts // chunk, body, 0)


def _conv_branch(z, w, b, ln_g, ln_b):
    B, S, _ = z.shape
    ts = min(512, S)
    chunk = min(256, ts)
    hb = ts // CONV_HALO
    last = S // CONV_HALO - 1
    cur = lambda blk: pl.BlockSpec((None, ts, BRANCH), lambda b_, i: (b_, i, blk))
    prev = lambda blk: pl.BlockSpec((None, CONV_HALO, BRANCH),
                                    lambda b_, i: (b_, jnp.maximum(i * hb - 1, 0), blk))
    nxt = lambda blk: pl.BlockSpec((None, CONV_HALO, BRANCH),
                                   lambda b_, i: (b_, jnp.minimum((i + 1) * hb, last), blk))
    const = lambda shape: pl.BlockSpec(shape, lambda b_, i: (0, 0))
    return pl.pallas_call(
        functools.partial(_conv_kernel, ts=ts, chunk=chunk),
        out_shape=jax.ShapeDtypeStruct((B, S, BRANCH), BF16),
        grid=(B, S // ts),
        in_specs=[cur(BLK_CA), cur(BLK_CG), prev(BLK_CA), prev(BLK_CG), nxt(BLK_CA), nxt(BLK_CG),
                  pl.BlockSpec((CONV_K, SUBLANE, BRANCH), lambda b_, i: (0, 0, 0)),
                  const((1, BRANCH)), const((1, BRANCH)), const((1, BRANCH))],
        out_specs=pl.BlockSpec((None, ts, BRANCH), lambda b_, i: (b_, i, 0)),
        scratch_shapes=[pltpu.VMEM((ts + 2 * CONV_HALO, BRANCH), F32)],
        compiler_params=_params("parallel", "parallel"),
        name="conv_branch",
    )(z, z, z, z, z, z, jnp.broadcast_to(w[:, None, :], (CONV_K, SUBLANE, BRANCH)), b, ln_g, ln_b)


def _head_rms(x, ones_bd):
    x2 = x * x
    hi = x2.astype(BF16)
    lo = (x2 - hi.astype(F32)).astype(BF16)
    ssum = (jnp.dot(hi, ones_bd, preferred_element_type=F32)
            + jnp.dot(lo, ones_bd, preferred_element_type=F32))
    return x * lax.rsqrt(ssum * (1.0 / HEAD_DIM) + EPS)


def _rope(x, cosf, sins):
    width = x.shape[1]
    reps = width // LANE
    c = jnp.concatenate([cosf] * reps, axis=1) if reps > 1 else cosf
    s = jnp.concatenate([sins] * reps, axis=1) if reps > 1 else sins
    lane = lax.broadcasted_iota(jnp.int32, x.shape, 1)
    first_half = (lane & (HEAD_DIM - 1)) < HEAD_DIM // 2
    partner = jnp.where(first_half, pltpu.roll(x, width - HEAD_DIM // 2, 1), pltpu.roll(x, HEAD_DIM // 2, 1))
    return x * c + partner * s


def _qkv_store(gq, dq, dk, dv, kvz, cosf, sins, qn, kn, bd, qg_ref, kg_ref, vg_ref, qd_ref, kd_ref, vd_ref):
    scale = HEAD_DIM ** -0.5 * math.log2(math.e)
    ts = gq.shape[0]

    half = jnp.zeros((HEAD_DIM, ts), qg_ref.dtype)
    q = _rope(_head_rms(gq, bd) * qn, cosf, sins) * scale
    for p in range(BRANCH // LANE):
        qt = q[:, p * LANE:(p + 1) * LANE].T
        for e in range(2):
            h = 2 * p + e
            kvh = h // GQA_GROUP
            base = h * LANE
            qg_ref[base + kvh * HEAD_DIM:base + (kvh + 1) * HEAD_DIM, :] = (
                qt[e * HEAD_DIM:(e + 1) * HEAD_DIM, :].astype(qg_ref.dtype))
            qg_ref[base + (1 - kvh) * HEAD_DIM:base + (2 - kvh) * HEAD_DIM, :] = half

    k = _rope(_head_rms(kvz[:, :LANE], bd[:LANE, :LANE]) * kn[:, :LANE], cosf, sins)
    kg_ref[...] = k.astype(kg_ref.dtype)
    vt = kvz[:, LANE:2 * LANE].T
    row = lax.broadcasted_iota(jnp.int32, (GQA_VT_ROWS - HEAD_DIM, ts), 0)
    tail = jnp.where(row == 0, 1.0, 0.0).astype(vg_ref.dtype)
    for g in range(GQA_KV_HEADS):
        vg_ref[g * GQA_VT_ROWS:g * GQA_VT_ROWS + HEAD_DIM, :] = vt[g * HEAD_DIM:(g + 1) * HEAD_DIM, :].astype(vg_ref.dtype)
        vg_ref[g * GQA_VT_ROWS + HEAD_DIM:(g + 1) * GQA_VT_ROWS, :] = tail

    qd = _rope(dq, cosf, sins) * scale
    for h in range(DIFF_HEADS):
        qt = qd[:, h * LANE:(h + 1) * LANE].T.astype(qd_ref.dtype)
        base = 2 * h * LANE
        qd_ref[base:base + HEAD_DIM, :] = qt[:HEAD_DIM, :]
        qd_ref[base + HEAD_DIM:base + LANE, :] = half
        qd_ref[base + LANE:base + LANE + HEAD_DIM, :] = half
        qd_ref[base + LANE + HEAD_DIM:base + 2 * LANE, :] = qt[HEAD_DIM:, :]
        vd_ref[h * DIFF_VT_ROWS:h * DIFF_VT_ROWS + LANE, :] = (
            dv[:, h * LANE:(h + 1) * LANE].T.astype(vd_ref.dtype))
        vd_ref[h * DIFF_VT_ROWS + LANE:(h + 1) * DIFF_VT_ROWS, :] = tail
    kd_ref[...] = _rope(dk, cosf, sins).astype(kd_ref.dtype)


def _attn_kernel(*refs, sets, tq, tk, mode, lam_init):
    if mode == "diff":
        q_ref, k_ref, v_ref, l1q, l1k, l2q, l2k, dn_ref, o_ref, q_s, acc_s = refs
    else:
        q_ref, k_ref, v_ref, o_ref, q_s, acc_s = refs
    n_groups, vrows = acc_s.shape[0], acc_s.shape[1]
    n_tiles = k_ref.shape[0] // tk
    width = sets * tq
    for g in range(n_groups):
        for r in range(sets):
            q_s[g, :, r * tq:(r + 1) * tq] = q_ref[(g * sets + r) * LANE:(g * sets + r + 1) * LANE, :]

    def k_tile(g, rows):
        return k_ref[rows, :] if mode == "gqa" else k_ref[rows, g * LANE:(g + 1) * LANE]

    def finish(g):
        if mode == "diff":
            ot = acc_s[g, :LANE, :] / acc_s[g, LANE:LANE + 1, :]
            o = [ot[:, r * tq:(r + 1) * tq].T for r in range(sets)]
            lam = (jnp.exp(jnp.sum(l1q[...] * l1k[...], axis=1, keepdims=True))
                   - jnp.exp(jnp.sum(l2q[...] * l2k[...], axis=1, keepdims=True)) + lam_init)
            d = o[0] - lam * o[1]
            d = d * lax.rsqrt(jnp.mean(d * d, axis=1, keepdims=True) + EPS) * dn_ref[...]
            o_ref[:, g * LANE:(g + 1) * LANE] = (d * (1.0 - lam_init)).astype(o_ref.dtype)
        else:
            ot = acc_s[g, :HEAD_DIM, :] / acc_s[g, HEAD_DIM:HEAD_DIM + 1, :]
            for pair in range(sets // 2):
                both = jnp.concatenate([ot[:, (2 * pair) * tq:(2 * pair + 1) * tq],
                                        ot[:, (2 * pair + 1) * tq:(2 * pair + 2) * tq]], axis=0)
                col = (g * (sets // 2) + pair) * LANE
                o_ref[:, col:col + LANE] = both.T.astype(o_ref.dtype)

    m_run = [None] * n_groups
    for t in range(n_tiles):
        for g in range(n_groups):
            st = jnp.dot(k_tile(g, slice(t * tk, (t + 1) * tk)), q_s[g], preferred_element_type=F32)
            m_tile = jnp.max(st, axis=0, keepdims=True)
            pt = jnp.exp2(st - (m_tile if t == 0 else m_run[g]))
            pv = jnp.dot(v_ref[g * vrows:(g + 1) * vrows, t * tk:(t + 1) * tk], pt.astype(BF16),
                         preferred_element_type=F32)
            if t == 0:
                m_run[g] = m_tile
                acc_s[g] = pv
            else:
                m_new = jnp.maximum(m_run[g], m_tile)
                alpha = jnp.exp2(m_run[g] - m_new)
                acc_s[g] = (acc_s[g] + pv) * alpha
                m_run[g] = m_new
    for g in range(n_groups):
        finish(g)
    overflowed = jnp.max(jnp.where(jnp.isfinite(acc_s[...]), 0.0, 1.0)) > 0.0

    @pl.when(overflowed)
    def _exact():
        for g in range(n_groups):
            def body(t, m_old, g=g):
                st = jnp.dot(k_tile(g, pl.ds(pl.multiple_of(t * tk, tk), tk)), q_s[g], preferred_element_type=F32)
                m_new = jnp.maximum(m_old, jnp.max(st, axis=0, keepdims=True))
                alpha = jnp.exp2(m_old - m_new)
                pt = jnp.exp2(st - m_new)
                v_tile = v_ref[g * vrows:(g + 1) * vrows, pl.ds(pl.multiple_of(t * tk, LANE), tk)]
                acc_s[g] = alpha * acc_s[g] + jnp.dot(v_tile, pt.astype(BF16), preferred_element_type=F32)
                return m_new

            acc_s[g] = jnp.zeros(acc_s.shape[1:], F32)
            lax.fori_loop(0, n_tiles, body, jnp.full((1, width), -jnp.inf, F32))
            finish(g)


def _attention(qt, k, vt, mode, key_start, NK, lam_vecs=None, diff_norm=None, lam_init=0.0):
    B, _, S = qt.shape
    assert key_start % NK == 0
    kb = key_start // NK
    tq = min(512 if mode == "diff" else 256, S)
    tk = 768
    tk = tk if NK % tk == 0 else 256
    per_step = 2
    if mode == "diff":
        groups, sets, out_w, vrows = DIFF_HEADS, 2, LANE, DIFF_VT_ROWS
        k_spec = pl.BlockSpec((None, NK, per_step * LANE), lambda b, g, i: (b, kb, g))
    else:
        groups, sets, out_w, vrows = GQA_KV_HEADS, GQA_GROUP, GQA_GROUP * HEAD_DIM, GQA_VT_ROWS
        k_spec = pl.BlockSpec((None, NK, LANE), lambda b, g, i: (b, kb, 0))
    in_specs = [pl.BlockSpec((None, per_step * sets * LANE, tq), lambda b, g, i: (b, g, i)),
                k_spec,
                pl.BlockSpec((None, per_step * vrows, NK), lambda b, g, i: (b, g, kb))]
    args = [qt, k, vt]
    if mode == "diff":
        in_specs += [pl.BlockSpec((1, HEAD_DIM), lambda b, g, i: (0, 0))] * 4
        in_specs += [pl.BlockSpec((1, LANE), lambda b, g, i: (0, 0))]
        args += list(lam_vecs) + [diff_norm]
    return pl.pallas_call(
        functools.partial(_attn_kernel, sets=sets, tq=tq, tk=tk, mode=mode, lam_init=lam_init),
        out_shape=jax.ShapeDtypeStruct((B, S, BRANCH), BF16),
        grid=(B, groups // per_step, S // tq),
        in_specs=in_specs,
        out_specs=pl.BlockSpec((None, tq, per_step * out_w), lambda b, g, i: (b, i, g)),
        scratch_shapes=[pltpu.VMEM((per_step, LANE, sets * tq), BF16),
                        pltpu.VMEM((per_step, vrows, sets * tq), F32)],
        compiler_params=_params("parallel", "parallel", "parallel"),
        name="attn_" + mode,
    )(*args)


def _dft_cos_sin(n):
    idx = np.arange(n)
    ang = 2.0 * np.pi * ((idx[:, None] * idx[None, :]) % n) / n
    return np.cos(ang), np.sin(ang)


def _channel_dft_matrix():
    c, s = _dft_cos_sin(GRID_W)
    groups = BRANCH // GRID_W
    eye = np.eye(groups)
    return np.concatenate([np.kron(eye, c), -np.kron(eye, s)], axis=1) / math.sqrt(GRID_W)


def _fourier_a_kernel(z_ref, w0_ref, d1_ref, tc_ref, ts_ref, o_ref):
    n1, nb, _ = z_ref.shape
    zt = pltpu.einshape("kjm->jkm", z_ref[...])
    u = jnp.dot(zt.reshape(nb * n1, BRANCH), w0_ref[...], preferred_element_type=F32)
    u = u.astype(BF16).reshape(nb, n1, 2 * BRANCH)
    reps = BRANCH // LANE
    out_r, out_i = [], []
    for j in range(nb):
        p = jnp.dot(d1_ref[...], u[j], preferred_element_type=F32)
        vr = p[:n1, :BRANCH] + p[n1:, BRANCH:]
        vi = p[:n1, BRANCH:] - p[n1:, :BRANCH]
        c = jnp.concatenate([tc_ref[j]] * reps, axis=1)
        s = jnp.concatenate([ts_ref[j]] * reps, axis=1)
        out_r.append(vr * c + vi * s)
        out_i.append(vi * c - vr * s)
    o_ref[0] = pltpu.einshape("jkm->kjm", jnp.stack(out_r).astype(o_ref.dtype))
    o_ref[1] = pltpu.einshape("jkm->kjm", jnp.stack(out_i).astype(o_ref.dtype))


def _fourier_b_kernel(v_ref, c3_ref, s3_ref, o_ref):
    nb = v_ref.shape[1]
    outs = [jnp.dot(c3_ref[...], v_ref[0, j], preferred_element_type=F32)
            + jnp.dot(s3_ref[...], v_ref[1, j], preferred_element_type=F32) for j in range(nb)]
    o_ref[...] = pltpu.einshape("jkm->kjm", jnp.stack(outs).astype(o_ref.dtype))


def _bmm_kernel(a_ref, x_ref, o_ref):
    o_ref[...] = jnp.dot(a_ref[...], x_ref[...], preferred_element_type=F32).astype(o_ref.dtype)


def _bmm(a, x, tn):
    M, K = a.shape
    B, _, N = x.shape
    tn = min(tn, N)
    return pl.pallas_call(
        _bmm_kernel,
        out_shape=jax.ShapeDtypeStruct((B, M, N), BF16),
        grid=(B, N // tn),
        in_specs=[pl.BlockSpec((M, K), lambda b, j: (0, 0)),
                  pl.BlockSpec((None, K, tn), lambda b, j: (b, 0, j))],
        out_specs=pl.BlockSpec((None, M, tn), lambda b, j: (b, 0, j)),
        compiler_params=_params("parallel", "parallel"),
        name="bmm",
    )(a, x)


def _slab_matmul_kernel(z_ref, w_ref, o_ref):
    o_ref[...] = jnp.dot(z_ref[...], w_ref[...], preferred_element_type=F32).astype(o_ref.dtype)


def _slab_matmul(z, blk, w):
    B, S, _ = z.shape
    K, N = w.shape
    ts = min(256, S)
    return pl.pallas_call(
        _slab_matmul_kernel,
        out_shape=jax.ShapeDtypeStruct((B, S, N), BF16),
        grid=(B, S // ts),
        in_specs=[pl.BlockSpec((None, ts, K), lambda b, i: (b, i, blk)),
                  pl.BlockSpec((K, N), lambda b, i: (0, 0))],
        out_specs=pl.BlockSpec((None, ts, N), lambda b, i: (b, i, 0)),
        compiler_params=_params("parallel", "parallel"),
        name="slab_matmul",
    )(z, w)


def _fourier_long(z, w0):
    B, S, _ = z.shape
    n2 = GRID_W
    n1 = S // n2
    nb = FOURIER_NB
    assert n1 % nb == 0 and n2 % nb == 0
    c1, s1 = _dft_cos_sin(n1)
    d1 = jnp.asarray(np.concatenate([c1, s1], axis=0) / math.sqrt(n1), F32).astype(BF16)
    ang = 2.0 * np.pi * (np.arange(n2)[:, None] * np.arange(n1)[None, :]) / S
    twc = jnp.asarray(np.repeat(np.cos(ang)[:, :, None], LANE, axis=2), F32)
    tws = jnp.asarray(np.repeat(np.sin(ang)[:, :, None], LANE, axis=2), F32)
    v = pl.pallas_call(
        _fourier_a_kernel,
        out_shape=jax.ShapeDtypeStruct((B, 2, n1, n2, BRANCH), BF16),
        grid=(B, n2 // nb),
        in_specs=[pl.BlockSpec((None, n1, nb, BRANCH), lambda b, j: (b, 0, j, BLK_FOUR)),
                  pl.BlockSpec((BRANCH, 2 * BRANCH), lambda b, j: (0, 0)),
                  pl.BlockSpec((2 * n1, n1), lambda b, j: (0, 0)),
                  pl.BlockSpec((nb, n1, LANE), lambda b, j: (j, 0, 0)),
                  pl.BlockSpec((nb, n1, LANE), lambda b, j: (j, 0, 0))],
        out_specs=pl.BlockSpec((None, 2, n1, nb, BRANCH), lambda b, j: (b, 0, 0, j, 0)),
        compiler_params=_params("parallel", "parallel"),
        name="fourier_a",
    )(z.reshape(B, n1, n2, z.shape[-1]), w0, d1, twc, tws)
    c3, s3 = _dft_cos_sin(n2)
    c3 = jnp.asarray(c3 / math.sqrt(n2), F32).astype(BF16)
    s3 = jnp.asarray(s3 / math.sqrt(n2), F32).astype(BF16)
    f = pl.pallas_call(
        _fourier_b_kernel,
        out_shape=jax.ShapeDtypeStruct((B, n2, n1, BRANCH), BF16),
        grid=(B, n1 // nb),
        in_specs=[pl.BlockSpec((None, 2, nb, n2, BRANCH), lambda b, i: (b, 0, i, 0, 0)),
                  pl.BlockSpec((n2, n2), lambda b, i: (0, 0)),
                  pl.BlockSpec((n2, n2), lambda b, i: (0, 0))],
        out_specs=pl.BlockSpec((None, n2, nb, BRANCH), lambda b, i: (b, 0, i, 0)),
        compiler_params=_params("parallel", "parallel"),
        name="fourier_b",
    )(v, c3, s3)
    return f.reshape(B, S, BRANCH)


def _fourier_short(z, w0):
    B, S, _ = z.shape
    u = _slab_matmul(z, BLK_FOUR, w0)
    ust = jnp.concatenate([u[:, :, :BRANCH], u[:, :, BRANCH:]], axis=1)
    c, s = _dft_cos_sin(S)
    dl = jnp.asarray(np.concatenate([c, s], axis=1) / math.sqrt(S), F32).astype(BF16)
    return _bmm(dl, ust, BRANCH)


def _rope_tables(n_rows):
    half = HEAD_DIM // 4
    inv = ROPE_BASE ** (-np.arange(0, HEAD_DIM // 2, 2, dtype=np.float64) / (HEAD_DIM // 2))
    row = np.repeat(np.arange(n_rows, dtype=np.float64), GRID_W)
    col = np.tile(np.arange(GRID_W, dtype=np.float64), n_rows)
    ang = np.concatenate([row[:, None] * inv, col[:, None] * inv], axis=-1)
    assert ang.shape[1] == 2 * half
    cos, sin = np.cos(ang), np.sin(ang)
    cosf = np.tile(np.concatenate([cos, cos], axis=1), (1, LANE // HEAD_DIM))
    sins = np.tile(np.concatenate([-sin, sin], axis=1), (1, LANE // HEAD_DIM))
    return jnp.asarray(cosf, F32), jnp.asarray(sins, F32)


def _mixer_branches(z, lw, w0, long_seq):
    f = _fourier_long(z, w0) if long_seq else _fourier_short(z, w0)
    cv = _conv_branch(z, lw["conv_w"], lw["conv_b"], lw["conv_ln_g"], lw["conv_ln_b"])
    return f, cv


def kernel(x, c, ctx, c_ctx, w_ada, b_ada, norm_mix, w_in, w_four, conv_w, conv_b, conv_ln_g, conv_ln_b, w_conv, q_norm, k_norm, w_gqa, lam_q1, lam_k1, lam_q2, lam_k2, diff_norm, w_diff, w_out, norm_ffn, w_ffn1, w_ffn3, w_ffn2, final_norm):
    B, S, D = x.shape
    Sc = ctx.shape[1]
    depth = w_ada.shape[0]
    assert B + 1 <= 8 and S % GRID_W == 0 and D == 2 * BRANCH

    cond = jnp.zeros((8, D), F32).at[:B].set(c).at[B].set(c_ctx)
    mods = _ada_mod(cond, w_ada, b_ada)

    rope_x = _rope_tables(S // GRID_W)
    rope_c = (jnp.ones((Sc, LANE), F32), jnp.zeros((Sc, LANE), F32))
    ones_bd = jnp.asarray(np.kron(np.eye(BRANCH // HEAD_DIM), np.ones((HEAD_DIM, HEAD_DIM))), F32).astype(BF16)
    w0 = jnp.asarray(_channel_dft_matrix(), F32).astype(BF16)
    tile2 = lambda v: jnp.tile(v, LANE // HEAD_DIM * (BRANCH // LANE)).reshape(1, BRANCH)

    for l in range(depth):
        last = l == depth - 1
        lam_init = 0.8 - 0.6 * math.exp(-0.3 * l)
        mx = [m.reshape(B, 1, D) for m in jnp.split(mods[l, :B], 6, axis=-1)]
        mc = [jnp.broadcast_to(m.reshape(1, 1, D), (B, 1, D)) for m in jnp.split(mods[l, B], 6, axis=-1)]
        lw = dict(conv_w=conv_w[l], conv_b=conv_b[l].reshape(1, BRANCH),
                  conv_ln_g=conv_ln_g[l].reshape(1, BRANCH), conv_ln_b=conv_ln_b[l].reshape(1, BRANCH),
                  q_norm=tile2(q_norm[l]), k_norm=tile2(k_norm[l]))
        w_z, w_att, w_gate = (w_in[l][:, a:b].astype(BF16) for a, b in ((0, COL_GQ), (COL_GQ, GATE_COL0), (GATE_COL0, None)))
        wf, wc, wa, wd, wo = (w.astype(BF16) for w in (w_four[l], w_conv[l], w_gqa[l], w_diff[l], w_out[l]))
        w1, w3, w2 = (w.astype(BF16) for w in (w_ffn1[l], w_ffn3[l], w_ffn2[l]))
        gain_m, gain_f = norm_mix[l].reshape(1, D), norm_ffn[l].reshape(1, D)
        lam_vecs = [v[l].reshape(1, HEAD_DIM) for v in (lam_q1, lam_k1, lam_q2, lam_k2)]
        dn = diff_norm[l].reshape(1, LANE)
        fin = final_norm.reshape(1, D)

        zx, qgx, kg, vg, qdx, kd, vd = _inproj_qkv(x, gain_m, mx[0], mx[1], w_z, w_att, rope_x[0], rope_x[1],
                                                   lw["q_norm"], lw["k_norm"], ones_bd, _kv_buffers(B, S + Sc), 0)
        zc, qgc, kg, vg, qdc, kd, vd = _inproj_qkv(ctx, gain_m, mc[0], mc[1], w_z, w_att, rope_c[0], rope_c[1],
                                                   lw["q_norm"], lw["k_norm"], ones_bd, (kg, vg, kd, vd), S)
        zgx = _inproj(x, gain_m, mx[0], mx[1], w_gate)
        zgc = _inproj(ctx, gain_m, mc[0], mc[1], w_gate)
        fx, cvx = _mixer_branches(zx, lw, w0, True)
        fc, cvc = _mixer_branches(zc, lw, w0, False)

        ogx = _attention(qgx, kg, vg, "gqa", 0, S + Sc)
        odx = _attention(qdx, kd, vd, "diff", 0, S + Sc, lam_vecs, dn, lam_init)
        x = _merge(x, mx[2], fx, cvx, ogx, odx, zgx, wf, wc, wa, wd, wo)
        if not last:
            ogc = _attention(qgc, kg, vg, "gqa", S, Sc)
            odc = _attention(qdc, kd, vd, "diff", S, Sc, lam_vecs, dn, lam_init)
            ctx = _merge(ctx, mc[2], fc, cvc, ogc, odc, zgc, wf, wc, wa, wd, wo)
            ctx = _ffn(ctx, gain_f, mc[3], mc[4], mc[5], w1, w3, w2, fin, False)
        x = _ffn(x, gain_f, mx[3], mx[4], mx[5], w1, w3, w2, fin, last)
    return x
```

```python
import functools
import math

import numpy as np
import jax
import jax.numpy as jnp
from jax import lax
from jax.experimental import pallas as pl
from jax.experimental.pallas import tpu as pltpu

F32 = jnp.float32
BF16 = jnp.bfloat16

HEAD_DIM = 64
BRANCH = 512
GQA_GROUP = 4
GQA_KV_HEADS = 2
DIFF_VT_ROWS = 144
GQA_VT_ROWS = 80
DIFF_HEADS = 4
N_BRANCH = 4
GRID_W = 64
CONV_K = 31
ROPE_BASE = 10000.0
EPS = 1e-6
LN_EPS = 1e-5

LANE = 128
SUBLANE = 8
VMEM_LIMIT = 56 * 1024 * 1024

BLK_FOUR, BLK_CA, BLK_CG = range(3)
COL_GQ, COL_KV, COL_DQ, COL_DK, COL_DV, GATE_COL0 = 1536, 2048, 2304, 2816, 3328, 3840
FOURIER_NB = 16
CONV_HALO = 16


def _params(*sem):
    return pltpu.CompilerParams(dimension_semantics=sem, vmem_limit_bytes=VMEM_LIMIT)


def _silu(x):
    return x * jax.nn.sigmoid(x)


def _rms_mod(x, g, shift, scale):
    y = x * lax.rsqrt(jnp.mean(x * x, axis=-1, keepdims=True) + EPS) * g
    return y * (1.0 + scale) + shift


def _ada_kernel(c_ref, w_ref, b_ref, o_ref):
    a = _silu(c_ref[...])
    o_ref[...] = jnp.dot(a, w_ref[...], precision=lax.Precision.HIGHEST,
                         preferred_element_type=F32) + b_ref[...]


def _ada_mod(cond, w_ada, b_ada):
    L, D, N = w_ada.shape
    tn = 1536
    return pl.pallas_call(
        _ada_kernel,
        out_shape=jax.ShapeDtypeStruct((L, 8, N), F32),
        grid=(L, N // tn),
        in_specs=[pl.BlockSpec((8, D), lambda l, j: (0, 0)),
                  pl.BlockSpec((None, D, tn), lambda l, j: (l, 0, j)),
                  pl.BlockSpec((None, 1, tn), lambda l, j: (l, 0, j))],
        out_specs=pl.BlockSpec((None, 8, tn), lambda l, j: (l, 0, j)),
        compiler_params=_params("parallel", "parallel"),
        name="ada_mod",
    )(cond, w_ada, b_ada.reshape(L, 1, N))


def _inproj_kernel(x_ref, g_ref, sh_ref, sc_ref, w_ref, o_ref):
    h = _rms_mod(x_ref[...], g_ref[...], sh_ref[...], sc_ref[...])
    o_ref[...] = jnp.dot(h.astype(BF16), w_ref[...], preferred_element_type=F32).astype(o_ref.dtype)


def _inproj(x, gain, shift, scale, w):
    B, S, D = x.shape
    N = w.shape[1]
    tm = min(512, S)
    tn = 4096
    return pl.pallas_call(
        _inproj_kernel,
        out_shape=jax.ShapeDtypeStruct((B, S, N), BF16),
        grid=(N // tn, B, S // tm),
        in_specs=[pl.BlockSpec((None, tm, D), lambda j, b, i: (b, i, 0)),
                  pl.BlockSpec((1, D), lambda j, b, i: (0, 0)),
                  pl.BlockSpec((None, 1, D), lambda j, b, i: (b, 0, 0)),
                  pl.BlockSpec((None, 1, D), lambda j, b, i: (b, 0, 0)),
                  pl.BlockSpec((D, tn), lambda j, b, i: (0, j))],
        out_specs=pl.BlockSpec((None, tm, tn), lambda j, b, i: (b, i, j)),
        compiler_params=_params("parallel", "parallel", "parallel"),
        name="in_proj",
    )(x, gain, shift, scale, w)


def _inproj_qkv_kernel(x_ref, g_ref, sh_ref, sc_ref, wz_ref, wa_ref, cos_ref, sin_ref, qn_ref, kn_ref, bd_ref, *rest):
    z_ref, qg_ref, kg_ref, vg_ref, qd_ref, kd_ref, vd_ref = rest[-7:]
    h = _rms_mod(x_ref[...], g_ref[...], sh_ref[...], sc_ref[...]).astype(BF16)
    acc = jnp.dot(h, wa_ref[...], preferred_element_type=F32)
    slab = lambda col, width=BRANCH: acc[:, col - COL_GQ:col - COL_GQ + width]
    _qkv_store(slab(COL_GQ), slab(COL_DQ), slab(COL_DK), slab(COL_DV), slab(COL_KV, 2 * LANE),
               cos_ref[...], sin_ref[...], qn_ref[...], kn_ref[...], bd_ref[...],
               qg_ref, kg_ref, vg_ref, qd_ref, kd_ref, vd_ref)
    z_ref[...] = jnp.dot(h, wz_ref[...], preferred_element_type=F32).astype(z_ref.dtype)


def _kv_buffers(B, n_keys):
    vg_rows, vd_rows = GQA_KV_HEADS * GQA_VT_ROWS, DIFF_HEADS * DIFF_VT_ROWS
    shapes = ((B, n_keys, LANE), (B, vg_rows, n_keys), (B, n_keys, BRANCH), (B, vd_rows, n_keys))
    return tuple(jnp.zeros(s, BF16) for s in shapes)


def _inproj_qkv(x, gain, shift, scale, w_z, w_att, cosf, sins, qn, kn, ones_bd, kv_bufs, key_start):
    B, S, D = x.shape
    n_keys = kv_bufs[0].shape[1]
    tm = min(512, S)
    assert key_start % tm == 0
    k0 = key_start // tm
    const = lambda shape: pl.BlockSpec(shape, lambda b, i: (0, 0))
    per_b = pl.BlockSpec((None, 1, D), lambda b, i: (b, 0, 0))
    tab = pl.BlockSpec((tm, LANE), lambda b, i: (i, 0))
    rows = lambda width: pl.BlockSpec((None, tm, width), lambda b, i: (b, i, 0))
    q_cols = lambda chans: pl.BlockSpec((None, chans, tm), lambda b, i: (b, 0, i))
    k_rows = lambda width: pl.BlockSpec((None, tm, width), lambda b, i: (b, k0 + i, 0))
    v_cols = lambda chans: pl.BlockSpec((None, chans, tm), lambda b, i: (b, 0, k0 + i))
    vg_rows, vd_rows = GQA_KV_HEADS * GQA_VT_ROWS, DIFF_HEADS * DIFF_VT_ROWS
    shapes = [(B, S, 3 * BRANCH), (B, 2 * BRANCH, S), (B, n_keys, LANE), (B, vg_rows, n_keys),
              (B, 2 * BRANCH, S), (B, n_keys, BRANCH), (B, vd_rows, n_keys)]
    specs = [rows(3 * BRANCH), q_cols(2 * BRANCH), k_rows(LANE), v_cols(vg_rows),
             q_cols(2 * BRANCH), k_rows(BRANCH), v_cols(vd_rows)]
    in_specs = [rows(D), const((1, D)), per_b, per_b, const(w_z.shape), const(w_att.shape), tab, tab,
                const((1, BRANCH)), const((1, BRANCH)), const((BRANCH, BRANCH))]
    args = [x, gain, shift, scale, w_z, w_att, cosf, sins, qn, kn, ones_bd]
    aliases = {}
    for out_idx, buf in zip((2, 3, 5, 6), kv_bufs):
        aliases[len(args)] = out_idx
        in_specs.append(pl.BlockSpec(memory_space=pl.ANY))
        args.append(buf)
    return pl.pallas_call(
        _inproj_qkv_kernel,
        out_shape=[jax.ShapeDtypeStruct(s, BF16) for s in shapes],
        grid=(B, S // tm),
        in_specs=in_specs,
        out_specs=specs,
        input_output_aliases=aliases,
        compiler_params=_params("parallel", "parallel"),
        name="in_proj_qkv",
    )(*args)


def _ffn_kernel(x_ref, g_ref, sh_ref, sc_ref, gate_ref, w1_ref, w3_ref, w2_ref, fg_ref, o_ref, *, final):
    x = x_ref[...]
    h = _rms_mod(x, g_ref[...], sh_ref[...], sc_ref[...]).astype(BF16)
    a = jnp.dot(h, w1_ref[...], preferred_element_type=F32)
    b = jnp.dot(h, w3_ref[...], preferred_element_type=F32)
    u = (_silu(a) * b).astype(BF16)
    y = x + gate_ref[...] * jnp.dot(u, w2_ref[...], preferred_element_type=F32)
    if final:
        y = y * lax.rsqrt(jnp.mean(y * y, axis=-1, keepdims=True) + EPS) * fg_ref[...]
    o_ref[...] = y


def _ffn(x, gain, shift, scale, gate, w1, w3, w2, final_gain, final):
    B, S, D = x.shape
    H = w1.shape[1]
    tm = min(512, S)
    const = lambda shape: pl.BlockSpec(shape, lambda b, i: (0,) * len(shape), pipeline_mode=pl.Buffered(1))
    per_b = pl.BlockSpec((None, 1, D), lambda b, i: (b, 0, 0))
    return pl.pallas_call(
        functools.partial(_ffn_kernel, final=final),
        out_shape=jax.ShapeDtypeStruct((B, S, D), F32),
        grid=(B, S // tm),
        in_specs=[pl.BlockSpec((None, tm, D), lambda b, i: (b, i, 0)),
                  const((1, D)), per_b, per_b, per_b,
                  const((D, H)), const((D, H)), const((H, D)), const((1, D))],
        out_specs=pl.BlockSpec((None, tm, D), lambda b, i: (b, i, 0)),
        compiler_params=_params("parallel", "parallel"),
        name="ffn",
    )(x, gain, shift, scale, gate, w1, w3, w2, final_gain)


def _merge_kernel(x_ref, gm_ref, f_ref, c_ref, a_ref, d_ref, zg_ref,
                  wf_ref, wc_ref, wa_ref, wd_ref, wo_ref, o_ref):
    D = x_ref.shape[-1]
    acc = None
    for b, (br, w) in enumerate(((f_ref, wf_ref), (c_ref, wc_ref), (a_ref, wa_ref), (d_ref, wd_ref))):
        y = jnp.dot(br[...], w[...], preferred_element_type=F32)
        gate = jax.nn.sigmoid(zg_ref[:, b * D:(b + 1) * D].astype(F32))
        acc = gate * y if acc is None else acc + gate * y
    out = jnp.dot(acc.astype(BF16), wo_ref[...], preferred_element_type=F32)
    o_ref[...] = x_ref[...] + gm_ref[...] * out


def _merge(x, gm, f, cv, og, od, zg, wf, wc, wa, wd, wo):
    B, S, D = x.shape
    tm = min(512, S)
    row = lambda width: pl.BlockSpec((None, tm, width), lambda b, i: (b, i, 0))
    const = lambda shape: pl.BlockSpec(shape, lambda b, i: (0, 0))
    return pl.pallas_call(
        _merge_kernel,
        out_shape=jax.ShapeDtypeStruct((B, S, D), F32),
        grid=(B, S // tm),
        in_specs=[row(D), pl.BlockSpec((None, 1, D), lambda b, i: (b, 0, 0)),
                  row(BRANCH), row(BRANCH), row(BRANCH), row(BRANCH),
                  pl.BlockSpec((None, tm, N_BRANCH * D), lambda b, i: (b, i, 0)),
                  const((BRANCH, D)), const((BRANCH, D)), const((BRANCH, D)), const((BRANCH, D)),
                  const((D, D))],
        out_specs=row(D),
        compiler_params=_params("parallel", "parallel"),
        name="merge",
    )(x, gm, f, cv, og, od, zg, wf, wc, wa, wd, wo)


def _conv_kernel(a_ref, g_ref, ap_ref, gp_ref, an_ref, gn_ref, w_ref, b_ref, lg_ref, lb_ref, o_ref, u_s,
                 *, ts, chunk):
    i = pl.program_id(1)
    n = pl.num_programs(1)

    def glu(a, g):
        return a.astype(F32) * jax.nn.sigmoid(g.astype(F32))

    u_s[CONV_HALO:CONV_HALO + ts, :] = glu(a_ref[...], g_ref[...])
    u_s[0:CONV_HALO, :] = jnp.where(i > 0, glu(ap_ref[...], gp_ref[...]), 0.0)
    u_s[CONV_HALO + ts:2 * CONV_HALO + ts, :] = jnp.where(i < n - 1, glu(an_ref[...], gn_ref[...]), 0.0)
    pad = CONV_K // 2

    def body(c, carry):
        r0 = pl.multiple_of(c * chunk, chunk)
        win = u_s[pl.ds(r0, chunk + 2 * CONV_HALO), :]
        rows = chunk + 2 * CONV_HALO
        acc = jnp.zeros((chunk // SUBLANE, SUBLANE, BRANCH), F32) + b_ref[...]
        for r in range(SUBLANE):
            taps = [j for j in range(CONV_K) if (CONV_HALO - pad + j) % SUBLANE == r]
            shifted = win if r == 0 else pltpu.roll(win, rows - r, 0)
            for j in taps:
                base = CONV_HALO - pad + j - r
                tap = shifted[base:base + chunk, :].reshape(chunk // SUBLANE, SUBLANE, BRANCH)
                acc = acc + w_ref[j] * tap
        acc = acc.reshape(chunk, BRANCH)
        mu = jnp.mean(acc, axis=-1, keepdims=True)
        d = acc - mu
        var = jnp.mean(d * d, axis=-1, keepdims=True)
        y = d * lax.rsqrt(var + LN_EPS) * lg_ref[...] + lb_ref[...]
        o_ref[pl.ds(r0, chunk), :] = _silu(y).astype(o_ref.dtype)
        return carry

    lax.fori_loop(0, ts // chunk, body, 0)


def _conv_branch(z, w, b, ln_g, ln_b):
    B, S, _ = z.shape
    ts = min(512, S)
    chunk = min(256, ts)
    hb = ts // CONV_HALO
    last = S // CONV_HALO - 1
    cur = lambda blk: pl.BlockSpec((None, ts, BRANCH), lambda b_, i: (b_, i, blk))
    prev = lambda blk: pl.BlockSpec((None, CONV_HALO, BRANCH),
                                    lambda b_, i: (b_, jnp.maximum(i * hb - 1, 0), blk))
    nxt = lambda blk: pl.BlockSpec((None, CONV_HALO, BRANCH),
                                   lambda b_, i: (b_, jnp.minimum((i + 1) * hb, last), blk))
    const = lambda shape: pl.BlockSpec(shape, lambda b_, i: (0, 0))
    return pl.pallas_call(
        functools.partial(_conv_kernel, ts=ts, chunk=chunk),
        out_shape=jax.ShapeDtypeStruct((B, S, BRANCH), BF16),
        grid=(B, S // ts),
        in_specs=[cur(BLK_CA), cur(BLK_CG), prev(BLK_CA), prev(BLK_CG), nxt(BLK_CA), nxt(BLK_CG),
                  pl.BlockSpec((CONV_K, SUBLANE, BRANCH), lambda b_, i: (0, 0, 0)),
                  const((1, BRANCH)), const((1, BRANCH)), const((1, BRANCH))],
        out_specs=pl.BlockSpec((None, ts, BRANCH), lambda b_, i: (b_, i, 0)),
        scratch_shapes=[pltpu.VMEM((ts + 2 * CONV_HALO, BRANCH), F32)],
        compiler_params=_params("parallel", "parallel"),
        name="conv_branch",
    )(z, z, z, z, z, z, jnp.broadcast_to(w[:, None, :], (CONV_K, SUBLANE, BRANCH)), b, ln_g, ln_b)


def _head_rms(x, ones_bd):
    x2 = x * x
    hi = x2.astype(BF16)
    lo = (x2 - hi.astype(F32)).astype(BF16)
    ssum = (jnp.dot(hi, ones_bd, preferred_element_type=F32)
            + jnp.dot(lo, ones_bd, preferred_element_type=F32))
    return x * lax.rsqrt(ssum * (1.0 / HEAD_DIM) + EPS)


def _rope(x, cosf, sins):
    width = x.shape[1]
    reps = width // LANE
    c = jnp.concatenate([cosf] * reps, axis=1) if reps > 1 else cosf
    s = jnp.concatenate([sins] * reps, axis=1) if reps > 1 else sins
    lane = lax.broadcasted_iota(jnp.int32, x.shape, 1)
    first_half = (lane & (HEAD_DIM - 1)) < HEAD_DIM // 2
    partner = jnp.where(first_half, pltpu.roll(x, width - HEAD_DIM // 2, 1), pltpu.roll(x, HEAD_DIM // 2, 1))
    return x * c + partner * s


def _qkv_store(gq, dq, dk, dv, kvz, cosf, sins, qn, kn, bd, qg_ref, kg_ref, vg_ref, qd_ref, kd_ref, vd_ref):
    scale = HEAD_DIM ** -0.5 * math.log2(math.e)
    ts = gq.shape[0]

    half = jnp.zeros((HEAD_DIM, ts), qg_ref.dtype)
    q = _rope(_head_rms(gq, bd) * qn, cosf, sins) * scale
    for p in range(BRANCH // LANE):
        qt = q[:, p * LANE:(p + 1) * LANE].T
        for e in range(2):
            h = 2 * p + e
            kvh = h // GQA_GROUP
            base = h * LANE
            qg_ref[base + kvh * HEAD_DIM:base + (kvh + 1) * HEAD_DIM, :] = (
                qt[e * HEAD_DIM:(e + 1) * HEAD_DIM, :].astype(qg_ref.dtype))
            qg_ref[base + (1 - kvh) * HEAD_DIM:base + (2 - kvh) * HEAD_DIM, :] = half

    k = _rope(_head_rms(kvz[:, :LANE], bd[:LANE, :LANE]) * kn[:, :LANE], cosf, sins)
    kg_ref[...] = k.astype(kg_ref.dtype)
    vt = kvz[:, LANE:2 * LANE].T
    row = lax.broadcasted_iota(jnp.int32, (GQA_VT_ROWS - HEAD_DIM, ts), 0)
    tail = jnp.where(row == 0, 1.0, 0.0).astype(vg_ref.dtype)
    for g in range(GQA_KV_HEADS):
        vg_ref[g * GQA_VT_ROWS:g * GQA_VT_ROWS + HEAD_DIM, :] = vt[g * HEAD_DIM:(g + 1) * HEAD_DIM, :].astype(vg_ref.dtype)
        vg_ref[g * GQA_VT_ROWS + HEAD_DIM:(g + 1) * GQA_VT_ROWS, :] = tail

    qd = _rope(dq, cosf, sins) * scale
    for h in range(DIFF_HEADS):
        qt = qd[:, h * LANE:(h + 1) * LANE].T.astype(qd_ref.dtype)
        base = 2 * h * LANE
        qd_ref[base:base + HEAD_DIM, :] = qt[:HEAD_DIM, :]
        qd_ref[base + HEAD_DIM:base + LANE, :] = half
        qd_ref[base + LANE:base + LANE + HEAD_DIM, :] = half
        qd_ref[base + LANE + HEAD_DIM:base + 2 * LANE, :] = qt[HEAD_DIM:, :]
        vd_ref[h * DIFF_VT_ROWS:h * DIFF_VT_ROWS + LANE, :] = (
            dv[:, h * LANE:(h + 1) * LANE].T.astype(vd_ref.dtype))
        vd_ref[h * DIFF_VT_ROWS + LANE:(h + 1) * DIFF_VT_ROWS, :] = tail
    kd_ref[...] = _rope(dk, cosf, sins).astype(kd_ref.dtype)


def _attn_kernel(*refs, sets, tq, tk, mode, lam_init):
    if mode == "diff":
        q_ref, k_ref, v_ref, l1q, l1k, l2q, l2k, dn_ref, o_ref, q_s, acc_s = refs
    else:
        q_ref, k_ref, v_ref, o_ref, q_s, acc_s = refs
    n_groups, vrows = acc_s.shape[0], acc_s.shape[1]
    n_tiles = k_ref.shape[0] // tk
    width = sets * tq
    for g in range(n_groups):
        for r in range(sets):
            q_s[g, :, r * tq:(r + 1) * tq] = q_ref[(g * sets + r) * LANE:(g * sets + r + 1) * LANE, :]

    def k_tile(g, rows):
        return k_ref[rows, :] if mode == "gqa" else k_ref[rows, g * LANE:(g + 1) * LANE]

    def finish(g):
        if mode == "diff":
            ot = acc_s[g, :LANE, :] / acc_s[g, LANE:LANE + 1, :]
            o = [ot[:, r * tq:(r + 1) * tq].T for r in range(sets)]
            lam = (jnp.exp(jnp.sum(l1q[...] * l1k[...], axis=1, keepdims=True))
                   - jnp.exp(jnp.sum(l2q[...] * l2k[...], axis=1, keepdims=True)) + lam_init)
            d = o[0] - lam * o[1]
            d = d * lax.rsqrt(jnp.mean(d * d, axis=1, keepdims=True) + EPS) * dn_ref[...]
            o_ref[:, g * LANE:(g + 1) * LANE] = (d * (1.0 - lam_init)).astype(o_ref.dtype)
        else:
            ot = acc_s[g, :HEAD_DIM, :] / acc_s[g, HEAD_DIM:HEAD_DIM + 1, :]
            for pair in range(sets // 2):
                both = jnp.concatenate([ot[:, (2 * pair) * tq:(2 * pair + 1) * tq],
                                        ot[:, (2 * pair + 1) * tq:(2 * pair + 2) * tq]], axis=0)
                col = (g * (sets // 2) + pair) * LANE
                o_ref[:, col:col + LANE] = both.T.astype(o_ref.dtype)

    m_run = [None] * n_groups
    for t in range(n_tiles):
        for g in range(n_groups):
            st = jnp.dot(k_tile(g, slice(t * tk, (t + 1) * tk)), q_s[g], preferred_element_type=F32)
            m_tile = jnp.max(st, axis=0, keepdims=True)
            pt = jnp.exp2(st - (m_tile if t == 0 else m_run[g]))
            pv = jnp.dot(v_ref[g * vrows:(g + 1) * vrows, t * tk:(t + 1) * tk], pt.astype(BF16),
                         preferred_element_type=F32)
            if t == 0:
                m_run[g] = m_tile
                acc_s[g] = pv
            else:
                m_new = jnp.maximum(m_run[g], m_tile)
                alpha = jnp.exp2(m_run[g] - m_new)
                acc_s[g] = (acc_s[g] + pv) * alpha
                m_run[g] = m_new
    for g in range(n_groups):
        finish(g)
    overflowed = jnp.max(jnp.where(jnp.isfinite(acc_s[...]), 0.0, 1.0)) > 0.0

    @pl.when(overflowed)
    def _exact():
        for g in range(n_groups):
            def body(t, m_old, g=g):
                st = jnp.dot(k_tile(g, pl.ds(pl.multiple_of(t * tk, tk), tk)), q_s[g], preferred_element_type=F32)
                m_new = jnp.maximum(m_old, jnp.max(st, axis=0, keepdims=True))
                alpha = jnp.exp2(m_old - m_new)
                pt = jnp.exp2(st - m_new)
                v_tile = v_ref[g * vrows:(g + 1) * vrows, pl.ds(pl.multiple_of(t * tk, LANE), tk)]
                acc_s[g] = alpha * acc_s[g] + jnp.dot(v_tile, pt.astype(BF16), preferred_element_type=F32)
                return m_new

            acc_s[g] = jnp.zeros(acc_s.shape[1:], F32)
            lax.fori_loop(0, n_tiles, body, jnp.full((1, width), -jnp.inf, F32))
            finish(g)


def _attention(qt, k, vt, mode, key_start, NK, lam_vecs=None, diff_norm=None, lam_init=0.0):
    B, _, S = qt.shape
    assert key_start % NK == 0
    kb = key_start // NK
    tq = min(512 if mode == "diff" else 256, S)
    tk = 768
    tk = tk if NK % tk == 0 else 256
    per_step = 2 if mode == "diff" else 1
    if mode == "diff":
        groups, sets, out_w, vrows = DIFF_HEADS, 2, LANE, DIFF_VT_ROWS
        k_spec = pl.BlockSpec((None, NK, per_step * LANE), lambda b, g, i: (b, kb, g))
    else:
        groups, sets, out_w, vrows = GQA_KV_HEADS, GQA_GROUP, GQA_GROUP * HEAD_DIM, GQA_VT_ROWS
        k_spec = pl.BlockSpec((None, NK, LANE), lambda b, g, i: (b, kb, 0))
    in_specs = [pl.BlockSpec((None, per_step * sets * LANE, tq), lambda b, g, i: (b, g, i)),
                k_spec,
                pl.BlockSpec((None, per_step * vrows, NK), lambda b, g, i: (b, g, kb))]
    args = [qt, k, vt]
    if mode == "diff":
        in_specs += [pl.BlockSpec((1, HEAD_DIM), lambda b, g, i: (0, 0))] * 4
        in_specs += [pl.BlockSpec((1, LANE), lambda b, g, i: (0, 0))]
        args += list(lam_vecs) + [diff_norm]
    return pl.pallas_call(
        functools.partial(_attn_kernel, sets=sets, tq=tq, tk=tk, mode=mode, lam_init=lam_init),
        out_shape=jax.ShapeDtypeStruct((B, S, BRANCH), BF16),
        grid=(B, groups // per_step, S // tq),
        in_specs=in_specs,
        out_specs=pl.BlockSpec((None, tq, per_step * out_w), lambda b, g, i: (b, i, g)),
        scratch_shapes=[pltpu.VMEM((per_step, LANE, sets * tq), BF16),
                        pltpu.VMEM((per_step, vrows, sets * tq), F32)],
        compiler_params=_params("parallel", "parallel", "parallel"),
        name="attn_" + mode,
    )(*args)


def _dft_cos_sin(n):
    idx = np.arange(n)
    ang = 2.0 * np.pi * ((idx[:, None] * idx[None, :]) % n) / n
    return np.cos(ang), np.sin(ang)


def _channel_dft_matrix():
    c, s = _dft_cos_sin(GRID_W)
    groups = BRANCH // GRID_W
    eye = np.eye(groups)
    return np.concatenate([np.kron(eye, c), -np.kron(eye, s)], axis=1) / math.sqrt(GRID_W)


def _fourier_a_kernel(z_ref, w0_ref, d1_ref, tc_ref, ts_ref, o_ref):
    n1, nb, _ = z_ref.shape
    zt = pltpu.einshape("kjm->jkm", z_ref[...])
    u = jnp.dot(zt.reshape(nb * n1, BRANCH), w0_ref[...], preferred_element_type=F32)
    u = u.astype(BF16).reshape(nb, n1, 2 * BRANCH)
    reps = BRANCH // LANE
    out_r, out_i = [], []
    for j in range(nb):
        p = jnp.dot(d1_ref[...], u[j], preferred_element_type=F32)
        vr = p[:n1, :BRANCH] + p[n1:, BRANCH:]
        vi = p[:n1, BRANCH:] - p[n1:, :BRANCH]
        c = jnp.concatenate([tc_ref[j]] * reps, axis=1)
        s = jnp.concatenate([ts_ref[j]] * reps, axis=1)
        out_r.append(vr * c + vi * s)
        out_i.append(vi * c - vr * s)
    o_ref[0] = pltpu.einshape("jkm->kjm", jnp.stack(out_r).astype(o_ref.dtype))
    o_ref[1] = pltpu.einshape("jkm->kjm", jnp.stack(out_i).astype(o_ref.dtype))


def _fourier_b_kernel(v_ref, c3_ref, s3_ref, o_ref):
    nb = v_ref.shape[1]
    outs = [jnp.dot(c3_ref[...], v_ref[0, j], preferred_element_type=F32)
            + jnp.dot(s3_ref[...], v_ref[1, j], preferred_element_type=F32) for j in range(nb)]
    o_ref[...] = pltpu.einshape("jkm->kjm", jnp.stack(outs).astype(o_ref.dtype))


def _bmm_kernel(a_ref, x_ref, o_ref):
    o_ref[...] = jnp.dot(a_ref[...], x_ref[...], preferred_element_type=F32).astype(o_ref.dtype)


def _bmm(a, x, tn):
    M, K = a.shape
    B, _, N = x.shape
    tn = min(tn, N)
    return pl.pallas_call(
        _bmm_kernel,
        out_shape=jax.ShapeDtypeStruct((B, M, N), BF16),
        grid=(B, N // tn),
        in_specs=[pl.BlockSpec((M, K), lambda b, j: (0, 0)),
                  pl.BlockSpec((None, K, tn), lambda b, j: (b, 0, j))],
        out_specs=pl.BlockSpec((None, M, tn), lambda b, j: (b, 0, j)),
        compiler_params=_params("parallel", "parallel"),
        name="bmm",
    )(a, x)


def _slab_matmul_kernel(z_ref, w_ref, o_ref):
    o_ref[...] = jnp.dot(z_ref[...], w_ref[...], preferred_element_type=F32).astype(o_ref.dtype)


def _slab_matmul(z, blk, w):
    B, S, _ = z.shape
    K, N = w.shape
    ts = min(256, S)
    return pl.pallas_call(
        _slab_matmul_kernel,
        out_shape=jax.ShapeDtypeStruct((B, S, N), BF16),
        grid=(B, S // ts),
        in_specs=[pl.BlockSpec((None, ts, K), lambda b, i: (b, i, blk)),
                  pl.BlockSpec((K, N), lambda b, i: (0, 0))],
        out_specs=pl.BlockSpec((None, ts, N), lambda b, i: (b, i, 0)),
        compiler_params=_params("parallel", "parallel"),
        name="slab_matmul",
    )(z, w)


def _fourier_long(z, w0):
    B, S, _ = z.shape
    n2 = GRID_W
    n1 = S // n2
    nb = FOURIER_NB
    assert n1 % nb == 0 and n2 % nb == 0
    c1, s1 = _dft_cos_sin(n1)
    d1 = jnp.asarray(np.concatenate([c1, s1], axis=0) / math.sqrt(n1), F32).astype(BF16)
    ang = 2.0 * np.pi * (np.arange(n2)[:, None] * np.arange(n1)[None, :]) / S
    twc = jnp.asarray(np.repeat(np.cos(ang)[:, :, None], LANE, axis=2), F32)
    tws = jnp.asarray(np.repeat(np.sin(ang)[:, :, None], LANE, axis=2), F32)
    v = pl.pallas_call(
        _fourier_a_kernel,
        out_shape=jax.ShapeDtypeStruct((B, 2, n1, n2, BRANCH), BF16),
        grid=(B, n2 // nb),
        in_specs=[pl.BlockSpec((None, n1, nb, BRANCH), lambda b, j: (b, 0, j, BLK_FOUR)),
                  pl.BlockSpec((BRANCH, 2 * BRANCH), lambda b, j: (0, 0)),
                  pl.BlockSpec((2 * n1, n1), lambda b, j: (0, 0)),
                  pl.BlockSpec((nb, n1, LANE), lambda b, j: (j, 0, 0)),
                  pl.BlockSpec((nb, n1, LANE), lambda b, j: (j, 0, 0))],
        out_specs=pl.BlockSpec((None, 2, n1, nb, BRANCH), lambda b, j: (b, 0, 0, j, 0)),
        compiler_params=_params("parallel", "parallel"),
        name="fourier_a",
    )(z.reshape(B, n1, n2, z.shape[-1]), w0, d1, twc, tws)
    c3, s3 = _dft_cos_sin(n2)
    c3 = jnp.asarray(c3 / math.sqrt(n2), F32).astype(BF16)
    s3 = jnp.asarray(s3 / math.sqrt(n2), F32).astype(BF16)
    f = pl.pallas_call(
        _fourier_b_kernel,
        out_shape=jax.ShapeDtypeStruct((B, n2, n1, BRANCH), BF16),
        grid=(B, n1 // nb),
        in_specs=[pl.BlockSpec((None, 2, nb, n2, BRANCH), lambda b, i: (b, 0, i, 0, 0)),
                  pl.BlockSpec((n2, n2), lambda b, i: (0, 0)),
                  pl.BlockSpec((n2, n2), lambda b, i: (0, 0))],
        out_specs=pl.BlockSpec((None, n2, nb, BRANCH), lambda b, i: (b, 0, i, 0)),
        compiler_params=_params("parallel", "parallel"),
        name="fourier_b",
    )(v, c3, s3)
    return f.reshape(B, S, BRANCH)


def _fourier_short(z, w0):
    B, S, _ = z.shape
    u = _slab_matmul(z, BLK_FOUR, w0)
    ust = jnp.concatenate([u[:, :, :BRANCH], u[:, :, BRANCH:]], axis=1)
    c, s = _dft_cos_sin(S)
    dl = jnp.asarray(np.concatenate([c, s], axis=1) / math.sqrt(S), F32).astype(BF16)
    return _bmm(dl, ust, BRANCH)


def _rope_tables(n_rows):
    half = HEAD_DIM // 4
    inv = ROPE_BASE ** (-np.arange(0, HEAD_DIM // 2, 2, dtype=np.float64) / (HEAD_DIM // 2))
    row = np.repeat(np.arange(n_rows, dtype=np.float64), GRID_W)
    col = np.tile(np.arange(GRID_W, dtype=np.float64), n_rows)
    ang = np.concatenate([row[:, None] * inv, col[:, None] * inv], axis=-1)
    assert ang.shape[1] == 2 * half
    cos, sin = np.cos(ang), np.sin(ang)
    cosf = np.tile(np.concatenate([cos, cos], axis=1), (1, LANE // HEAD_DIM))
    sins = np.tile(np.concatenate([-sin, sin], axis=1), (1, LANE // HEAD_DIM))
    return jnp.asarray(cosf, F32), jnp.asarray(sins, F32)


def _mixer_branches(z, lw, w0, long_seq):
    f = _fourier_long(z, w0) if long_seq else _fourier_short(z, w0)
    cv = _conv_branch(z, lw["conv_w"], lw["conv_b"], lw["conv_ln_g"], lw["conv_ln_b"])
    return f, cv


def kernel(x, c, ctx, c_ctx, w_ada, b_ada, norm_mix, w_in, w_four, conv_w, conv_b, conv_ln_g, conv_ln_b, w_conv, q_norm, k_norm, w_gqa, lam_q1, lam_k1, lam_q2, lam_k2, diff_norm, w_diff, w_out, norm_ffn, w_ffn1, w_ffn3, w_ffn2, final_norm):
    B, S, D = x.shape
    Sc = ctx.shape[1]
    depth = w_ada.shape[0]
    assert B + 1 <= 8 and S % GRID_W == 0 and D == 2 * BRANCH

    cond = jnp.zeros((8, D), F32).at[:B].set(c).at[B].set(c_ctx)
    mods = _ada_mod(cond, w_ada, b_ada)

    rope_x = _rope_tables(S // GRID_W)
    rope_c = (jnp.ones((Sc, LANE), F32), jnp.zeros((Sc, LANE), F32))
    ones_bd = jnp.asarray(np.kron(np.eye(BRANCH // HEAD_DIM), np.ones((HEAD_DIM, HEAD_DIM))), F32).astype(BF16)
    w0 = jnp.asarray(_channel_dft_matrix(), F32).astype(BF16)
    tile2 = lambda v: jnp.tile(v, LANE // HEAD_DIM * (BRANCH // LANE)).reshape(1, BRANCH)

    for l in range(depth):
        last = l == depth - 1
        lam_init = 0.8 - 0.6 * math.exp(-0.3 * l)
        mx = [m.reshape(B, 1, D) for m in jnp.split(mods[l, :B], 6, axis=-1)]
        mc = [jnp.broadcast_to(m.reshape(1, 1, D), (B, 1, D)) for m in jnp.split(mods[l, B], 6, axis=-1)]
        lw = dict(conv_w=conv_w[l], conv_b=conv_b[l].reshape(1, BRANCH),
                  conv_ln_g=conv_ln_g[l].reshape(1, BRANCH), conv_ln_b=conv_ln_b[l].reshape(1, BRANCH),
                  q_norm=tile2(q_norm[l]), k_norm=tile2(k_norm[l]))
        w_z, w_att, w_gate = (w_in[l][:, a:b].astype(BF16) for a, b in ((0, COL_GQ), (COL_GQ, GATE_COL0), (GATE_COL0, None)))
        wf, wc, wa, wd, wo = (w.astype(BF16) for w in (w_four[l], w_conv[l], w_gqa[l], w_diff[l], w_out[l]))
        w1, w3, w2 = (w.astype(BF16) for w in (w_ffn1[l], w_ffn3[l], w_ffn2[l]))
        gain_m, gain_f = norm_mix[l].reshape(1, D), norm_ffn[l].reshape(1, D)
        lam_vecs = [v[l].reshape(1, HEAD_DIM) for v in (lam_q1, lam_k1, lam_q2, lam_k2)]
        dn = diff_norm[l].reshape(1, LANE)
        fin = final_norm.reshape(1, D)

        zx, qgx, kg, vg, qdx, kd, vd = _inproj_qkv(x, gain_m, mx[0], mx[1], w_z, w_att, rope_x[0], rope_x[1],
                                                   lw["q_norm"], lw["k_norm"], ones_bd, _kv_buffers(B, S + Sc), 0)
        zc, qgc, kg, vg, qdc, kd, vd = _inproj_qkv(ctx, gain_m, mc[0], mc[1], w_z, w_att, rope_c[0], rope_c[1],
                                                   lw["q_norm"], lw["k_norm"], ones_bd, (kg, vg, kd, vd), S)
        zgx = _inproj(x, gain_m, mx[0], mx[1], w_gate)
        zgc = _inproj(ctx, gain_m, mc[0], mc[1], w_gate)
        fx, cvx = _mixer_branches(zx, lw, w0, True)
        fc, cvc = _mixer_branches(zc, lw, w0, False)

        ogx = _attention(qgx, kg, vg, "gqa", 0, S + Sc)
        odx = _attention(qdx, kd, vd, "diff", 0, S + Sc, lam_vecs, dn, lam_init)
        x = _merge(x, mx[2], fx, cvx, ogx, odx, zgx, wf, wc, wa, wd, wo)
        if not last:
            ogc = _attention(qgc, kg, vg, "gqa", S, Sc)
            odc = _attention(qdc, kd, vd, "diff", S, Sc, lam_vecs, dn, lam_init)
            ctx = _merge(ctx, mc[2], fc, cvc, ogc, odc, zgc, wf, wc, wa, wd, wo)
            ctx = _ffn(ctx, gain_f, mc[3], mc[4], mc[5], w1, w3, w2, fin, False)
        x = _ffn(x, gain_f, mx[3], mx[4], mx[5], w1, w3, w2, fin, last)
    return x
```

```python
import functools
import math

import numpy as np
import jax
import jax.numpy as jnp
from jax import lax
from jax.experimental import pallas as pl
from jax.experimental.pallas import tpu as pltpu

F32 = jnp.float32
BF16 = jnp.bfloat16

HEAD_DIM = 64
BRANCH = 512
GQA_GROUP = 4
GQA_KV_HEADS = 2
DIFF_VT_ROWS = 144
GQA_VT_ROWS = 80
DIFF_HEADS = 4
N_BRANCH = 4
GRID_W = 64
CONV_K = 31
ROPE_BASE = 10000.0
EPS = 1e-6
LN_EPS = 1e-5

LANE = 128
SUBLANE = 8
VMEM_LIMIT = 56 * 1024 * 1024

BLK_FOUR, BLK_CA, BLK_CG = range(3)
COL_GQ, COL_KV, COL_DQ, COL_DK, COL_DV, GATE_COL0 = 1536, 2048, 2304, 2816, 3328, 3840
FOURIER_NB = 16
CONV_HALO = 16


def _params(*sem):
    return pltpu.CompilerParams(dimension_semantics=sem, vmem_limit_bytes=VMEM_LIMIT)


def _silu(x):
    return x * jax.nn.sigmoid(x)


def _rms_mod(x, g, shift, scale):
    y = x * lax.rsqrt(jnp.mean(x * x, axis=-1, keepdims=True) + EPS) * g
    return y * (1.0 + scale) + shift


def _ada_kernel(c_ref, w_ref, b_ref, o_ref):
    a = _silu(c_ref[...])
    o_ref[...] = jnp.dot(a, w_ref[...], precision=lax.Precision.HIGHEST,
                         preferred_element_type=F32) + b_ref[...]


def _ada_mod(cond, w_ada, b_ada):
    L, D, N = w_ada.shape
    tn = 1536
    return pl.pallas_call(
        _ada_kernel,
        out_shape=jax.ShapeDtypeStruct((L, 8, N), F32),
        grid=(L, N // tn),
        in_specs=[pl.BlockSpec((8, D), lambda l, j: (0, 0)),
                  pl.BlockSpec((None, D, tn), lambda l, j: (l, 0, j)),
                  pl.BlockSpec((None, 1, tn), lambda l, j: (l, 0, j))],
        out_specs=pl.BlockSpec((None, 8, tn), lambda l, j: (l, 0, j)),
        compiler_params=_params("parallel", "parallel"),
        name="ada_mod",
    )(cond, w_ada, b_ada.reshape(L, 1, N))


def _inproj_kernel(x_ref, g_ref, sh_ref, sc_ref, w_ref, o_ref):
    h = _rms_mod(x_ref[...], g_ref[...], sh_ref[...], sc_ref[...])
    o_ref[...] = jnp.dot(h.astype(BF16), w_ref[...], preferred_element_type=F32).astype(o_ref.dtype)


def _inproj(x, gain, shift, scale, w):
    B, S, D = x.shape
    N = w.shape[1]
    tm = min(512, S)
    tn = 4096
    return pl.pallas_call(
        _inproj_kernel,
        out_shape=jax.ShapeDtypeStruct((B, S, N), BF16),
        grid=(N // tn, B, S // tm),
        in_specs=[pl.BlockSpec((None, tm, D), lambda j, b, i: (b, i, 0)),
                  pl.BlockSpec((1, D), lambda j, b, i: (0, 0)),
                  pl.BlockSpec((None, 1, D), lambda j, b, i: (b, 0, 0)),
                  pl.BlockSpec((None, 1, D), lambda j, b, i: (b, 0, 0)),
                  pl.BlockSpec((D, tn), lambda j, b, i: (0, j))],
        out_specs=pl.BlockSpec((None, tm, tn), lambda j, b, i: (b, i, j)),
        compiler_params=_params("parallel", "parallel", "parallel"),
        name="in_proj",
    )(x, gain, shift, scale, w)


def _inproj_qkv_kernel(x_ref, g_ref, sh_ref, sc_ref, wz_ref, wa_ref, cos_ref, sin_ref, qn_ref, kn_ref, bd_ref, *rest):
    z_ref, qg_ref, kg_ref, vg_ref, qd_ref, kd_ref, vd_ref = rest[-7:]
    h = _rms_mod(x_ref[...], g_ref[...], sh_ref[...], sc_ref[...]).astype(BF16)
    acc = jnp.dot(h, wa_ref[...], preferred_element_type=F32)
    slab = lambda col, width=BRANCH: acc[:, col - COL_GQ:col - COL_GQ + width]
    _qkv_store(slab(COL_GQ), slab(COL_DQ), slab(COL_DK), slab(COL_DV), slab(COL_KV, 2 * LANE),
               cos_ref[...], sin_ref[...], qn_ref[...], kn_ref[...], bd_ref[...],
               qg_ref, kg_ref, vg_ref, qd_ref, kd_ref, vd_ref)
    z_ref[...] = jnp.dot(h, wz_ref[...], preferred_element_type=F32).astype(z_ref.dtype)


def _kv_buffers(B, n_keys):
    vg_rows, vd_rows = GQA_KV_HEADS * GQA_VT_ROWS, DIFF_HEADS * DIFF_VT_ROWS
    shapes = ((B, n_keys, LANE), (B, vg_rows, n_keys), (B, n_keys, BRANCH), (B, vd_rows, n_keys))
    return tuple(jnp.zeros(s, BF16) for s in shapes)


def _inproj_qkv(x, gain, shift, scale, w_z, w_att, cosf, sins, qn, kn, ones_bd, kv_bufs, key_start):
    B, S, D = x.shape
    n_keys = kv_bufs[0].shape[1]
    tm = min(512, S)
    assert key_start % tm == 0
    k0 = key_start // tm
    const = lambda shape: pl.BlockSpec(shape, lambda b, i: (0, 0))
    per_b = pl.BlockSpec((None, 1, D), lambda b, i: (b, 0, 0))
    tab = pl.BlockSpec((tm, LANE), lambda b, i: (i, 0))
    rows = lambda width: pl.BlockSpec((None, tm, width), lambda b, i: (b, i, 0))
    q_cols = lambda chans: pl.BlockSpec((None, chans, tm), lambda b, i: (b, 0, i))
    k_rows = lambda width: pl.BlockSpec((None, tm, width), lambda b, i: (b, k0 + i, 0))
    v_cols = lambda chans: pl.BlockSpec((None, chans, tm), lambda b, i: (b, 0, k0 + i))
    vg_rows, vd_rows = GQA_KV_HEADS * GQA_VT_ROWS, DIFF_HEADS * DIFF_VT_ROWS
    shapes = [(B, S, 3 * BRANCH), (B, 2 * BRANCH, S), (B, n_keys, LANE), (B, vg_rows, n_keys),
              (B, 2 * BRANCH, S), (B, n_keys, BRANCH), (B, vd_rows, n_keys)]
    specs = [rows(3 * BRANCH), q_cols(2 * BRANCH), k_rows(LANE), v_cols(vg_rows),
             q_cols(2 * BRANCH), k_rows(BRANCH), v_cols(vd_rows)]
    in_specs = [rows(D), const((1, D)), per_b, per_b, const(w_z.shape), const(w_att.shape), tab, tab,
                const((1, BRANCH)), const((1, BRANCH)), const((BRANCH, BRANCH))]
    args = [x, gain, shift, scale, w_z, w_att, cosf, sins, qn, kn, ones_bd]
    aliases = {}
    for out_idx, buf in zip((2, 3, 5, 6), kv_bufs):
        aliases[len(args)] = out_idx
        in_specs.append(pl.BlockSpec(memory_space=pl.ANY))
        args.append(buf)
    return pl.pallas_call(
        _inproj_qkv_kernel,
        out_shape=[jax.ShapeDtypeStruct(s, BF16) for s in shapes],
        grid=(B, S // tm),
        in_specs=in_specs,
        out_specs=specs,
        input_output_aliases=aliases,
        compiler_params=_params("parallel", "parallel"),
        name="in_proj_qkv",
    )(*args)


def _ffn_kernel(x_ref, g_ref, sh_ref, sc_ref, gate_ref, w1_ref, w3_ref, w2_ref, fg_ref, o_ref, *, final):
    x = x_ref[...]
    h = _rms_mod(x, g_ref[...], sh_ref[...], sc_ref[...]).astype(BF16)
    a = jnp.dot(h, w1_ref[...], preferred_element_type=F32)
    b = jnp.dot(h, w3_ref[...], preferred_element_type=F32)
    u = (_silu(a) * b).astype(BF16)
    y = x + gate_ref[...] * jnp.dot(u, w2_ref[...], preferred_element_type=F32)
    if final:
        y = y * lax.rsqrt(jnp.mean(y * y, axis=-1, keepdims=True) + EPS) * fg_ref[...]
    o_ref[...] = y


def _ffn(x, gain, shift, scale, gate, w1, w3, w2, final_gain, final):
    B, S, D = x.shape
    H = w1.shape[1]
    tm = min(512, S)
    const = lambda shape: pl.BlockSpec(shape, lambda b, i: (0,) * len(shape), pipeline_mode=pl.Buffered(1))
    per_b = pl.BlockSpec((None, 1, D), lambda b, i: (b, 0, 0))
    return pl.pallas_call(
        functools.partial(_ffn_kernel, final=final),
        out_shape=jax.ShapeDtypeStruct((B, S, D), F32),
        grid=(B, S // tm),
        in_specs=[pl.BlockSpec((None, tm, D), lambda b, i: (b, i, 0)),
                  const((1, D)), per_b, per_b, per_b,
                  const((D, H)), const((D, H)), const((H, D)), const((1, D))],
        out_specs=pl.BlockSpec((None, tm, D), lambda b, i: (b, i, 0)),
        compiler_params=_params("parallel", "parallel"),
        name="ffn",
    )(x, gain, shift, scale, gate, w1, w3, w2, final_gain)


def _merge_kernel(x_ref, gm_ref, f_ref, c_ref, a_ref, d_ref, zg_ref,
                  wf_ref, wc_ref, wa_ref, wd_ref, wo_ref, o_ref):
    D = x_ref.shape[-1]
    acc = None
    for b, (br, w) in enumerate(((f_ref, wf_ref), (c_ref, wc_ref), (a_ref, wa_ref), (d_ref, wd_ref))):
        y = jnp.dot(br[...], w[...], preferred_element_type=F32)
        gate = jax.nn.sigmoid(zg_ref[:, b * D:(b + 1) * D].astype(F32))
        acc = gate * y if acc is None else acc + gate * y
    out = jnp.dot(acc.astype(BF16), wo_ref[...], preferred_element_type=F32)
    o_ref[...] = x_ref[...] + gm_ref[...] * out


def _merge(x, gm, f, cv, og, od, zg, wf, wc, wa, wd, wo):
    B, S, D = x.shape
    tm = min(512, S)
    row = lambda width: pl.BlockSpec((None, tm, width), lambda b, i: (b, i, 0))
    const = lambda shape: pl.BlockSpec(shape, lambda b, i: (0, 0))
    return pl.pallas_call(
        _merge_kernel,
        out_shape=jax.ShapeDtypeStruct((B, S, D), F32),
        grid=(B, S // tm),
        in_specs=[row(D), pl.BlockSpec((None, 1, D), lambda b, i: (b, 0, 0)),
                  row(BRANCH), row(BRANCH), row(BRANCH), row(BRANCH),
                  pl.BlockSpec((None, tm, N_BRANCH * D), lambda b, i: (b, i, 0)),
                  const((BRANCH, D)), const((BRANCH, D)), const((BRANCH, D)), const((BRANCH, D)),
                  const((D, D))],
        out_specs=row(D),
        compiler_params=_params("parallel", "parallel"),
        name="merge",
    )(x, gm, f, cv, og, od, zg, wf, wc, wa, wd, wo)


def _conv_kernel(a_ref, g_ref, ap_ref, gp_ref, an_ref, gn_ref, w_ref, b_ref, lg_ref, lb_ref, o_ref, u_s,
                 *, ts, chunk):
    i = pl.program_id(1)
    n = pl.num_programs(1)

    def glu(a, g):
        return a.astype(F32) * jax.nn.sigmoid(g.astype(F32))

    u_s[CONV_HALO:CONV_HALO + ts, :] = glu(a_ref[...], g_ref[...])
    u_s[0:CONV_HALO, :] = jnp.where(i > 0, glu(ap_ref[...], gp_ref[...]), 0.0)
    u_s[CONV_HALO + ts:2 * CONV_HALO + ts, :] = jnp.where(i < n - 1, glu(an_ref[...], gn_ref[...]), 0.0)
    pad = CONV_K // 2

    def body(c, carry):
        r0 = pl.multiple_of(c * chunk, chunk)
        win = u_s[pl.ds(r0, chunk + 2 * CONV_HALO), :]
        rows = chunk + 2 * CONV_HALO
        acc = jnp.zeros((chunk // SUBLANE, SUBLANE, BRANCH), F32) + b_ref[...]
        for r in range(SUBLANE):
            taps = [j for j in range(CONV_K) if (CONV_HALO - pad + j) % SUBLANE == r]
            shifted = win if r == 0 else pltpu.roll(win, rows - r, 0)
            for j in taps:
                base = CONV_HALO - pad + j - r
                tap = shifted[base:base + chunk, :].reshape(chunk // SUBLANE, SUBLANE, BRANCH)
                acc = acc + w_ref[j] * tap
        acc = acc.reshape(chunk, BRANCH)
        mu = jnp.mean(acc, axis=-1, keepdims=True)
        d = acc - mu
        var = jnp.mean(d * d, axis=-1, keepdims=True)
        y = d * lax.rsqrt(var + LN_EPS) * lg_ref[...] + lb_ref[...]
        o_ref[pl.ds(r0, chunk), :] = _silu(y).astype(o_ref.dtype)
        return carry

    lax.fori_loop(0, ts // chunk, body, 0)


def _conv_branch(z, w, b, ln_g, ln_b):
    B, S, _ = z.shape
    ts = min(512, S)
    chunk = min(256, ts)
    hb = ts // CONV_HALO
    last = S // CONV_HALO - 1
    cur = lambda blk: pl.BlockSpec((None, ts, BRANCH), lambda b_, i: (b_, i, blk))
    prev = lambda blk: pl.BlockSpec((None, CONV_HALO, BRANCH),
                                    lambda b_, i: (b_, jnp.maximum(i * hb - 1, 0), blk))
    nxt = lambda blk: pl.BlockSpec((None, CONV_HALO, BRANCH),
                                   lambda b_, i: (b_, jnp.minimum((i + 1) * hb, last), blk))
    const = lambda shape: pl.BlockSpec(shape, lambda b_, i: (0, 0))
    return pl.pallas_call(
        functools.partial(_conv_kernel, ts=ts, chunk=chunk),
        out_shape=jax.ShapeDtypeStruct((B, S, BRANCH), BF16),
        grid=(B, S // ts),
        in_specs=[cur(BLK_CA), cur(BLK_CG), prev(BLK_CA), prev(BLK_CG), nxt(BLK_CA), nxt(BLK_CG),
                  pl.BlockSpec((CONV_K, SUBLANE, BRANCH), lambda b_, i: (0, 0, 0)),
                  const((1, BRANCH)), const((1, BRANCH)), const((1, BRANCH))],
        out_specs=pl.BlockSpec((None, ts, BRANCH), lambda b_, i: (b_, i, 0)),
        scratch_shapes=[pltpu.VMEM((ts + 2 * CONV_HALO, BRANCH), F32)],
        compiler_params=_params("parallel", "parallel"),
        name="conv_branch",
    )(z, z, z, z, z, z, jnp.broadcast_to(w[:, None, :], (CONV_K, SUBLANE, BRANCH)), b, ln_g, ln_b)


def _head_rms(x, ones_bd):
    x2 = x * x
    hi = x2.astype(BF16)
    lo = (x2 - hi.astype(F32)).astype(BF16)
    ssum = (jnp.dot(hi, ones_bd, preferred_element_type=F32)
            + jnp.dot(lo, ones_bd, preferred_element_type=F32))
    return x * lax.rsqrt(ssum * (1.0 / HEAD_DIM) + EPS)


def _rope(x, cosf, sins):
    width = x.shape[1]
    reps = width // LANE
    c = jnp.concatenate([cosf] * reps, axis=1) if reps > 1 else cosf
    s = jnp.concatenate([sins] * reps, axis=1) if reps > 1 else sins
    lane = lax.broadcasted_iota(jnp.int32, x.shape, 1)
    first_half = (lane & (HEAD_DIM - 1)) < HEAD_DIM // 2
    partner = jnp.where(first_half, pltpu.roll(x, width - HEAD_DIM // 2, 1), pltpu.roll(x, HEAD_DIM // 2, 1))
    return x * c + partner * s


def _qkv_store(gq, dq, dk, dv, kvz, cosf, sins, qn, kn, bd, qg_ref, kg_ref, vg_ref, qd_ref, kd_ref, vd_ref):
    scale = HEAD_DIM ** -0.5 * math.log2(math.e)
    ts = gq.shape[0]

    half = jnp.zeros((HEAD_DIM, ts), qg_ref.dtype)
    q = _rope(_head_rms(gq, bd) * qn, cosf, sins) * scale
    for p in range(BRANCH // LANE):
        qt = q[:, p * LANE:(p + 1) * LANE].T
        for e in range(2):
            h = 2 * p + e
            kvh = h // GQA_GROUP
            base = h * LANE
            qg_ref[base + kvh * HEAD_DIM:base + (kvh + 1) * HEAD_DIM, :] = (
                qt[e * HEAD_DIM:(e + 1) * HEAD_DIM, :].astype(qg_ref.dtype))
            qg_ref[base + (1 - kvh) * HEAD_DIM:base + (2 - kvh) * HEAD_DIM, :] = half

    k = _rope(_head_rms(kvz[:, :LANE], bd[:LANE, :LANE]) * kn[:, :LANE], cosf, sins)
    kg_ref[...] = k.astype(kg_ref.dtype)
    vt = kvz[:, LANE:2 * LANE].T
    row = lax.broadcasted_iota(jnp.int32, (GQA_VT_ROWS - HEAD_DIM, ts), 0)
    tail = jnp.where(row == 0, 1.0, 0.0).astype(vg_ref.dtype)
    for g in range(GQA_KV_HEADS):
        vg_ref[g * GQA_VT_ROWS:g * GQA_VT_ROWS + HEAD_DIM, :] = vt[g * HEAD_DIM:(g + 1) * HEAD_DIM, :].astype(vg_ref.dtype)
        vg_ref[g * GQA_VT_ROWS + HEAD_DIM:(g + 1) * GQA_VT_ROWS, :] = tail

    qd = _rope(dq, cosf, sins) * scale
    for h in range(DIFF_HEADS):
        qt = qd[:, h * LANE:(h + 1) * LANE].T.astype(qd_ref.dtype)
        base = 2 * h * LANE
        qd_ref[base:base + HEAD_DIM, :] = qt[:HEAD_DIM, :]
        qd_ref[base + HEAD_DIM:base + LANE, :] = half
        qd_ref[base + LANE:base + LANE + HEAD_DIM, :] = half
        qd_ref[base + LANE + HEAD_DIM:base + 2 * LANE, :] = qt[HEAD_DIM:, :]
        vd_ref[h * DIFF_VT_ROWS:h * DIFF_VT_ROWS + LANE, :] = (
            dv[:, h * LANE:(h + 1) * LANE].T.astype(vd_ref.dtype))
        vd_ref[h * DIFF_VT_ROWS + LANE:(h + 1) * DIFF_VT_ROWS, :] = tail
    kd_ref[...] = _rope(dk, cosf, sins).astype(kd_ref.dtype)


def _attn_kernel(*refs, sets, tq, tk, mode, lam_init):
    if mode == "diff":
        q_ref, k_ref, v_ref, l1q, l1k, l2q, l2k, dn_ref, o_ref, q_s, acc_s = refs
    else:
        q_ref, k_ref, v_ref, o_ref, q_s, acc_s = refs
    n_groups, vrows = acc_s.shape[0], acc_s.shape[1]
    n_tiles = k_ref.shape[0] // tk
    width = sets * tq
    for g in range(n_groups):
        for r in range(sets):
            q_s[g, :, r * tq:(r + 1) * tq] = q_ref[(g * sets + r) * LANE:(g * sets + r + 1) * LANE, :]

    def k_tile(g, rows):
        return k_ref[rows, :] if mode == "gqa" else k_ref[rows, g * LANE:(g + 1) * LANE]

    def finish(g):
        if mode == "diff":
            ot = acc_s[g, :LANE, :] / acc_s[g, LANE:LANE + 1, :]
            o = [ot[:, r * tq:(r + 1) * tq].T for r in range(sets)]
            lam = (jnp.exp(jnp.sum(l1q[...] * l1k[...], axis=1, keepdims=True))
                   - jnp.exp(jnp.sum(l2q[...] * l2k[...], axis=1, keepdims=True)) + lam_init)
            d = o[0] - lam * o[1]
            d = d * lax.rsqrt(jnp.mean(d * d, axis=1, keepdims=True) + EPS) * dn_ref[...]
            o_ref[:, g * LANE:(g + 1) * LANE] = (d * (1.0 - lam_init)).astype(o_ref.dtype)
        else:
            ot = acc_s[g, :HEAD_DIM, :] / acc_s[g, HEAD_DIM:HEAD_DIM + 1, :]
            for pair in range(sets // 2):
                both = jnp.concatenate([ot[:, (2 * pair) * tq:(2 * pair + 1) * tq],
                                        ot[:, (2 * pair + 1) * tq:(2 * pair + 2) * tq]], axis=0)
                col = (g * (sets // 2) + pair) * LANE
                o_ref[:, col:col + LANE] = both.T.astype(o_ref.dtype)

    m_run = [None] * n_groups
    for t in range(n_tiles):
        for g in range(n_groups):
            st = jnp.dot(k_tile(g, slice(t * tk, (t + 1) * tk)), q_s[g], preferred_element_type=F32)
            m_tile = jnp.max(st, axis=0, keepdims=True)
            pt = jnp.exp2(st - (m_tile if t == 0 else m_run[g]))
            pv = jnp.dot(v_ref[g * vrows:(g + 1) * vrows, t * tk:(t + 1) * tk], pt.astype(BF16),
                         preferred_element_type=F32)
            if t == 0:
                m_run[g] = m_tile
                acc_s[g] = pv
            else:
                m_new = jnp.maximum(m_run[g], m_tile)
                alpha = jnp.exp2(m_run[g] - m_new)
                acc_s[g] = (acc_s[g] + pv) * alpha
                m_run[g] = m_new
    for g in range(n_groups):
        finish(g)
    overflowed = jnp.max(jnp.where(jnp.isfinite(acc_s[...]), 0.0, 1.0)) > 0.0

    @pl.when(overflowed)
    def _exact():
        for g in range(n_groups):
            def body(t, m_old, g=g):
                st = jnp.dot(k_tile(g, pl.ds(pl.multiple_of(t * tk, tk), tk)), q_s[g], preferred_element_type=F32)
                m_new = jnp.maximum(m_old, jnp.max(st, axis=0, keepdims=True))
                alpha = jnp.exp2(m_old - m_new)
                pt = jnp.exp2(st - m_new)
                v_tile = v_ref[g * vrows:(g + 1) * vrows, pl.ds(pl.multiple_of(t * tk, LANE), tk)]
                acc_s[g] = alpha * acc_s[g] + jnp.dot(v_tile, pt.astype(BF16), preferred_element_type=F32)
                return m_new

            acc_s[g] = jnp.zeros(acc_s.shape[1:], F32)
            lax.fori_loop(0, n_tiles, body, jnp.full((1, width), -jnp.inf, F32))
            finish(g)


def _attention(qt, k, vt, mode, key_start, NK, lam_vecs=None, diff_norm=None, lam_init=0.0):
    B, _, S = qt.shape
    assert key_start % NK == 0
    kb = key_start // NK
    tq = min(512, S)
    tk = 768
    tk = tk if NK % tk == 0 else 256
    per_step = 2 if mode == "diff" else 1
    if mode == "diff":
        groups, sets, out_w, vrows = DIFF_HEADS, 2, LANE, DIFF_VT_ROWS
        k_spec = pl.BlockSpec((None, NK, per_step * LANE), lambda b, g, i: (b, kb, g))
    else:
        groups, sets, out_w, vrows = GQA_KV_HEADS, GQA_GROUP, GQA_GROUP * HEAD_DIM, GQA_VT_ROWS
        k_spec = pl.BlockSpec((None, NK, LANE), lambda b, g, i: (b, kb, 0))
    in_specs = [pl.BlockSpec((None, per_step * sets * LANE, tq), lambda b, g, i: (b, g, i)),
                k_spec,
                pl.BlockSpec((None, per_step * vrows, NK), lambda b, g, i: (b, g, kb))]
    args = [qt, k, vt]
    if mode == "diff":
        in_specs += [pl.BlockSpec((1, HEAD_DIM), lambda b, g, i: (0, 0))] * 4
        in_specs += [pl.BlockSpec((1, LANE), lambda b, g, i: (0, 0))]
        args += list(lam_vecs) + [diff_norm]
    return pl.pallas_call(
        functools.partial(_attn_kernel, sets=sets, tq=tq, tk=tk, mode=mode, lam_init=lam_init),
        out_shape=jax.ShapeDtypeStruct((B, S, BRANCH), BF16),
        grid=(B, groups // per_step, S // tq),
        in_specs=in_specs,
        out_specs=pl.BlockSpec((None, tq, per_step * out_w), lambda b, g, i: (b, i, g)),
        scratch_shapes=[pltpu.VMEM((per_step, LANE, sets * tq), BF16),
                        pltpu.VMEM((per_step, vrows, sets * tq), F32)],
        compiler_params=_params("parallel", "parallel", "parallel"),
        name="attn_" + mode,
    )(*args)


def _dft_cos_sin(n):
    idx = np.arange(n)
    ang = 2.0 * np.pi * ((idx[:, None] * idx[None, :]) % n) / n
    return np.cos(ang), np.sin(ang)


def _channel_dft_matrix():
    c, s = _dft_cos_sin(GRID_W)
    groups = BRANCH // GRID_W
    eye = np.eye(groups)
    return np.concatenate([np.kron(eye, c), -np.kron(eye, s)], axis=1) / math.sqrt(GRID_W)


def _fourier_a_kernel(z_ref, w0_ref, d1_ref, tc_ref, ts_ref, o_ref):
    n1, nb, _ = z_ref.shape
    zt = pltpu.einshape("kjm->jkm", z_ref[...])
    u = jnp.dot(zt.reshape(nb * n1, BRANCH), w0_ref[...], preferred_element_type=F32)
    u = u.astype(BF16).reshape(nb, n1, 2 * BRANCH)
    reps = BRANCH // LANE
    out_r, out_i = [], []
    for j in range(nb):
        p = jnp.dot(d1_ref[...], u[j], preferred_element_type=F32)
        vr = p[:n1, :BRANCH] + p[n1:, BRANCH:]
        vi = p[:n1, BRANCH:] - p[n1:, :BRANCH]
        c = jnp.concatenate([tc_ref[j]] * reps, axis=1)
        s = jnp.concatenate([ts_ref[j]] * reps, axis=1)
        out_r.append(vr * c + vi * s)
        out_i.append(vi * c - vr * s)
    o_ref[0] = pltpu.einshape("jkm->kjm", jnp.stack(out_r).astype(o_ref.dtype))
    o_ref[1] = pltpu.einshape("jkm->kjm", jnp.stack(out_i).astype(o_ref.dtype))


def _fourier_b_kernel(v_ref, c3_ref, s3_ref, o_ref):
    nb = v_ref.shape[1]
    outs = [jnp.dot(c3_ref[...], v_ref[0, j], preferred_element_type=F32)
            + jnp.dot(s3_ref[...], v_ref[1, j], preferred_element_type=F32) for j in range(nb)]
    o_ref[...] = pltpu.einshape("jkm->kjm", jnp.stack(outs).astype(o_ref.dtype))


def _bmm_kernel(a_ref, x_ref, o_ref):
    o_ref[...] = jnp.dot(a_ref[...], x_ref[...], preferred_element_type=F32).astype(o_ref.dtype)


def _bmm(a, x, tn):
    M, K = a.shape
    B, _, N = x.shape
    tn = min(tn, N)
    return pl.pallas_call(
        _bmm_kernel,
        out_shape=jax.ShapeDtypeStruct((B, M, N), BF16),
        grid=(B, N // tn),
        in_specs=[pl.BlockSpec((M, K), lambda b, j: (0, 0)),
                  pl.BlockSpec((None, K, tn), lambda b, j: (b, 0, j))],
        out_specs=pl.BlockSpec((None, M, tn), lambda b, j: (b, 0, j)),
        compiler_params=_params("parallel", "parallel"),
        name="bmm",
    )(a, x)


def _slab_matmul_kernel(z_ref, w_ref, o_ref):
    o_ref[...] = jnp.dot(z_ref[...], w_ref[...], preferred_element_type=F32).astype(o_ref.dtype)


def _slab_matmul(z, blk, w):
    B, S, _ = z.shape
    K, N = w.shape
    ts = min(256, S)
    return pl.pallas_call(
        _slab_matmul_kernel,
        out_shape=jax.ShapeDtypeStruct((B, S, N), BF16),
        grid=(B, S // ts),
        in_specs=[pl.BlockSpec((None, ts, K), lambda b, i: (b, i, blk)),
                  pl.BlockSpec((K, N), lambda b, i: (0, 0))],
        out_specs=pl.BlockSpec((None, ts, N), lambda b, i: (b, i, 0)),
        compiler_params=_params("parallel", "parallel"),
        name="slab_matmul",
    )(z, w)


def _fourier_long(z, w0):
    B, S, _ = z.shape
    n2 = GRID_W
    n1 = S // n2
    nb = FOURIER_NB
    assert n1 % nb == 0 and n2 % nb == 0
    c1, s1 = _dft_cos_sin(n1)
    d1 = jnp.asarray(np.concatenate([c1, s1], axis=0) / math.sqrt(n1), F32).astype(BF16)
    ang = 2.0 * np.pi * (np.arange(n2)[:, None] * np.arange(n1)[None, :]) / S
    twc = jnp.asarray(np.repeat(np.cos(ang)[:, :, None], LANE, axis=2), F32)
    tws = jnp.asarray(np.repeat(np.sin(ang)[:, :, None], LANE, axis=2), F32)
    v = pl.pallas_call(
        _fourier_a_kernel,
        out_shape=jax.ShapeDtypeStruct((B, 2, n1, n2, BRANCH), BF16),
        grid=(B, n2 // nb),
        in_specs=[pl.BlockSpec((None, n1, nb, BRANCH), lambda b, j: (b, 0, j, BLK_FOUR)),
                  pl.BlockSpec((BRANCH, 2 * BRANCH), lambda b, j: (0, 0)),
                  pl.BlockSpec((2 * n1, n1), lambda b, j: (0, 0)),
                  pl.BlockSpec((nb, n1, LANE), lambda b, j: (j, 0, 0)),
                  pl.BlockSpec((nb, n1, LANE), lambda b, j: (j, 0, 0))],
        out_specs=pl.BlockSpec((None, 2, n1, nb, BRANCH), lambda b, j: (b, 0, 0, j, 0)),
        compiler_params=_params("parallel", "parallel"),
        name="fourier_a",
    )(z.reshape(B, n1, n2, z.shape[-1]), w0, d1, twc, tws)
    c3, s3 = _dft_cos_sin(n2)
    c3 = jnp.asarray(c3 / math.sqrt(n2), F32).astype(BF16)
    s3 = jnp.asarray(s3 / math.sqrt(n2), F32).astype(BF16)
    f = pl.pallas_call(
        _fourier_b_kernel,
        out_shape=jax.ShapeDtypeStruct((B, n2, n1, BRANCH), BF16),
        grid=(B, n1 // nb),
        in_specs=[pl.BlockSpec((None, 2, nb, n2, BRANCH), lambda b, i: (b, 0, i, 0, 0)),
                  pl.BlockSpec((n2, n2), lambda b, i: (0, 0)),
                  pl.BlockSpec((n2, n2), lambda b, i: (0, 0))],
        out_specs=pl.BlockSpec((None, n2, nb, BRANCH), lambda b, i: (b, 0, i, 0)),
        compiler_params=_params("parallel", "parallel"),
        name="fourier_b",
    )(v, c3, s3)
    return f.reshape(B, S, BRANCH)


def _fourier_short(z, w0):
    B, S, _ = z.shape
    u = _slab_matmul(z, BLK_FOUR, w0)
    ust = jnp.concatenate([u[:, :, :BRANCH], u[:, :, BRANCH:]], axis=1)
    c, s = _dft_cos_sin(S)
    dl = jnp.asarray(np.concatenate([c, s], axis=1) / math.sqrt(S), F32).astype(BF16)
    return _bmm(dl, ust, BRANCH)


def _rope_tables(n_rows):
    half = HEAD_DIM // 4
    inv = ROPE_BASE ** (-np.arange(0, HEAD_DIM // 2, 2, dtype=np.float64) / (HEAD_DIM // 2))
    row = np.repeat(np.arange(n_rows, dtype=np.float64), GRID_W)
    col = np.tile(np.arange(GRID_W, dtype=np.float64), n_rows)
    ang = np.concatenate([row[:, None] * inv, col[:, None] * inv], axis=-1)
    assert ang.shape[1] == 2 * half
    cos, sin = np.cos(ang), np.sin(ang)
    cosf = np.tile(np.concatenate([cos, cos], axis=1), (1, LANE // HEAD_DIM))
    sins = np.tile(np.concatenate([-sin, sin], axis=1), (1, LANE // HEAD_DIM))
    return jnp.asarray(cosf, F32), jnp.asarray(sins, F32)


def _mixer_branches(z, lw, w0, long_seq):
    f = _fourier_long(z, w0) if long_seq else _fourier_short(z, w0)
    cv = _conv_branch(z, lw["conv_w"], lw["conv_b"], lw["conv_ln_g"], lw["conv_ln_b"])
    return f, cv


def kernel(x, c, ctx, c_ctx, w_ada, b_ada, norm_mix, w_in, w_four, conv_w, conv_b, conv_ln_g, conv_ln_b, w_conv, q_norm, k_norm, w_gqa, lam_q1, lam_k1, lam_q2, lam_k2, diff_norm, w_diff, w_out, norm_ffn, w_ffn1, w_ffn3, w_ffn2, final_norm):
    B, S, D = x.shape
    Sc = ctx.shape[1]
    depth = w_ada.shape[0]
    assert B + 1 <= 8 and S % GRID_W == 0 and D == 2 * BRANCH

    cond = jnp.zeros((8, D), F32).at[:B].set(c).at[B].set(c_ctx)
    mods = _ada_mod(cond, w_ada, b_ada)

    rope_x = _rope_tables(S // GRID_W)
    rope_c = (jnp.ones((Sc, LANE), F32), jnp.zeros((Sc, LANE), F32))
    ones_bd = jnp.asarray(np.kron(np.eye(BRANCH // HEAD_DIM), np.ones((HEAD_DIM, HEAD_DIM))), F32).astype(BF16)
    w0 = jnp.asarray(_channel_dft_matrix(), F32).astype(BF16)
    tile2 = lambda v: jnp.tile(v, LANE // HEAD_DIM * (BRANCH // LANE)).reshape(1, BRANCH)

    for l in range(depth):
        last = l == depth - 1
        lam_init = 0.8 - 0.6 * math.exp(-0.3 * l)
        mx = [m.reshape(B, 1, D) for m in jnp.split(mods[l, :B], 6, axis=-1)]
        mc = [jnp.broadcast_to(m.reshape(1, 1, D), (B, 1, D)) for m in jnp.split(mods[l, B], 6, axis=-1)]
        lw = dict(conv_w=conv_w[l], conv_b=conv_b[l].reshape(1, BRANCH),
                  conv_ln_g=conv_ln_g[l].reshape(1, BRANCH), conv_ln_b=conv_ln_b[l].reshape(1, BRANCH),
                  q_norm=tile2(q_norm[l]), k_norm=tile2(k_norm[l]))
        w_z, w_att, w_gate = (w_in[l][:, a:b].astype(BF16) for a, b in ((0, COL_GQ), (COL_GQ, GATE_COL0), (GATE_COL0, None)))
        wf, wc, wa, wd, wo = (w.astype(BF16) for w in (w_four[l], w_conv[l], w_gqa[l], w_diff[l], w_out[l]))
        w1, w3, w2 = (w.astype(BF16) for w in (w_ffn1[l], w_ffn3[l], w_ffn2[l]))
        gain_m, gain_f = norm_mix[l].reshape(1, D), norm_ffn[l].reshape(1, D)
        lam_vecs = [v[l].reshape(1, HEAD_DIM) for v in (lam_q1, lam_k1, lam_q2, lam_k2)]
        dn = diff_norm[l].reshape(1, LANE)
        fin = final_norm.reshape(1, D)

        zx, qgx, kg, vg, qdx, kd, vd = _inproj_qkv(x, gain_m, mx[0], mx[1], w_z, w_att, rope_x[0], rope_x[1],
                                                   lw["q_norm"], lw["k_norm"], ones_bd, _kv_buffers(B, S + Sc), 0)
        zc, qgc, kg, vg, qdc, kd, vd = _inproj_qkv(ctx, gain_m, mc[0], mc[1], w_z, w_att, rope_c[0], rope_c[1],
                                                   lw["q_norm"], lw["k_norm"], ones_bd, (kg, vg, kd, vd), S)
        zgx = _inproj(x, gain_m, mx[0], mx[1], w_gate)
        zgc = _inproj(ctx, gain_m, mc[0], mc[1], w_gate)
        fx, cvx = _mixer_branches(zx, lw, w0, True)
        fc, cvc = _mixer_branches(zc, lw, w0, False)

        ogx = _attention(qgx, kg, vg, "gqa", 0, S + Sc)
        odx = _attention(qdx, kd, vd, "diff", 0, S + Sc, lam_vecs, dn, lam_init)
        x = _merge(x, mx[2], fx, cvx, ogx, odx, zgx, wf, wc, wa, wd, wo)
        if not last:
            ogc = _attention(qgc, kg, vg, "gqa", S, Sc)
            odc = _attention(qdc, kd, vd, "diff", S, Sc, lam_vecs, dn, lam_init)
            ctx = _merge(ctx, mc[2], fc, cvc, ogc, odc, zgc, wf, wc, wa, wd, wo)
            ctx = _ffn(ctx, gain_f, mc[3], mc[4], mc[5], w1, w3, w2, fin, False)
        x = _ffn(x, gain_f, mx[3], mx[4], mx[5], w1, w3, w2, fin, last)
    return x
```

```python
import functools
import math

import numpy as np
import jax
import jax.numpy as jnp
from jax import lax
from jax.experimental import pallas as pl
from jax.experimental.pallas import tpu as pltpu

F32 = jnp.float32
BF16 = jnp.bfloat16

HEAD_DIM = 64
BRANCH = 512
GQA_GROUP = 4
GQA_KV_HEADS = 2
DIFF_VT_ROWS = 144
GQA_VT_ROWS = 80
DIFF_HEADS = 4
N_BRANCH = 4
GRID_W = 64
CONV_K = 31
ROPE_BASE = 10000.0
EPS = 1e-6
LN_EPS = 1e-5

LANE = 128
SUBLANE = 8
VMEM_LIMIT = 56 * 1024 * 1024

BLK_FOUR, BLK_CA, BLK_CG = range(3)
COL_GQ, COL_KV, COL_DQ, COL_DK, COL_DV, GATE_COL0 = 1536, 2048, 2304, 2816, 3328, 3840
FOURIER_NB = 16
CONV_HALO = 16


def _params(*sem):
    return pltpu.CompilerParams(dimension_semantics=sem, vmem_limit_bytes=VMEM_LIMIT)


def _silu(x):
    return x * jax.nn.sigmoid(x)


def _rms_mod(x, g, shift, scale):
    y = x * lax.rsqrt(jnp.mean(x * x, axis=-1, keepdims=True) + EPS) * g
    return y * (1.0 + scale) + shift


def _ada_kernel(c_ref, w_ref, b_ref, o_ref):
    a = _silu(c_ref[...])
    o_ref[...] = jnp.dot(a, w_ref[...], precision=lax.Precision.HIGHEST,
                         preferred_element_type=F32) + b_ref[...]


def _ada_mod(cond, w_ada, b_ada):
    L, D, N = w_ada.shape
    tn = 1536
    return pl.pallas_call(
        _ada_kernel,
        out_shape=jax.ShapeDtypeStruct((L, 8, N), F32),
        grid=(L, N // tn),
        in_specs=[pl.BlockSpec((8, D), lambda l, j: (0, 0)),
                  pl.BlockSpec((None, D, tn), lambda l, j: (l, 0, j)),
                  pl.BlockSpec((None, 1, tn), lambda l, j: (l, 0, j))],
        out_specs=pl.BlockSpec((None, 8, tn), lambda l, j: (l, 0, j)),
        compiler_params=_params("parallel", "parallel"),
        name="ada_mod",
    )(cond, w_ada, b_ada.reshape(L, 1, N))


def _inproj_kernel(x_ref, g_ref, sh_ref, sc_ref, w_ref, o_ref):
    h = _rms_mod(x_ref[...], g_ref[...], sh_ref[...], sc_ref[...])
    o_ref[...] = jnp.dot(h.astype(BF16), w_ref[...], preferred_element_type=F32).astype(o_ref.dtype)


def _inproj(x, gain, shift, scale, w):
    B, S, D = x.shape
    N = w.shape[1]
    tm = min(512, S)
    tn = 4096
    return pl.pallas_call(
        _inproj_kernel,
        out_shape=jax.ShapeDtypeStruct((B, S, N), BF16),
        grid=(N // tn, B, S // tm),
        in_specs=[pl.BlockSpec((None, tm, D), lambda j, b, i: (b, i, 0)),
                  pl.BlockSpec((1, D), lambda j, b, i: (0, 0)),
                  pl.BlockSpec((None, 1, D), lambda j, b, i: (b, 0, 0)),
                  pl.BlockSpec((None, 1, D), lambda j, b, i: (b, 0, 0)),
                  pl.BlockSpec((D, tn), lambda j, b, i: (0, j))],
        out_specs=pl.BlockSpec((None, tm, tn), lambda j, b, i: (b, i, j)),
        compiler_params=_params("parallel", "parallel", "parallel"),
        name="in_proj",
    )(x, gain, shift, scale, w)


def _inproj_qkv_kernel(x_ref, g_ref, sh_ref, sc_ref, wz_ref, wa_ref, cos_ref, sin_ref, qn_ref, kn_ref, bd_ref, *rest):
    z_ref, qg_ref, kg_ref, vg_ref, qd_ref, kd_ref, vd_ref = rest[-7:]
    h = _rms_mod(x_ref[...], g_ref[...], sh_ref[...], sc_ref[...]).astype(BF16)
    acc = jnp.dot(h, wa_ref[...], preferred_element_type=F32)
    slab = lambda col, width=BRANCH: acc[:, col - COL_GQ:col - COL_GQ + width]
    _qkv_store(slab(COL_GQ), slab(COL_DQ), slab(COL_DK), slab(COL_DV), slab(COL_KV, 2 * LANE),
               cos_ref[...], sin_ref[...], qn_ref[...], kn_ref[...], bd_ref[...],
               qg_ref, kg_ref, vg_ref, qd_ref, kd_ref, vd_ref)
    z_ref[...] = jnp.dot(h, wz_ref[...], preferred_element_type=F32).astype(z_ref.dtype)


def _kv_buffers(B, n_keys):
    vg_rows, vd_rows = GQA_KV_HEADS * GQA_VT_ROWS, DIFF_HEADS * DIFF_VT_ROWS
    shapes = ((B, n_keys, LANE), (B, vg_rows, n_keys), (B, n_keys, BRANCH), (B, vd_rows, n_keys))
    return tuple(jnp.zeros(s, BF16) for s in shapes)


def _inproj_qkv(x, gain, shift, scale, w_z, w_att, cosf, sins, qn, kn, ones_bd, kv_bufs, key_start):
    B, S, D = x.shape
    n_keys = kv_bufs[0].shape[1]
    tm = min(512, S)
    assert key_start % tm == 0
    k0 = key_start // tm
    const = lambda shape: pl.BlockSpec(shape, lambda b, i: (0, 0))
    per_b = pl.BlockSpec((None, 1, D), lambda b, i: (b, 0, 0))
    tab = pl.BlockSpec((tm, LANE), lambda b, i: (i, 0))
    rows = lambda width: pl.BlockSpec((None, tm, width), lambda b, i: (b, i, 0))
    q_cols = lambda chans: pl.BlockSpec((None, chans, tm), lambda b, i: (b, 0, i))
    k_rows = lambda width: pl.BlockSpec((None, tm, width), lambda b, i: (b, k0 + i, 0))
    v_cols = lambda chans: pl.BlockSpec((None, chans, tm), lambda b, i: (b, 0, k0 + i))
    vg_rows, vd_rows = GQA_KV_HEADS * GQA_VT_ROWS, DIFF_HEADS * DIFF_VT_ROWS
    shapes = [(B, S, 3 * BRANCH), (B, 2 * BRANCH, S), (B, n_keys, LANE), (B, vg_rows, n_keys),
              (B, 2 * BRANCH, S), (B, n_keys, BRANCH), (B, vd_rows, n_keys)]
    specs = [rows(3 * BRANCH), q_cols(2 * BRANCH), k_rows(LANE), v_cols(vg_rows),
             q_cols(2 * BRANCH), k_rows(BRANCH), v_cols(vd_rows)]
    in_specs = [rows(D), const((1, D)), per_b, per_b, const(w_z.shape), const(w_att.shape), tab, tab,
                const((1, BRANCH)), const((1, BRANCH)), const((BRANCH, BRANCH))]
    args = [x, gain, shift, scale, w_z, w_att, cosf, sins, qn, kn, ones_bd]
    aliases = {}
    for out_idx, buf in zip((2, 3, 5, 6), kv_bufs):
        aliases[len(args)] = out_idx
        in_specs.append(pl.BlockSpec(memory_space=pl.ANY))
        args.append(buf)
    return pl.pallas_call(
        _inproj_qkv_kernel,
        out_shape=[jax.ShapeDtypeStruct(s, BF16) for s in shapes],
        grid=(B, S // tm),
        in_specs=in_specs,
        out_specs=specs,
        input_output_aliases=aliases,
        compiler_params=_params("parallel", "parallel"),
        name="in_proj_qkv",
    )(*args)


def _ffn_kernel(x_ref, g_ref, sh_ref, sc_ref, gate_ref, w1_ref, w3_ref, w2_ref, fg_ref, o_ref, *, final):
    x = x_ref[...]
    h = _rms_mod(x, g_ref[...], sh_ref[...], sc_ref[...]).astype(BF16)
    a = jnp.dot(h, w1_ref[...], preferred_element_type=F32)
    b = jnp.dot(h, w3_ref[...], preferred_element_type=F32)
    u = (_silu(a) * b).astype(BF16)
    y = x + gate_ref[...] * jnp.dot(u, w2_ref[...], preferred_element_type=F32)
    if final:
        y = y * lax.rsqrt(jnp.mean(y * y, axis=-1, keepdims=True) + EPS) * fg_ref[...]
    o_ref[...] = y


def _ffn(x, gain, shift, scale, gate, w1, w3, w2, final_gain, final):
    B, S, D = x.shape
    H = w1.shape[1]
    tm = min(512, S)
    const = lambda shape: pl.BlockSpec(shape, lambda b, i: (0,) * len(shape), pipeline_mode=pl.Buffered(1))
    per_b = pl.BlockSpec((None, 1, D), lambda b, i: (b, 0, 0))
    return pl.pallas_call(
        functools.partial(_ffn_kernel, final=final),
        out_shape=jax.ShapeDtypeStruct((B, S, D), F32),
        grid=(B, S // tm),
        in_specs=[pl.BlockSpec((None, tm, D), lambda b, i: (b, i, 0)),
                  const((1, D)), per_b, per_b, per_b,
                  const((D, H)), const((D, H)), const((H, D)), const((1, D))],
        out_specs=pl.BlockSpec((None, tm, D), lambda b, i: (b, i, 0)),
        compiler_params=_params("parallel", "parallel"),
        name="ffn",
    )(x, gain, shift, scale, gate, w1, w3, w2, final_gain)


def _merge_kernel(x_ref, gm_ref, f_ref, c_ref, a_ref, d_ref, zg_ref,
                  wf_ref, wc_ref, wa_ref, wd_ref, wo_ref, o_ref):
    D = x_ref.shape[-1]
    acc = None
    for b, (br, w) in enumerate(((f_ref, wf_ref), (c_ref, wc_ref), (a_ref, wa_ref), (d_ref, wd_ref))):
        y = jnp.dot(br[...], w[...], preferred_element_type=F32)
        gate = jax.nn.sigmoid(zg_ref[:, b * D:(b + 1) * D].astype(F32))
        acc = gate * y if acc is None else acc + gate * y
    out = jnp.dot(acc.astype(BF16), wo_ref[...], preferred_element_type=F32)
    o_ref[...] = x_ref[...] + gm_ref[...] * out


def _merge(x, gm, f, cv, og, od, zg, wf, wc, wa, wd, wo):
    B, S, D = x.shape
    tm = min(512, S)
    row = lambda width: pl.BlockSpec((None, tm, width), lambda b, i: (b, i, 0))
    const = lambda shape: pl.BlockSpec(shape, lambda b, i: (0, 0))
    return pl.pallas_call(
        _merge_kernel,
        out_shape=jax.ShapeDtypeStruct((B, S, D), F32),
        grid=(B, S // tm),
        in_specs=[row(D), pl.BlockSpec((None, 1, D), lambda b, i: (b, 0, 0)),
                  row(BRANCH), row(BRANCH), row(BRANCH), row(BRANCH),
                  pl.BlockSpec((None, tm, N_BRANCH * D), lambda b, i: (b, i, 0)),
                  const((BRANCH, D)), const((BRANCH, D)), const((BRANCH, D)), const((BRANCH, D)),
                  const((D, D))],
        out_specs=row(D),
        compiler_params=_params("parallel", "parallel"),
        name="merge",
    )(x, gm, f, cv, og, od, zg, wf, wc, wa, wd, wo)


def _conv_kernel(a_ref, g_ref, ap_ref, gp_ref, an_ref, gn_ref, w_ref, b_ref, lg_ref, lb_ref, o_ref, u_s,
                 *, ts, chunk):
    i = pl.program_id(1)
    n = pl.num_programs(1)

    def glu(a, g):
        return a.astype(F32) * jax.nn.sigmoid(g.astype(F32))

    u_s[CONV_HALO:CONV_HALO + ts, :] = glu(a_ref[...], g_ref[...])
    u_s[0:CONV_HALO, :] = jnp.where(i > 0, glu(ap_ref[...], gp_ref[...]), 0.0)
    u_s[CONV_HALO + ts:2 * CONV_HALO + ts, :] = jnp.where(i < n - 1, glu(an_ref[...], gn_ref[...]), 0.0)
    pad = CONV_K // 2

    def body(c, carry):
        r0 = pl.multiple_of(c * chunk, chunk)
        win = u_s[pl.ds(r0, chunk + 2 * CONV_HALO), :]
        rows = chunk + 2 * CONV_HALO
        acc = jnp.zeros((chunk // SUBLANE, SUBLANE, BRANCH), F32) + b_ref[...]
        for r in range(SUBLANE):
            taps = [j for j in range(CONV_K) if (CONV_HALO - pad + j) % SUBLANE == r]
            shifted = win if r == 0 else pltpu.roll(win, rows - r, 0)
            for j in taps:
                base = CONV_HALO - pad + j - r
                tap = shifted[base:base + chunk, :].reshape(chunk // SUBLANE, SUBLANE, BRANCH)
                acc = acc + w_ref[j] * tap
        acc = acc.reshape(chunk, BRANCH)
        mu = jnp.mean(acc, axis=-1, keepdims=True)
        d = acc - mu
        var = jnp.mean(d * d, axis=-1, keepdims=True)
        y = d * lax.rsqrt(var + LN_EPS) * lg_ref[...] + lb_ref[...]
        o_ref[pl.ds(r0, chunk), :] = _silu(y).astype(o_ref.dtype)
        return carry

    lax.fori_loop(0, ts // chunk, body, 0)


def _conv_branch(z, w, b, ln_g, ln_b):
    B, S, _ = z.shape
    ts = min(512, S)
    chunk = min(256, ts)
    hb = ts // CONV_HALO
    last = S // CONV_HALO - 1
    cur = lambda blk: pl.BlockSpec((None, ts, BRANCH), lambda b_, i: (b_, i, blk))
    prev = lambda blk: pl.BlockSpec((None, CONV_HALO, BRANCH),
                                    lambda b_, i: (b_, jnp.maximum(i * hb - 1, 0), blk))
    nxt = lambda blk: pl.BlockSpec((None, CONV_HALO, BRANCH),
                                   lambda b_, i: (b_, jnp.minimum((i + 1) * hb, last), blk))
    const = lambda shape: pl.BlockSpec(shape, lambda b_, i: (0, 0))
    return pl.pallas_call(
        functools.partial(_conv_kernel, ts=ts, chunk=chunk),
        out_shape=jax.ShapeDtypeStruct((B, S, BRANCH), BF16),
        grid=(B, S // ts),
        in_specs=[cur(BLK_CA), cur(BLK_CG), prev(BLK_CA), prev(BLK_CG), nxt(BLK_CA), nxt(BLK_CG),
                  pl.BlockSpec((CONV_K, SUBLANE, BRANCH), lambda b_, i: (0, 0, 0)),
                  const((1, BRANCH)), const((1, BRANCH)), const((1, BRANCH))],
        out_specs=pl.BlockSpec((None, ts, BRANCH), lambda b_, i: (b_, i, 0)),
        scratch_shapes=[pltpu.VMEM((ts + 2 * CONV_HALO, BRANCH), F32)],
        compiler_params=_params("parallel", "parallel"),
        name="conv_branch",
    )(z, z, z, z, z, z, jnp.broadcast_to(w[:, None, :], (CONV_K, SUBLANE, BRANCH)), b, ln_g, ln_b)


def _head_rms(x, ones_bd):
    x2 = x * x
    hi = x2.astype(BF16)
    lo = (x2 - hi.astype(F32)).astype(BF16)
    ssum = (jnp.dot(hi, ones_bd, preferred_element_type=F32)
            + jnp.dot(lo, ones_bd, preferred_element_type=F32))
    return x * lax.rsqrt(ssum * (1.0 / HEAD_DIM) + EPS)


def _rope(x, cosf, sins):
    width = x.shape[1]
    reps = width // LANE
    c = jnp.concatenate([cosf] * reps, axis=1) if reps > 1 else cosf
    s = jnp.concatenate([sins] * reps, axis=1) if reps > 1 else sins
    lane = lax.broadcasted_iota(jnp.int32, x.shape, 1)
    first_half = (lane & (HEAD_DIM - 1)) < HEAD_DIM // 2
    partner = jnp.where(first_half, pltpu.roll(x, width - HEAD_DIM // 2, 1), pltpu.roll(x, HEAD_DIM // 2, 1))
    return x * c + partner * s


def _qkv_store(gq, dq, dk, dv, kvz, cosf, sins, qn, kn, bd, qg_ref, kg_ref, vg_ref, qd_ref, kd_ref, vd_ref):
    scale = HEAD_DIM ** -0.5 * math.log2(math.e)
    ts = gq.shape[0]

    half = jnp.zeros((HEAD_DIM, ts), qg_ref.dtype)
    q = _rope(_head_rms(gq, bd) * qn, cosf, sins) * scale
    for p in range(BRANCH // LANE):
        qt = q[:, p * LANE:(p + 1) * LANE].T
        for e in range(2):
            h = 2 * p + e
            kvh = h // GQA_GROUP
            base = h * LANE
            qg_ref[base + kvh * HEAD_DIM:base + (kvh + 1) * HEAD_DIM, :] = (
                qt[e * HEAD_DIM:(e + 1) * HEAD_DIM, :].astype(qg_ref.dtype))
            qg_ref[base + (1 - kvh) * HEAD_DIM:base + (2 - kvh) * HEAD_DIM, :] = half

    k = _rope(_head_rms(kvz[:, :LANE], bd[:LANE, :LANE]) * kn[:, :LANE], cosf, sins)
    kg_ref[...] = k.astype(kg_ref.dtype)
    vt = kvz[:, LANE:2 * LANE].T
    row = lax.broadcasted_iota(jnp.int32, (GQA_VT_ROWS - HEAD_DIM, ts), 0)
    tail = jnp.where(row == 0, 1.0, 0.0).astype(vg_ref.dtype)
    for g in range(GQA_KV_HEADS):
        vg_ref[g * GQA_VT_ROWS:g * GQA_VT_ROWS + HEAD_DIM, :] = vt[g * HEAD_DIM:(g + 1) * HEAD_DIM, :].astype(vg_ref.dtype)
        vg_ref[g * GQA_VT_ROWS + HEAD_DIM:(g + 1) * GQA_VT_ROWS, :] = tail

    qd = _rope(dq, cosf, sins) * scale
    for h in range(DIFF_HEADS):
        qt = qd[:, h * LANE:(h + 1) * LANE].T.astype(qd_ref.dtype)
        base = 2 * h * LANE
        qd_ref[base:base + HEAD_DIM, :] = qt[:HEAD_DIM, :]
        qd_ref[base + HEAD_DIM:base + LANE, :] = half
        qd_ref[base + LANE:base + LANE + HEAD_DIM, :] = half
        qd_ref[base + LANE + HEAD_DIM:base + 2 * LANE, :] = qt[HEAD_DIM:, :]
        vd_ref[h * DIFF_VT_ROWS:h * DIFF_VT_ROWS + LANE, :] = (
            dv[:, h * LANE:(h + 1) * LANE].T.astype(vd_ref.dtype))
        vd_ref[h * DIFF_VT_ROWS + LANE:(h + 1) * DIFF_VT_ROWS, :] = tail
    kd_ref[...] = _rope(dk, cosf, sins).astype(kd_ref.dtype)


def _attn_kernel(*refs, sets, tq, tk, mode, lam_init):
    if mode == "diff":
        q_ref, k_ref, v_ref, l1q, l1k, l2q, l2k, dn_ref, o_ref, q_s, acc_s = refs
    else:
        q_ref, k_ref, v_ref, o_ref, q_s, acc_s = refs
    n_groups, vrows = acc_s.shape[0], acc_s.shape[1]
    n_tiles = k_ref.shape[0] // tk
    width = sets * tq
    for g in range(n_groups):
        for r in range(sets):
            q_s[g, :, r * tq:(r + 1) * tq] = q_ref[(g * sets + r) * LANE:(g * sets + r + 1) * LANE, :]

    def k_tile(g, rows):
        return k_ref[rows, :] if mode == "gqa" else k_ref[rows, g * LANE:(g + 1) * LANE]

    def finish(g):
        if mode == "diff":
            ot = acc_s[g, :LANE, :] / acc_s[g, LANE:LANE + 1, :]
            o = [ot[:, r * tq:(r + 1) * tq].T for r in range(sets)]
            lam = (jnp.exp(jnp.sum(l1q[...] * l1k[...], axis=1, keepdims=True))
                   - jnp.exp(jnp.sum(l2q[...] * l2k[...], axis=1, keepdims=True)) + lam_init)
            d = o[0] - lam * o[1]
            d = d * lax.rsqrt(jnp.mean(d * d, axis=1, keepdims=True) + EPS) * dn_ref[...]
            o_ref[:, g * LANE:(g + 1) * LANE] = (d * (1.0 - lam_init)).astype(o_ref.dtype)
        else:
            ot = acc_s[g, :HEAD_DIM, :] / acc_s[g, HEAD_DIM:HEAD_DIM + 1, :]
            for pair in range(sets // 2):
                both = jnp.concatenate([ot[:, (2 * pair) * tq:(2 * pair + 1) * tq],
                                        ot[:, (2 * pair + 1) * tq:(2 * pair + 2) * tq]], axis=0)
                col = (g * (sets // 2) + pair) * LANE
                o_ref[:, col:col + LANE] = both.T.astype(o_ref.dtype)

    m_run = [None] * n_groups
    for t in range(n_tiles):
        for g in range(n_groups):
            st = jnp.dot(k_tile(g, slice(t * tk, (t + 1) * tk)), q_s[g], preferred_element_type=F32)
            m_tile = jnp.max(st, axis=0, keepdims=True)
            pt = jnp.exp2(st - (m_tile if t == 0 else m_run[g]))
            pv = jnp.dot(v_ref[g * vrows:(g + 1) * vrows, t * tk:(t + 1) * tk], pt.astype(BF16),
                         preferred_element_type=F32)
            if t == 0:
                m_run[g] = m_tile
                acc_s[g] = pv
            else:
                m_new = jnp.maximum(m_run[g], m_tile)
                alpha = jnp.exp2(m_run[g] - m_new)
                acc_s[g] = (acc_s[g] + pv) * alpha
                m_run[g] = m_new
    for g in range(n_groups):
        finish(g)
    overflowed = jnp.max(jnp.where(jnp.isfinite(acc_s[...]), 0.0, 1.0)) > 0.0

    @pl.when(overflowed)
    def _exact():
        for g in range(n_groups):
            def body(t, m_old, g=g):
                st = jnp.dot(k_tile(g, pl.ds(pl.multiple_of(t * tk, tk), tk)), q_s[g], preferred_element_type=F32)
                m_new = jnp.maximum(m_old, jnp.max(st, axis=0, keepdims=True))
                alpha = jnp.exp2(m_old - m_new)
                pt = jnp.exp2(st - m_new)
                v_tile = v_ref[g * vrows:(g + 1) * vrows, pl.ds(pl.multiple_of(t * tk, LANE), tk)]
                acc_s[g] = alpha * acc_s[g] + jnp.dot(v_tile, pt.astype(BF16), preferred_element_type=F32)
                return m_new

            acc_s[g] = jnp.zeros(acc_s.shape[1:], F32)
            lax.fori_loop(0, n_tiles, body, jnp.full((1, width), -jnp.inf, F32))
            finish(g)


def _attention(qt, k, vt, mode, key_start, NK, lam_vecs=None, diff_norm=None, lam_init=0.0):
    B, _, S = qt.shape
    assert key_start % NK == 0
    kb = key_start // NK
    tq = min(256, S)
    tk = 768
    tk = tk if NK % tk == 0 else 256
    per_step = 2 if mode == "diff" else 1
    if mode == "diff":
        groups, sets, out_w, vrows = DIFF_HEADS, 2, LANE, DIFF_VT_ROWS
        k_spec = pl.BlockSpec((None, NK, per_step * LANE), lambda b, g, i: (b, kb, g))
    else:
        groups, sets, out_w, vrows = GQA_KV_HEADS, GQA_GROUP, GQA_GROUP * HEAD_DIM, GQA_VT_ROWS
        k_spec = pl.BlockSpec((None, NK, LANE), lambda b, g, i: (b, kb, 0))
    in_specs = [pl.BlockSpec((None, per_step * sets * LANE, tq), lambda b, g, i: (b, g, i)),
                k_spec,
                pl.BlockSpec((None, per_step * vrows, NK), lambda b, g, i: (b, g, kb))]
    args = [qt, k, vt]
    if mode == "diff":
        in_specs += [pl.BlockSpec((1, HEAD_DIM), lambda b, g, i: (0, 0))] * 4
        in_specs += [pl.BlockSpec((1, LANE), lambda b, g, i: (0, 0))]
        args += list(lam_vecs) + [diff_norm]
    return pl.pallas_call(
        functools.partial(_attn_kernel, sets=sets, tq=tq, tk=tk, mode=mode, lam_init=lam_init),
        out_shape=jax.ShapeDtypeStruct((B, S, BRANCH), BF16),
        grid=(B, groups // per_step, S // tq),
        in_specs=in_specs,
        out_specs=pl.BlockSpec((None, tq, per_step * out_w), lambda b, g, i: (b, i, g)),
        scratch_shapes=[pltpu.VMEM((per_step, LANE, sets * tq), BF16),
                        pltpu.VMEM((per_step, vrows, sets * tq), F32)],
        compiler_params=_params("parallel", "parallel", "parallel"),
        name="attn_" + mode,
    )(*args)


def _dft_cos_sin(n):
    idx = np.arange(n)
    ang = 2.0 * np.pi * ((idx[:, None] * idx[None, :]) % n) / n
    return np.cos(ang), np.sin(ang)


def _channel_dft_matrix():
    c, s = _dft_cos_sin(GRID_W)
    groups = BRANCH // GRID_W
    eye = np.eye(groups)
    return np.concatenate([np.kron(eye, c), -np.kron(eye, s)], axis=1) / math.sqrt(GRID_W)


def _fourier_a_kernel(z_ref, w0_ref, d1_ref, tc_ref, ts_ref, o_ref):
    n1, nb, _ = z_ref.shape
    zt = pltpu.einshape("kjm->jkm", z_ref[...])
    u = jnp.dot(zt.reshape(nb * n1, BRANCH), w0_ref[...], preferred_element_type=F32)
    u = u.astype(BF16).reshape(nb, n1, 2 * BRANCH)
    reps = BRANCH // LANE
    out_r, out_i = [], []
    for j in range(nb):
        p = jnp.dot(d1_ref[...], u[j], preferred_element_type=F32)
        vr = p[:n1, :BRANCH] + p[n1:, BRANCH:]
        vi = p[:n1, BRANCH:] - p[n1:, :BRANCH]
        c = jnp.concatenate([tc_ref[j]] * reps, axis=1)
        s = jnp.concatenate([ts_ref[j]] * reps, axis=1)
        out_r.append(vr * c + vi * s)
        out_i.append(vi * c - vr * s)
    o_ref[0] = pltpu.einshape("jkm->kjm", jnp.stack(out_r).astype(o_ref.dtype))
    o_ref[1] = pltpu.einshape("jkm->kjm", jnp.stack(out_i).astype(o_ref.dtype))


def _fourier_b_kernel(v_ref, c3_ref, s3_ref, o_ref):
    nb = v_ref.shape[1]
    outs = [jnp.dot(c3_ref[...], v_ref[0, j], preferred_element_type=F32)
            + jnp.dot(s3_ref[...], v_ref[1, j], preferred_element_type=F32) for j in range(nb)]
    o_ref[...] = pltpu.einshape("jkm->kjm", jnp.stack(outs).astype(o_ref.dtype))


def _bmm_kernel(a_ref, x_ref, o_ref):
    o_ref[...] = jnp.dot(a_ref[...], x_ref[...], preferred_element_type=F32).astype(o_ref.dtype)


def _bmm(a, x, tn):
    M, K = a.shape
    B, _, N = x.shape
    tn = min(tn, N)
    return pl.pallas_call(
        _bmm_kernel,
        out_shape=jax.ShapeDtypeStruct((B, M, N), BF16),
        grid=(B, N // tn),
        in_specs=[pl.BlockSpec((M, K), lambda b, j: (0, 0)),
                  pl.BlockSpec((None, K, tn), lambda b, j: (b, 0, j))],
        out_specs=pl.BlockSpec((None, M, tn), lambda b, j: (b, 0, j)),
        compiler_params=_params("parallel", "parallel"),
        name="bmm",
    )(a, x)


def _slab_matmul_kernel(z_ref, w_ref, o_ref):
    o_ref[...] = jnp.dot(z_ref[...], w_ref[...], preferred_element_type=F32).astype(o_ref.dtype)


def _slab_matmul(z, blk, w):
    B, S, _ = z.shape
    K, N = w.shape
    ts = min(256, S)
    return pl.pallas_call(
        _slab_matmul_kernel,
        out_shape=jax.ShapeDtypeStruct((B, S, N), BF16),
        grid=(B, S // ts),
        in_specs=[pl.BlockSpec((None, ts, K), lambda b, i: (b, i, blk)),
                  pl.BlockSpec((K, N), lambda b, i: (0, 0))],
        out_specs=pl.BlockSpec((None, ts, N), lambda b, i: (b, i, 0)),
        compiler_params=_params("parallel", "parallel"),
        name="slab_matmul",
    )(z, w)


def _fourier_long(z, w0):
    B, S, _ = z.shape
    n2 = GRID_W
    n1 = S // n2
    nb = FOURIER_NB
    assert n1 % nb == 0 and n2 % nb == 0
    c1, s1 = _dft_cos_sin(n1)
    d1 = jnp.asarray(np.concatenate([c1, s1], axis=0) / math.sqrt(n1), F32).astype(BF16)
    ang = 2.0 * np.pi * (np.arange(n2)[:, None] * np.arange(n1)[None, :]) / S
    twc = jnp.asarray(np.repeat(np.cos(ang)[:, :, None], LANE, axis=2), F32)
    tws = jnp.asarray(np.repeat(np.sin(ang)[:, :, None], LANE, axis=2), F32)
    v = pl.pallas_call(
        _fourier_a_kernel,
        out_shape=jax.ShapeDtypeStruct((B, 2, n1, n2, BRANCH), BF16),
        grid=(B, n2 // nb),
        in_specs=[pl.BlockSpec((None, n1, nb, BRANCH), lambda b, j: (b, 0, j, BLK_FOUR)),
                  pl.BlockSpec((BRANCH, 2 * BRANCH), lambda b, j: (0, 0)),
                  pl.BlockSpec((2 * n1, n1), lambda b, j: (0, 0)),
                  pl.BlockSpec((nb, n1, LANE), lambda b, j: (j, 0, 0)),
                  pl.BlockSpec((nb, n1, LANE), lambda b, j: (j, 0, 0))],
        out_specs=pl.BlockSpec((None, 2, n1, nb, BRANCH), lambda b, j: (b, 0, 0, j, 0)),
        compiler_params=_params("parallel", "parallel"),
        name="fourier_a",
    )(z.reshape(B, n1, n2, z.shape[-1]), w0, d1, twc, tws)
    c3, s3 = _dft_cos_sin(n2)
    c3 = jnp.asarray(c3 / math.sqrt(n2), F32).astype(BF16)
    s3 = jnp.asarray(s3 / math.sqrt(n2), F32).astype(BF16)
    f = pl.pallas_call(
        _fourier_b_kernel,
        out_shape=jax.ShapeDtypeStruct((B, n2, n1, BRANCH), BF16),
        grid=(B, n1 // nb),
        in_specs=[pl.BlockSpec((None, 2, nb, n2, BRANCH), lambda b, i: (b, 0, i, 0, 0)),
                  pl.BlockSpec((n2, n2), lambda b, i: (0, 0)),
                  pl.BlockSpec((n2, n2), lambda b, i: (0, 0))],
        out_specs=pl.BlockSpec((None, n2, nb, BRANCH), lambda b, i: (b, 0, i, 0)),
        compiler_params=_params("parallel", "parallel"),
        name="fourier_b",
    )(v, c3, s3)
    return f.reshape(B, S, BRANCH)


def _fourier_short(z, w0):
    B, S, _ = z.shape
    u = _slab_matmul(z, BLK_FOUR, w0)
    ust = jnp.concatenate([u[:, :, :BRANCH], u[:, :, BRANCH:]], axis=1)
    c, s = _dft_cos_sin(S)
    dl = jnp.asarray(np.concatenate([c, s], axis=1) / math.sqrt(S), F32).astype(BF16)
    return _bmm(dl, ust, BRANCH)


def _rope_tables(n_rows):
    half = HEAD_DIM // 4
    inv = ROPE_BASE ** (-np.arange(0, HEAD_DIM // 2, 2, dtype=np.float64) / (HEAD_DIM // 2))
    row = np.repeat(np.arange(n_rows, dtype=np.float64), GRID_W)
    col = np.tile(np.arange(GRID_W, dtype=np.float64), n_rows)
    ang = np.concatenate([row[:, None] * inv, col[:, None] * inv], axis=-1)
    assert ang.shape[1] == 2 * half
    cos, sin = np.cos(ang), np.sin(ang)
    cosf = np.tile(np.concatenate([cos, cos], axis=1), (1, LANE // HEAD_DIM))
    sins = np.tile(np.concatenate([-sin, sin], axis=1), (1, LANE // HEAD_DIM))
    return jnp.asarray(cosf, F32), jnp.asarray(sins, F32)


def _mixer_branches(z, lw, w0, long_seq):
    f = _fourier_long(z, w0) if long_seq else _fourier_short(z, w0)
    cv = _conv_branch(z, lw["conv_w"], lw["conv_b"], lw["conv_ln_g"], lw["conv_ln_b"])
    return f, cv


def kernel(x, c, ctx, c_ctx, w_ada, b_ada, norm_mix, w_in, w_four, conv_w, conv_b, conv_ln_g, conv_ln_b, w_conv, q_norm, k_norm, w_gqa, lam_q1, lam_k1, lam_q2, lam_k2, diff_norm, w_diff, w_out, norm_ffn, w_ffn1, w_ffn3, w_ffn2, final_norm):
    B, S, D = x.shape
    Sc = ctx.shape[1]
    depth = w_ada.shape[0]
    assert B + 1 <= 8 and S % GRID_W == 0 and D == 2 * BRANCH

    cond = jnp.zeros((8, D), F32).at[:B].set(c).at[B].set(c_ctx)
    mods = _ada_mod(cond, w_ada, b_ada)

    rope_x = _rope_tables(S // GRID_W)
    rope_c = (jnp.ones((Sc, LANE), F32), jnp.zeros((Sc, LANE), F32))
    ones_bd = jnp.asarray(np.kron(np.eye(BRANCH // HEAD_DIM), np.ones((HEAD_DIM, HEAD_DIM))), F32).astype(BF16)
    w0 = jnp.asarray(_channel_dft_matrix(), F32).astype(BF16)
    tile2 = lambda v: jnp.tile(v, LANE // HEAD_DIM * (BRANCH // LANE)).reshape(1, BRANCH)

    for l in range(depth):
        last = l == depth - 1
        lam_init = 0.8 - 0.6 * math.exp(-0.3 * l)
        mx = [m.reshape(B, 1, D) for m in jnp.split(mods[l, :B], 6, axis=-1)]
        mc = [jnp.broadcast_to(m.reshape(1, 1, D), (B, 1, D)) for m in jnp.split(mods[l, B], 6, axis=-1)]
        lw = dict(conv_w=conv_w[l], conv_b=conv_b[l].reshape(1, BRANCH),
                  conv_ln_g=conv_ln_g[l].reshape(1, BRANCH), conv_ln_b=conv_ln_b[l].reshape(1, BRANCH),
                  q_norm=tile2(q_norm[l]), k_norm=tile2(k_norm[l]))
        w_z, w_att, w_gate = (w_in[l][:, a:b].astype(BF16) for a, b in ((0, COL_GQ), (COL_GQ, GATE_COL0), (GATE_COL0, None)))
        wf, wc, wa, wd, wo = (w.astype(BF16) for w in (w_four[l], w_conv[l], w_gqa[l], w_diff[l], w_out[l]))
        w1, w3, w2 = (w.astype(BF16) for w in (w_ffn1[l], w_ffn3[l], w_ffn2[l]))
        gain_m, gain_f = norm_mix[l].reshape(1, D), norm_ffn[l].reshape(1, D)
        lam_vecs = [v[l].reshape(1, HEAD_DIM) for v in (lam_q1, lam_k1, lam_q2, lam_k2)]
        dn = diff_norm[l].reshape(1, LANE)
        fin = final_norm.reshape(1, D)

        zx, qgx, kg, vg, qdx, kd, vd = _inproj_qkv(x, gain_m, mx[0], mx[1], w_z, w_att, rope_x[0], rope_x[1],
                                                   lw["q_norm"], lw["k_norm"], ones_bd, _kv_buffers(B, S + Sc), 0)
        zc, qgc, kg, vg, qdc, kd, vd = _inproj_qkv(ctx, gain_m, mc[0], mc[1], w_z, w_att, rope_c[0], rope_c[1],
                                                   lw["q_norm"], lw["k_norm"], ones_bd, (kg, vg, kd, vd), S)
        zgx = _inproj(x, gain_m, mx[0], mx[1], w_gate)
        zgc = _inproj(ctx, gain_m, mc[0], mc[1], w_gate)
        fx, cvx = _mixer_branches(zx, lw, w0, True)
        fc, cvc = _mixer_branches(zc, lw, w0, False)

        ogx = _attention(qgx, kg, vg, "gqa", 0, S + Sc)
        odx = _attention(qdx, kd, vd, "diff", 0, S + Sc, lam_vecs, dn, lam_init)
        x = _merge(x, mx[2], fx, cvx, ogx, odx, zgx, wf, wc, wa, wd, wo)
        if not last:
            ogc = _attention(qgc, kg, vg, "gqa", S, Sc)
            odc = _attention(qdc, kd, vd, "diff", S, Sc, lam_vecs, dn, lam_init)
            ctx = _merge(ctx, mc[2], fc, cvc, ogc, odc, zgc, wf, wc, wa, wd, wo)
            ctx = _ffn(ctx, gain_f, mc[3], mc[4], mc[5], w1, w3, w2, fin, False)
        x = _ffn(x, gain_f, mx[3], mx[4], mx[5], w1, w3, w2, fin, last)
    return x
```
